```python
import math
import jax
import jax.numpy as jnp
from jax import lax
import numpy as np

D_MODEL = 1024
BATCH = 8
SEQ = 4096
DEPTH = 4

GRID_W = 64
CTX_LEN = 256
N_BRANCH = 4
BRANCH_W = D_MODEL // 4
NA_HEADS = 4
NA_HD = BRANCH_W // NA_HEADS
NA_KH = 8
NA_KW = 16
MLA_HEADS = 4
MLA_Q_LORA = D_MODEL // 4
MLA_KV_LORA = D_MODEL // 8
MLA_NOPE = 64
MLA_ROPE = 32
MLA_V = BRANCH_W // MLA_HEADS
S5_GROUP_CH = 16
S5_GROUPS = BRANCH_W // S5_GROUP_CH
S5_STATE = 64
DIFF_HEADS = 4
DIFF_HD = BRANCH_W // (2 * DIFF_HEADS)
D_FF = 3584
N_EXPERTS = 8
TOP_K = 2
EXPERT_BLOCK = 128
N_DENSE = (DEPTH + 1) // 2
N_MOE = DEPTH // 2
Q_BLOCK = 128
ROPE_BASE = 10000.0
LN_EPS = 1e-5
RMS_EPS = 1e-6
ALPHA = (2 * DEPTH) ** 0.25
BETA = (8 * DEPTH) ** -0.25
NA_SCALE = NA_HD ** -0.5
MLA_SCALE = (MLA_NOPE + MLA_ROPE) ** -0.5
DIFF_SCALE = DIFF_HD ** -0.5
IN_SIZES = (BRANCH_W, BRANCH_W, BRANCH_W, MLA_Q_LORA, MLA_KV_LORA, MLA_ROPE, BRANCH_W, 2 * DIFF_HEADS * DIFF_HD, 2 * DIFF_HEADS * DIFF_HD, 2 * DIFF_HEADS * DIFF_HD)
IN_COLS = sum(IN_SIZES)

kernel_name = 'hybrid_diffusion_trunk'


def _split_cols(z, sizes):
    parts, start = [], 0
    for size in sizes:
        parts.append(z[..., start:start + size])
        start += size
    return parts


def _heads(z, n_heads):
    b, t, _ = z.shape
    return z.reshape(b, t, n_heads, -1).transpose(0, 2, 1, 3)


def _merge_heads(z):
    b, h, t, dh = z.shape
    return z.transpose(0, 2, 1, 3).reshape(b, t, h * dh)


def _layer_norm(z, g, b):
    zf = z.astype(jnp.float32)
    zc = zf - jnp.mean(zf, axis=-1, keepdims=True)
    var = jnp.mean(zc * zc, axis=-1, keepdims=True)
    return (zc * lax.rsqrt(var + LN_EPS) * g.astype(jnp.float32) + b.astype(jnp.float32)).astype(z.dtype)


def _rms_norm(z, g):
    zf = z.astype(jnp.float32)
    return (zf * lax.rsqrt(jnp.mean(zf * zf, axis=-1, keepdims=True) + RMS_EPS) * g.astype(jnp.float32)).astype(z.dtype)


def _softmax(s):
    return jax.nn.softmax(s.astype(jnp.float32), axis=-1)


def _rope_2d(z, prow, pcol):
    half = z.shape[-1] // 2
    nf = half // 2
    inv = ROPE_BASE ** (-jnp.arange(nf, dtype=jnp.float32) / nf)

    def rot(za, p):
        ang = p.astype(jnp.float32)[:, None] * inv[None, :]
        cos, sin = jnp.cos(ang), jnp.sin(ang)
        z1 = za[..., :nf].astype(jnp.float32)
        z2 = za[..., nf:].astype(jnp.float32)
        return jnp.concatenate([z1 * cos - z2 * sin, z1 * sin + z2 * cos], axis=-1)

    return jnp.concatenate([rot(z[..., :half], prow), rot(z[..., half:], pcol)], axis=-1).astype(z.dtype)


def _sweep_query_blocks(fn, *qs):
    b, h, t, _ = qs[0].shape
    nb = t // Q_BLOCK
    blocks = tuple(q.reshape(b, h, nb, Q_BLOCK, q.shape[-1]).transpose(2, 0, 1, 3, 4) for q in qs)
    out = lax.map(lambda blk: fn(*blk), blocks)
    return out.transpose(1, 2, 0, 3, 4).reshape(b, h, t, out.shape[-1])


def _dense_attention(q, k, v, scale):
    p = _softmax(jnp.einsum('bhqd,bhkd->bhqk', q, k) * scale)
    return jnp.einsum('bhqk,bhkd->bhqd', p.astype(v.dtype), v)


def _neighbourhood_attention(q, k, v, kc, vc, rpb):
    b, h, t, dh = q.shape
    rows = t // GRID_W
    kh = min(NA_KH, rows)
    kw = NA_KW
    q = q.reshape(b, h, rows, GRID_W, dh)
    k = k.reshape(b, h, rows, GRID_W, dh)
    v = v.reshape(b, h, rows, GRID_W, dh)
    col = jnp.arange(GRID_W)
    col_idx = jnp.clip(col - kw // 2, 0, GRID_W - kw)[:, None] + jnp.arange(kw)[None, :]
    col_off = col_idx - col[:, None] + (NA_KW - 1)

    def row_block(r):
        r0 = jnp.clip(r - kh // 2, 0, rows - kh)
        qr = lax.dynamic_index_in_dim(q, r, axis=2, keepdims=False)
        kg = lax.dynamic_slice_in_dim(k, r0, kh, axis=2)[:, :, :, col_idx]
        vg = lax.dynamic_slice_in_dim(v, r0, kh, axis=2)[:, :, :, col_idx]
        row_off = r0 + jnp.arange(kh) - r + (NA_KH - 1)
        bias = rpb[:, row_off][:, :, col_off].transpose(0, 2, 1, 3)
        s_loc = (jnp.einsum('bhqd,bhrqcd->bhqrc', qr, kg) * NA_SCALE + bias[None]).reshape(b, h, GRID_W, kh * kw)
        s_ctx = jnp.einsum('bhqd,bhld->bhql', qr, kc) * NA_SCALE
        p = _softmax(jnp.concatenate([s_loc, s_ctx], axis=-1)).astype(v.dtype)
        p_loc = p[..., :kh * kw].reshape(b, h, GRID_W, kh, kw)
        return jnp.einsum('bhqrc,bhrqcd->bhqd', p_loc, vg) + jnp.einsum('bhql,bhld->bhqd', p[..., kh * kw:], vc)

    out = lax.map(row_block, jnp.arange(rows))
    return out.transpose(1, 2, 0, 3, 4).reshape(b, h, t, dh)


def _mla_qkv(q_lat, kv_lat, k_rope, q_norm, kv_norm, w_uq, w_ukv, pos):
    b, t, _ = q_lat.shape
    q = (_rms_norm(q_lat, q_norm) @ w_uq).reshape(b, t, MLA_HEADS, MLA_NOPE + MLA_ROPE).transpose(0, 2, 1, 3)
    kv = (_rms_norm(kv_lat, kv_norm) @ w_ukv).reshape(b, t, MLA_HEADS, MLA_NOPE + MLA_V).transpose(0, 2, 1, 3)
    q_nope, q_rope = q[..., :MLA_NOPE], q[..., MLA_NOPE:]
    k_nope, v = kv[..., :MLA_NOPE], kv[..., MLA_NOPE:]
    if pos is not None:
        q_rope = _rope_2d(q_rope, *pos)
        k_rope = _rope_2d(k_rope, *pos)
    return q_nope, q_rope, k_nope, k_rope, v


def _mla_attend(qn, qr, kn, kr, v):
    s = jnp.einsum('bhqd,bhkd->bhqk', qn, kn) + jnp.einsum('bhqr,bkr->bhqk', qr, kr)
    p = _softmax(s * MLA_SCALE)
    return jnp.einsum('bhqk,bhkd->bhqd', p.astype(v.dtype), v)


def _diag_scan(a_bar, bu, h0, reverse):
    if h0 is not None:
        edge = bu.shape[1] - 1 if reverse else 0
        bu = bu.at[:, edge].add(a_bar * h0)
    a = jnp.broadcast_to(a_bar, bu.shape)

    def combine(e1, e2):
        a1, b1 = e1
        a2, b2 = e2
        return a1 * a2, a2 * b1 + b2

    return lax.associative_scan(combine, (a, bu), axis=1, reverse=reverse)[1]


def _s5_mixer(u, uc, lam_re, lam_im, log_dt, b_re, b_im, c_re, c_im, d_skip, w_glu, ctx_out):
    def groups(z):
        return z.astype(jnp.float32).reshape(z.shape[0], z.shape[1], S5_GROUPS, S5_GROUP_CH).astype(jnp.complex64)

    ug, ucg = groups(u), groups(uc)
    ys, ycs = [], []
    for direction in range(2):
        rev = direction == 1
        lam = lax.complex(lam_re[direction].astype(jnp.float32), lam_im[direction].astype(jnp.float32))
        dt = jnp.exp(log_dt[direction].astype(jnp.float32))[:, None]
        a_bar = jnp.exp(lam * dt)
        b_mat = lax.complex(b_re[direction].astype(jnp.float32), b_im[direction].astype(jnp.float32))
        b_bar = ((a_bar - 1.0) / lam)[..., None] * b_mat
        c_mat = lax.complex(c_re[direction].astype(jnp.float32), c_im[direction].astype(jnp.float32))
        st_c = _diag_scan(a_bar, jnp.einsum('gpc,btgc->btgp', b_bar, ucg), None, rev)
        h0 = st_c[:, 0] if rev else st_c[:, -1]
        st = _diag_scan(a_bar, jnp.einsum('gpc,btgc->btgp', b_bar, ug), h0, rev)
        ys.append(jnp.real(jnp.einsum('gcp,btgp->btgc', c_mat, st)))
        if ctx_out:
            ycs.append(jnp.real(jnp.einsum('gcp,btgp->btgc', c_mat, st_c)))

    def finish(y_dirs, z):
        y = (y_dirs[0] + y_dirs[1]).reshape(z.shape) + d_skip.astype(jnp.float32) * z.astype(jnp.float32)
        val, gate = jnp.split(y.astype(z.dtype) @ w_glu, 2, axis=-1)
        return val * jax.nn.sigmoid(gate)

    return finish(ys, u), (finish(ycs, uc) if ctx_out else None)


def _diff_qkv(zq, zk, zv, pos):
    b, t, _ = zq.shape
    q = zq.reshape(b, t, DIFF_HEADS, 2, DIFF_HD).transpose(0, 2, 3, 1, 4)
    k = zk.reshape(b, t, DIFF_HEADS, 2, DIFF_HD).transpose(0, 2, 3, 1, 4)
    if pos is not None:
        q = _rope_2d(q, *pos)
        k = _rope_2d(k, *pos)
    return q[:, :, 0], q[:, :, 1], k[:, :, 0], k[:, :, 1], _heads(zv, DIFF_HEADS)


def _diff_attend(q1, q2, k1, k2, v, lam):
    p1 = _softmax(jnp.einsum('bhqd,bhkd->bhqk', q1, k1) * DIFF_SCALE)
    p2 = _softmax(jnp.einsum('bhqd,bhkd->bhqk', q2, k2) * DIFF_SCALE)
    return jnp.einsum('bhqk,bhkd->bhqd', (p1 - lam * p2).astype(v.dtype), v)


def _diff_finish(o, subln, lam_init):
    return _merge_heads(_rms_norm(o, subln) * (1.0 - lam_init))


def _gated_merge(h, outs, w_branch, w_gate, b_gate):
    d = h.shape[-1]
    m = None
    for i, o in enumerate(outs):
        gate = jax.nn.sigmoid(h @ w_gate[:, i * d:(i + 1) * d] + b_gate[i * d:(i + 1) * d])
        term = gate * (o @ w_branch[i])
        m = term if m is None else m + term
    return m


def _token_mixing(h, hc, pos, w_in, na_rpb, mla_q_norm, mla_kv_norm, mla_w_uq, mla_w_ukv,
                  s5_lam_re, s5_lam_im, s5_log_dt, s5_b_re, s5_b_im, s5_c_re, s5_c_im, s5_d, s5_w_glu,
                  diff_lam_q1, diff_lam_k1, diff_lam_q2, diff_lam_k2, diff_subln,
                  w_branch, w_gate, b_gate, w_out, lam_init, ctx_out):
    na_q, na_k, na_v, m_q, m_kv, m_kr, s5_u, d_q, d_k, d_v = _split_cols(h @ w_in, IN_SIZES)
    na_qc, na_kc, na_vc, m_qc, m_kvc, m_krc, s5_uc, d_qc, d_kc, d_vc = _split_cols(hc @ w_in, IN_SIZES)
    outs, outs_c = [], []

    kac, vac = _heads(na_kc, NA_HEADS), _heads(na_vc, NA_HEADS)
    o = _neighbourhood_attention(_heads(na_q, NA_HEADS), _heads(na_k, NA_HEADS), _heads(na_v, NA_HEADS), kac, vac, na_rpb)
    outs.append(_merge_heads(o))
    if ctx_out:
        outs_c.append(_merge_heads(_dense_attention(_heads(na_qc, NA_HEADS), kac, vac, NA_SCALE)))

    qn, qr, kn, kr, vb = _mla_qkv(m_q, m_kv, m_kr, mla_q_norm, mla_kv_norm, mla_w_uq, mla_w_ukv, pos)
    qnc, qrc, knc, krc, vbc = _mla_qkv(m_qc, m_kvc, m_krc, mla_q_norm, mla_kv_norm, mla_w_uq, mla_w_ukv, None)
    kn_all = jnp.concatenate([kn, knc], axis=2)
    kr_all = jnp.concatenate([kr, krc], axis=1)
    vb_all = jnp.concatenate([vb, vbc], axis=2)
    o = _sweep_query_blocks(lambda a, r: _mla_attend(a, r, kn_all, kr_all, vb_all), qn, qr)
    outs.append(_merge_heads(o))
    if ctx_out:
        outs_c.append(_merge_heads(_mla_attend(qnc, qrc, knc, krc, vbc)))

    o, oc = _s5_mixer(s5_u, s5_uc, s5_lam_re, s5_lam_im, s5_log_dt, s5_b_re, s5_b_im, s5_c_re, s5_c_im, s5_d, s5_w_glu, ctx_out)
    outs.append(o)
    if ctx_out:
        outs_c.append(oc)

    lam = (jnp.exp(jnp.sum(diff_lam_q1.astype(jnp.float32) * diff_lam_k1.astype(jnp.float32)))
           - jnp.exp(jnp.sum(diff_lam_q2.astype(jnp.float32) * diff_lam_k2.astype(jnp.float32))) + lam_init)
    q1, q2, k1, k2, vd = _diff_qkv(d_q, d_k, d_v, pos)
    q1c, q2c, k1c, k2c, vdc = _diff_qkv(d_qc, d_kc, d_vc, None)
    k1a = jnp.concatenate([k1, k1c], axis=2)
    k2a = jnp.concatenate([k2, k2c], axis=2)
    vda = jnp.concatenate([vd, vdc], axis=2)
    o = _sweep_query_blocks(lambda a, bq: _diff_attend(a, bq, k1a, k2a, vda, lam), q1, q2)
    outs.append(_diff_finish(o, diff_subln, lam_init))
    if ctx_out:
        outs_c.append(_diff_finish(_diff_attend(q1c, q2c, k1c, k2c, vdc, lam), diff_subln, lam_init))

    y = _gated_merge(h, outs, w_branch, w_gate, b_gate) @ w_out
    yc = (_gated_merge(hc, outs_c, w_branch, w_gate, b_gate) @ w_out) if ctx_out else None
    return y, yc


def _swiglu(h, w1, w3, w2):
    return (jax.nn.silu(h @ w1) * (h @ w3)) @ w2


def _moe_swiglu(h, w_router, w1, w3, w2):
    t, d = h.shape
    top_v, top_i = lax.top_k((h @ w_router).astype(jnp.float32), TOP_K)
    gates = jax.nn.softmax(top_v, axis=-1)
    flat_e = top_i.reshape(-1)
    n_assign = t * TOP_K
    order = jnp.argsort(flat_e)
    sorted_e = flat_e[order]
    tok = (order // TOP_K).astype(jnp.int32)
    counts = jnp.bincount(flat_e, length=N_EXPERTS)
    padded = (counts + EXPERT_BLOCK - 1) // EXPERT_BLOCK * EXPERT_BLOCK
    pad_end = jnp.cumsum(padded)
    pad_start = pad_end - padded
    start = jnp.cumsum(counts) - counts
    dest = pad_start[sorted_e] + jnp.arange(n_assign) - start[sorted_e]
    n_blocks = -(-n_assign // EXPERT_BLOCK) + N_EXPERTS
    buf_tok = jnp.full((n_blocks * EXPERT_BLOCK,), t, jnp.int32).at[dest].set(tok)
    block_e = jnp.minimum(jnp.searchsorted(pad_end, jnp.arange(n_blocks) * EXPERT_BLOCK, side='right'), N_EXPERTS - 1)
    hp = jnp.concatenate([h, jnp.zeros((1, d), h.dtype)], axis=0)
    xb = hp[buf_tok].reshape(n_blocks, EXPERT_BLOCK, d)
    yb = lax.map(lambda a: _swiglu(a[0], w1[a[1]], w3[a[1]], w2[a[1]]), (xb, block_e))
    y_assign = yb.reshape(-1, d)[dest] * gates.reshape(-1)[order][:, None].astype(h.dtype)
    return jax.ops.segment_sum(y_assign, tok, num_segments=t)


def _channel_mixer(z, layer, ffn_w1, ffn_w3, ffn_w2, moe_router, moe_w1, moe_w3, moe_w2):
    j = layer // 2
    if layer % 2 == 0:
        return _swiglu(z, ffn_w1[j], ffn_w3[j], ffn_w2[j])
    flat = z.reshape(-1, z.shape[-1])
    return _moe_swiglu(flat, moe_router[j], moe_w1[j], moe_w3[j], moe_w2[j]).reshape(z.shape)


def setup_inputs(seed: int = 0) -> dict:
    key = jax.random.key(seed)
    ks = list(jax.random.split(key, 48))
    f32 = jnp.float32

    def nrm(shape, scale):
        return jax.random.normal(ks.pop(), shape, f32) * scale

    def gain(shape):
        return 1.0 + nrm(shape, 0.02)

    d = D_MODEL
    n_idx = jnp.arange(S5_STATE, dtype=f32)
    return {
        'x': nrm((BATCH, SEQ, d), 1.0),
        'c': nrm((BATCH, d), 1.0),
        'ctx': nrm((BATCH, CTX_LEN, d), 1.0),
        'c_ctx': nrm((d,), 1.0),
        'w_ada': nrm((DEPTH, d, 6 * d), d ** -0.5),
        'b_ada': nrm((DEPTH, 6 * d), 0.02),
        'w_in': nrm((DEPTH, d, IN_COLS), d ** -0.5),
        'na_rpb': nrm((DEPTH, NA_HEADS, 2 * NA_KH - 1, 2 * NA_KW - 1), 0.05),
        'mla_q_norm': gain((DEPTH, MLA_Q_LORA)),
        'mla_kv_norm': gain((DEPTH, MLA_KV_LORA)),
        'mla_w_uq': nrm((DEPTH, MLA_Q_LORA, MLA_HEADS * (MLA_NOPE + MLA_ROPE)), MLA_Q_LORA ** -0.5),
        'mla_w_ukv': nrm((DEPTH, MLA_KV_LORA, MLA_HEADS * (MLA_NOPE + MLA_V)), MLA_KV_LORA ** -0.5),
        's5_lam_re': -0.5 + nrm((DEPTH, 2, S5_GROUPS, S5_STATE), 0.01),
        's5_lam_im': math.pi * n_idx + nrm((DEPTH, 2, S5_GROUPS, S5_STATE), 0.01),
        's5_log_dt': jax.random.uniform(ks.pop(), (DEPTH, 2, S5_GROUPS), f32, math.log(1e-3), math.log(1e-1)),
        's5_b_re': nrm((DEPTH, 2, S5_GROUPS, S5_STATE, S5_GROUP_CH), (2 * S5_GROUP_CH) ** -0.5),
        's5_b_im': nrm((DEPTH, 2, S5_GROUPS, S5_STATE, S5_GROUP_CH), (2 * S5_GROUP_CH) ** -0.5),
        's5_c_re': nrm((DEPTH, 2, S5_GROUPS, S5_GROUP_CH, S5_STATE), S5_STATE ** -0.5),
        's5_c_im': nrm((DEPTH, 2, S5_GROUPS, S5_GROUP_CH, S5_STATE), S5_STATE ** -0.5),
        's5_d': nrm((DEPTH, BRANCH_W), 0.5),
        's5_w_glu': nrm((DEPTH, BRANCH_W, 2 * BRANCH_W), BRANCH_W ** -0.5),
        'diff_lam_q1': nrm((DEPTH, DIFF_HD), 0.1),
        'diff_lam_k1': nrm((DEPTH, DIFF_HD), 0.1),
        'diff_lam_q2': nrm((DEPTH, DIFF_HD), 0.1),
        'diff_lam_k2': nrm((DEPTH, DIFF_HD), 0.1),
        'diff_subln': gain((DEPTH, 2 * DIFF_HD)),
        'w_branch': nrm((DEPTH, N_BRANCH, BRANCH_W, d), BETA * BRANCH_W ** -0.5),
        'w_gate': nrm((DEPTH, d, N_BRANCH * d), d ** -0.5),
        'b_gate': nrm((DEPTH, N_BRANCH * d), 0.02),
        'w_out': nrm((DEPTH, d, d), BETA * d ** -0.5),
        'ln1_g': gain((DEPTH, d)),
        'ln1_b': nrm((DEPTH, d), 0.02),
        'ln2_g': gain((DEPTH, d)),
        'ln2_b': nrm((DEPTH, d), 0.02),
        'ffn_w1': nrm((N_DENSE, d, D_FF), d ** -0.5),
        'ffn_w3': nrm((N_DENSE, d, D_FF), d ** -0.5),
        'ffn_w2': nrm((N_DENSE, D_FF, d), BETA * D_FF ** -0.5),
        'moe_router': nrm((N_MOE, d, N_EXPERTS), d ** -0.5),
        'moe_w1': nrm((N_MOE, N_EXPERTS, d, D_FF), d ** -0.5),
        'moe_w3': nrm((N_MOE, N_EXPERTS, d, D_FF), d ** -0.5),
        'moe_w2': nrm((N_MOE, N_EXPERTS, D_FF, d), BETA * D_FF ** -0.5),
    }


def reference(x, c, ctx, c_ctx, w_ada, b_ada, w_in, na_rpb, mla_q_norm, mla_kv_norm, mla_w_uq, mla_w_ukv,
              s5_lam_re, s5_lam_im, s5_log_dt, s5_b_re, s5_b_im, s5_c_re, s5_c_im, s5_d, s5_w_glu,
              diff_lam_q1, diff_lam_k1, diff_lam_q2, diff_lam_k2, diff_subln,
              w_branch, w_gate, b_gate, w_out, ln1_g, ln1_b, ln2_g, ln2_b,
              ffn_w1, ffn_w3, ffn_w2, moe_router, moe_w1, moe_w3, moe_w2):
    b, s, d = x.shape
    t = jnp.arange(s)
    pos = (t // GRID_W, t % GRID_W)
    cond = jax.nn.silu(c)
    cond_ctx = jax.nn.silu(c_ctx)[None, :]
    xc = ctx
    for layer in range(DEPTH):
        ctx_out = layer < DEPTH - 1
        lam_init = 0.8 - 0.6 * math.exp(-0.3 * layer)
        mod = jnp.split((cond @ w_ada[layer] + b_ada[layer])[:, None, :], 6, axis=-1)
        mod_c = jnp.split((cond_ctx @ w_ada[layer] + b_ada[layer])[:, None, :], 6, axis=-1)
        y, yc = _token_mixing(x * (1.0 + mod[1]) + mod[0], xc * (1.0 + mod_c[1]) + mod_c[0], pos,
                              w_in[layer], na_rpb[layer], mla_q_norm[layer], mla_kv_norm[layer],
                              mla_w_uq[layer], mla_w_ukv[layer],
                              s5_lam_re[layer], s5_lam_im[layer], s5_log_dt[layer], s5_b_re[layer], s5_b_im[layer],
                              s5_c_re[layer], s5_c_im[layer], s5_d[layer], s5_w_glu[layer],
                              diff_lam_q1[layer], diff_lam_k1[layer], diff_lam_q2[layer], diff_lam_k2[layer],
                              diff_subln[layer], w_branch[layer], w_gate[layer], b_gate[layer], w_out[layer],
                              lam_init, ctx_out)
        x = _layer_norm(ALPHA * x + mod[2] * y, ln1_g[layer], ln1_b[layer])
        f = _channel_mixer(x * (1.0 + mod[4]) + mod[3], layer, ffn_w1, ffn_w3, ffn_w2, moe_router, moe_w1, moe_w3, moe_w2)
        x = _layer_norm(ALPHA * x + mod[5] * f, ln2_g[layer], ln2_b[layer])
        if ctx_out:
            xc = _layer_norm(ALPHA * xc + mod_c[2] * yc, ln1_g[layer], ln1_b[layer])
            fc = _channel_mixer(xc * (1.0 + mod_c[4]) + mod_c[3], layer, ffn_w1, ffn_w3, ffn_w2, moe_router, moe_w1, moe_w3, moe_w2)
            xc = _layer_norm(ALPHA * xc + mod_c[5] * fc, ln2_g[layer], ln2_b[layer])
    return x
```

```python
import functools
import math

import jax
import jax.numpy as jnp
import numpy as np
from jax import lax
from jax.experimental import pallas as pl
from jax.experimental.pallas import tpu as pltpu

F32 = jnp.float32
BF16 = jnp.bfloat16

D_MODEL = 1024
DEPTH = 4
GRID_W = 64
BRANCH_W = 256
NA_HEADS = 4
NA_HD = 64
NA_KH = 8
NA_KW = 16
MLA_HEADS = 4
MLA_NOPE = 64
MLA_ROPE = 32
MLA_V = 64
S5_GROUP_CH = 16
S5_GROUPS = 16
S5_STATE = 64
DIFF_HEADS = 4
DIFF_HD = 32
D_FF = 3584
N_EXPERTS = 8
ROPE_BASE = 10000.0
LN_EPS = 1e-5
RMS_EPS = 1e-6
ALPHA = (2 * DEPTH) ** 0.25
NA_SCALE = NA_HD ** -0.5
MLA_SCALE = (MLA_NOPE + MLA_ROPE) ** -0.5
DIFF_SCALE = DIFF_HD ** -0.5

LANES = 128
ZW = 2304
Z_NAQ, Z_NAK, Z_NAV, Z_MQ, Z_MKV, Z_MKR, Z_S5, Z_DQ, Z_DK, Z_DV = (
    0, 256, 512, 768, 1024, 1152, 1280, 1536, 1792, 2048)
NEG = -1e30
TM = 512
TQ = 512
TK = 512
NA_QROWS = 8
NA_KROWS = NA_QROWS + NA_KH
S5_TC = 128
FF_CHUNK = 512
MOE_BLK = 512
GATHER_ROWS = 512


def _cparams(sem, vmem_mb=48):
    return pltpu.CompilerParams(dimension_semantics=sem, vmem_limit_bytes=vmem_mb << 20)


def _dot(a, b):
    return jnp.dot(a, b, preferred_element_type=F32)


def _dot_nt(a, b):
    return lax.dot_general(a, b, (((1,), (1,)), ((), ())), preferred_element_type=F32)


def _layer_norm(r, g, b):
    rc = r - jnp.mean(r, axis=-1, keepdims=True)
    var = jnp.mean(rc * rc, axis=-1, keepdims=True)
    return rc * lax.rsqrt(var + LN_EPS) * g + b


def _ada_kernel(c_ref, w_ref, b_ref, o_ref):
    c = c_ref[...]
    cond = c * jax.nn.sigmoid(c)
    o_ref[0] = _dot(cond.astype(BF16), w_ref[0].astype(BF16)) + b_ref[0]


def _ada_all(cvec, w_ada, b_ada):
    nrow = cvec.shape[0]
    d = D_MODEL
    return pl.pallas_call(
        _ada_kernel,
        grid=(DEPTH, 6),
        in_specs=[pl.BlockSpec((nrow, d), lambda l, n: (0, 0)),
                  pl.BlockSpec((1, d, d), lambda l, n: (l, 0, n)),
                  pl.BlockSpec((1, 1, d), lambda l, n: (l, 0, n))],
        out_specs=pl.BlockSpec((1, nrow, d), lambda l, n: (l, 0, n)),
        out_shape=jax.ShapeDtypeStruct((DEPTH, nrow, 6 * d), F32),
        compiler_params=_cparams(("parallel", "parallel")),
        name="ada_mod",
    )(cvec, w_ada, b_ada.reshape(DEPTH, 1, 6 * d))


def _inproj_kernel(x_ref, mod_ref, w_ref, o_ref):
    m = mod_ref[0]
    h = x_ref[...] * (1.0 + m[1:2]) + m[0:1]
    o_ref[...] = _dot(h.astype(BF16), w_ref[...])


def _inproj(xa, modl, w_in_pad, seg):
    n, d = xa.shape
    return pl.pallas_call(
        _inproj_kernel,
        grid=(n // TM,),
        in_specs=[pl.BlockSpec((TM, d), lambda i: (i, 0)),
                  pl.BlockSpec((1, 6, d), lambda i: (seg(i), 0, 0)),
                  pl.BlockSpec((d, ZW), lambda i: (0, 0))],
        out_specs=pl.BlockSpec((TM, ZW), lambda i: (i, 0)),
        out_shape=jax.ShapeDtypeStruct((n, ZW), F32),
        compiler_params=_cparams(("parallel",)),
        name="in_proj",
    )(xa, modl, w_in_pad)


def _swap8(z):
    w = z.shape[-1]
    lane = lax.broadcasted_iota(jnp.int32, z.shape, 1)
    up = pltpu.roll(z, w - 8, 1)
    dn = pltpu.roll(z, 8, 1)
    return jnp.where((lane & 8) == 0, up, dn)


def _rms(z, g):
    return z * lax.rsqrt(jnp.mean(z * z, axis=-1, keepdims=True) + RMS_EPS) * g


def _prep_kernel(naq_ref, mq_ref, mkv_ref, mkr_ref, dq_ref, dk_ref, dv_ref,
                 cq_ref, sq_ref, ck_ref, sk_ref, cd_ref, sd_ref,
                 qn_ref, kvn_ref, wuq_ref, wukv_ref, place_ref,
                 qm_ref, km_ref, vm_ref, qd_ref, kd_ref, vd_ref, qna_ref):
    aq = _rms(mq_ref[...], qn_ref[...])
    q = _dot(aq.astype(BF16), wuq_ref[...])
    q = q * cq_ref[...] + _swap8(q) * sq_ref[...]
    qm_ref[...] = q.astype(BF16)
    akv = _rms(mkv_ref[...], kvn_ref[...])
    kv = _dot(akv.astype(BF16), wukv_ref[...])
    kr = mkr_ref[...]
    kr = kr * ck_ref[...] + _swap8(kr) * sk_ref[...]
    km_ref[...] = (kv[:, :4 * LANES] + _dot(kr.astype(BF16), place_ref[...])).astype(BF16)
    vm_ref[...] = kv[:, 4 * LANES:].astype(BF16)
    dq = dq_ref[...]
    dq = dq * cd_ref[...] + _swap8(dq) * sd_ref[...]
    dk = dk_ref[...]
    dk = dk * cd_ref[...] + _swap8(dk) * sd_ref[...]
    kd_ref[...] = dk.astype(BF16)
    vd_ref[...] = dv_ref[...].astype(BF16)
    lane = lax.broadcasted_iota(jnp.int32, (dq.shape[0], LANES), 1)
    for g in range(2 * DIFF_HEADS):
        blk = dq[:, (g // 4) * LANES:(g // 4 + 1) * LANES]
        qd_ref[:, g * LANES:(g + 1) * LANES] = jnp.where((lane // DIFF_HD) == (g % 4), blk, 0.0).astype(BF16)
    naq = naq_ref[...]
    for h in range(NA_HEADS):
        blk = naq[:, (h // 2) * LANES:(h // 2 + 1) * LANES]
        qna_ref[:, h * LANES:(h + 1) * LANES] = jnp.where((lane // NA_HD) == (h % 2), blk, 0.0).astype(BF16)


def _prep(z, tabs, qn, kvn, wuq, wukv, place, tab_idx):
    n = z.shape[0]
    cq, sq, ck, sk, cd, sd = tabs

    def zspec(width, col):
        return pl.BlockSpec((TM, width), lambda i: (i, col))

    def tspec(width):
        return pl.BlockSpec((TM, width), lambda i: (tab_idx(i), 0))

    def wspec(a):
        return pl.BlockSpec(a.shape, lambda i: (0,) * a.ndim)

    def ospec(width):
        return pl.BlockSpec((TM, width), lambda i: (i, 0))

    outs = [(512, "qm"), (512, "km"), (256, "vm"), (1024, "qd"), (256, "kd"), (256, "vd"), (512, "qna")]
    return pl.pallas_call(
        _prep_kernel,
        grid=(n // TM,),
        in_specs=[zspec(256, Z_NAQ // 256), zspec(256, Z_MQ // 256), zspec(128, Z_MKV // 128),
                  zspec(128, Z_MKR // 128), zspec(256, Z_DQ // 256), zspec(256, Z_DK // 256),
                  zspec(256, Z_DV // 256),
                  tspec(512), tspec(512), tspec(128), tspec(128), tspec(256), tspec(256),
                  wspec(qn), wspec(kvn), wspec(wuq), wspec(wukv), wspec(place)],
        out_specs=[ospec(w) for w, _ in outs],
        out_shape=[jax.ShapeDtypeStruct((n, w), BF16) for w, _ in outs],
        compiler_params=_cparams(("parallel",)),
        name="attn_prep",
    )(z, z, z, z, z, z, z, cq, sq, ck, sk, cd, sd, qn, kvn, wuq, wukv, place)


def _flash_kernel(*refs, heads, scale, n_acc, with_lat, diff, lam_init):
    refs = list(refs)
    q_ref = refs.pop(0)
    if with_lat:
        kl_ref, vl_ref = refs.pop(0), refs.pop(0)
    kc_ref, vc_ref = refs.pop(0), refs.pop(0)
    if diff:
        lam_ref, sub_ref = refs.pop(0), refs.pop(0)
    o_ref, m_sc, l_sc, acc_sc = refs
    tq = q_ref.shape[0]
    j = pl.program_id(2)
    lane = lax.broadcasted_iota(jnp.int32, (tq, LANES), 1)
    lo_half = lane < 64

    def tile(k_ref, v_ref):
        for g, (kb, vb, vh, ai) in enumerate(heads):
            q = q_ref[:, g * LANES:(g + 1) * LANES]
            k = k_ref[:, kb * LANES:(kb + 1) * LANES].astype(BF16)
            s = _dot_nt(q, k) * scale
            m_prev = m_sc[g]
            m_new = jnp.maximum(m_prev, jnp.max(s, axis=-1, keepdims=True))
            alpha = jnp.exp(m_prev - m_new)
            p = jnp.exp(s - m_new[:, :1])
            l_sc[g] = alpha * l_sc[g] + jnp.sum(p, axis=-1, keepdims=True)
            m_sc[g] = m_new
            v = v_ref[:, vb * LANES:(vb + 1) * LANES].astype(BF16)
            pv = _dot(p.astype(BF16), v)
            old = acc_sc[ai, :, vb * LANES:(vb + 1) * LANES]
            mine = lo_half if vh == 0 else jnp.logical_not(lo_half)
            acc_sc[ai, :, vb * LANES:(vb + 1) * LANES] = jnp.where(mine, alpha * old + pv, old)

    @pl.when(j == 0)
    def _():
        m_sc[...] = jnp.full(m_sc.shape, NEG, F32)
        l_sc[...] = jnp.zeros(l_sc.shape, F32)
        acc_sc[...] = jnp.zeros(acc_sc.shape, F32)
        tile(kc_ref, vc_ref)

    if with_lat:
        tile(kl_ref, vl_ref)

    @pl.when(j == pl.num_programs(2) - 1)
    def _():
        def inv_l(g):
            return 1.0 / l_sc[g]

        for c in range(2):
            if not diff:
                o = jnp.where(lo_half, acc_sc[0, :, c * LANES:(c + 1) * LANES] * inv_l(2 * c),
                              acc_sc[0, :, c * LANES:(c + 1) * LANES] * inv_l(2 * c + 1))
            else:
                o1 = jnp.where(lo_half, acc_sc[0, :, c * LANES:(c + 1) * LANES] * inv_l(4 * c),
                               acc_sc[0, :, c * LANES:(c + 1) * LANES] * inv_l(4 * c + 2))
                o2 = jnp.where(lo_half, acc_sc[1, :, c * LANES:(c + 1) * LANES] * inv_l(4 * c + 1),
                               acc_sc[1, :, c * LANES:(c + 1) * LANES] * inv_l(4 * c + 3))
                o = o1 - lam_ref[...] * o2
                sq = o * o
                ms_lo = jnp.sum(jnp.where(lo_half, sq, 0.0), axis=-1, keepdims=True) * (1.0 / 64)
                ms_hi = jnp.sum(jnp.where(lo_half, 0.0, sq), axis=-1, keepdims=True) * (1.0 / 64)
                rs = jnp.where(lo_half, lax.rsqrt(ms_lo + RMS_EPS), lax.rsqrt(ms_hi + RMS_EPS))
                o = o * rs * sub_ref[:, c * LANES:(c + 1) * LANES] * (1.0 - lam_init)
            o_ref[:, c * LANES:(c + 1) * LANES] = o


def _flash(q, k, v, *, nb, s_len, l_len, heads, scale, kcol=0, vcol=0, kw=None, vw=256,
           with_lat=True, diff=False, lam=None, subln=None, lam_init=0.0, name="flash"):
    nh = len(heads)
    n_acc = 2 if diff else 1
    ctx0 = nb * s_len // l_len
    if with_lat:
        tq, tk = min(TQ, s_len), min(TK, s_len)
        grid = (nb, s_len // tq, s_len // tk)
        rows = nb * s_len
        qmap = lambda b, i, j: (b * (s_len // tq) + i, 0)
    else:
        tq = l_len
        grid = (nb, 1, 1)
        rows = nb * l_len
        qmap = lambda b, i, j: (ctx0 + b, 0)
    in_specs = [pl.BlockSpec((tq, nh * LANES), qmap)]
    args = [q]
    if with_lat:
        in_specs += [pl.BlockSpec((tk, kw), lambda b, i, j: (b * (s_len // tk) + j, kcol)),
                     pl.BlockSpec((tk, vw), lambda b, i, j: (b * (s_len // tk) + j, vcol))]
        args += [k, v]
    in_specs += [pl.BlockSpec((l_len, kw), lambda b, i, j: (ctx0 + b, kcol)),
                 pl.BlockSpec((l_len, vw), lambda b, i, j: (ctx0 + b, vcol))]
    args += [k, v]
    if diff:
        in_specs += [pl.BlockSpec((1, LANES), lambda b, i, j: (0, 0)),
                     pl.BlockSpec((1, 256), lambda b, i, j: (0, 0))]
        args += [lam, subln]
    if with_lat:
        omap = lambda b, i, j: (b * (s_len // tq) + i, 0)
    else:
        omap = lambda b, i, j: (b, 0)
    return pl.pallas_call(
        functools.partial(_flash_kernel, heads=heads, scale=scale, n_acc=n_acc, with_lat=with_lat,
                          diff=diff, lam_init=lam_init),
        grid=grid,
        in_specs=in_specs,
        out_specs=pl.BlockSpec((tq, 256), omap),
        out_shape=jax.ShapeDtypeStruct((rows, 256), F32),
        scratch_shapes=[pltpu.VMEM((nh, tq, LANES), F32), pltpu.VMEM((nh, tq, LANES), F32),
                        pltpu.VMEM((n_acc, tq, 256), F32)],
        compiler_params=_cparams(("parallel", "parallel", "arbitrary")),
        name=name,
    )(*args)


MLA_HEADSPEC = tuple((h, h // 2, h % 2, 0) for h in range(MLA_HEADS))
NA_HEADSPEC = tuple((h // 2, h // 2, h % 2, 0) for h in range(NA_HEADS))
DIFF_HEADSPEC = tuple((g // 4, g // 4, (g // 2) % 2, g % 2) for g in range(2 * DIFF_HEADS))


def _na_kernel(q_ref, k_ref, v_ref, kc_ref, vc_ref, bias_ref, o_ref, *, rows):
    rb = pl.program_id(1)
    k0 = jnp.clip(NA_QROWS * rb - NA_KH // 2, 0, rows - NA_KROWS)
    start = pl.multiple_of(k0 * GRID_W, GRID_W)
    nk = NA_KROWS * GRID_W
    kw = k_ref[pl.ds(start, nk), :].astype(BF16)
    vw = v_ref[pl.ds(start, nk), :].astype(BF16)
    kc = kc_ref[...].astype(BF16)
    vc = vc_ref[...].astype(BF16)
    tq = q_ref.shape[0]
    lane = lax.broadcasted_iota(jnp.int32, (tq, LANES), 1)
    lo_half = lane < 64
    for c in range(2):
        o_c = jnp.zeros((tq, LANES), F32)
        for half in range(2):
            h = 2 * c + half
            q = q_ref[:, h * LANES:(h + 1) * LANES]
            s_loc = _dot_nt(q, kw[:, c * LANES:(c + 1) * LANES]) * NA_SCALE + bias_ref[0, h]
            s_ctx = _dot_nt(q, kc[:, c * LANES:(c + 1) * LANES]) * NA_SCALE
            m = jnp.maximum(jnp.max(s_loc, axis=-1, keepdims=True), jnp.max(s_ctx, axis=-1, keepdims=True))
            p_loc = jnp.exp(s_loc - m)
            p_ctx = jnp.exp(s_ctx - m)
            l = jnp.sum(p_loc, axis=-1, keepdims=True) + jnp.sum(p_ctx, axis=-1, keepdims=True)
            pv = (_dot(p_loc.astype(BF16), vw[:, c * LANES:(c + 1) * LANES])
                  + _dot(p_ctx.astype(BF16), vc[:, c * LANES:(c + 1) * LANES]))
            mine = lo_half if half == 0 else jnp.logical_not(lo_half)
            o_c = jnp.where(mine, pv * (1.0 / l), o_c)
        o_ref[:, c * LANES:(c + 1) * LANES] = o_c


def _na_attention(qna, z, bias, *, nb, s_len, l_len):
    rows = s_len // GRID_W
    tq = NA_QROWS * GRID_W
    nrb = rows // NA_QROWS
    ctx0 = nb * s_len // l_len

    def variant(rb):
        return jnp.where(rb == 0, 0, jnp.where(rb == nrb - 1, 2, 1))

    return pl.pallas_call(
        functools.partial(_na_kernel, rows=rows),
        grid=(nb, nrb),
        in_specs=[pl.BlockSpec((tq, NA_HEADS * LANES), lambda b, r: (b * nrb + r, 0)),
                  pl.BlockSpec((s_len, 256), lambda b, r: (b, Z_NAK // 256)),
                  pl.BlockSpec((s_len, 256), lambda b, r: (b, Z_NAV // 256)),
                  pl.BlockSpec((l_len, 256), lambda b, r: (ctx0 + b, Z_NAK // 256)),
                  pl.BlockSpec((l_len, 256), lambda b, r: (ctx0 + b, Z_NAV // 256)),
                  pl.BlockSpec((1, NA_HEADS, tq, NA_KROWS * GRID_W), lambda b, r: (variant(r), 0, 0, 0))],
        out_specs=pl.BlockSpec((tq, 256), lambda b, r: (b * nrb + r, 0)),
        out_shape=jax.ShapeDtypeStruct((nb * s_len, 256), F32),
        compiler_params=_cparams(("parallel", "arbitrary"), 56),
        name="na_attn",
    )(qna, z, z, z, z, bias)


def _na_bias_tables(rpb, rows):
    a = np.arange(NA_QROWS)
    qc = np.arange(GRID_W)
    kr_rel = np.arange(NA_KROWS)
    kc = np.arange(GRID_W)
    col0 = np.clip(qc - NA_KW // 2, 0, GRID_W - NA_KW)
    col_valid = (kc[None, :] >= col0[:, None]) & (kc[None, :] < col0[:, None] + NA_KW)
    col_off = np.clip(kc[None, :] - qc[:, None] + (NA_KW - 1), 0, 2 * NA_KW - 2)
    oh_col = (col_off[:, :, None] == np.arange(2 * NA_KW - 1)).astype(np.float32)
    big = 10 ** 6
    out = []
    for r_blk, k0, rows_eff in ((0, 0, big), (NA_QROWS, NA_KH // 2, big), (rows - NA_QROWS, rows - NA_KROWS, rows)):
        qr = r_blk + a
        r0 = np.clip(qr - NA_KH // 2, 0, rows_eff - NA_KH)
        kr = k0 + kr_rel
        row_valid = (kr[None, :] >= r0[:, None]) & (kr[None, :] < r0[:, None] + NA_KH)
        row_off = np.clip(kr[None, :] - qr[:, None] + (NA_KH - 1), 0, 2 * NA_KH - 2)
        oh_row = (row_off[:, :, None] == np.arange(2 * NA_KH - 1)).astype(np.float32)
        b = jnp.einsum('akr,hrc,qlc->haqkl', oh_row, rpb, oh_col, precision=lax.Precision.HIGHEST)
        valid = row_valid[:, None, :, None] & col_valid[None, :, None, :]
        b = jnp.where(valid[None], b, NEG)
        out.append(b.reshape(NA_HEADS, NA_QROWS * GRID_W, NA_KROWS * GRID_W))
    return jnp.stack(out)


def _s5_kernel(uf_ref, ub_ref, bf_ref, bb_ref, a_ref, cf_ref, cb_ref, yf_ref, yb_ref,
               buf_sc, bub_sc, hf_sc, hb_sc, *, nb):
    j = pl.program_id(0)
    tc = uf_ref.shape[0] // nb
    half = S5_GROUPS * S5_STATE

    @pl.when(j == 0)
    def _():
        hf_sc[...] = jnp.zeros(hf_sc.shape, F32)
        hb_sc[...] = jnp.zeros(hb_sc.shape, F32)

    def scan(u_ref, b_ref, c_ref, y_ref, bu_sc, h_sc, d, reverse):
        bu_sc[...] = _dot(u_ref[...].astype(BF16), b_ref[...])
        ar = jnp.broadcast_to(a_ref[d, 0:1, :], (nb, half))
        ai = jnp.broadcast_to(a_ref[d, 1:2, :], (nb, half))

        def step(t, carry):
            hr, hi = carry
            tt = (tc - 1 - t) if reverse else t
            row = pl.multiple_of(tt * nb, nb)
            nr = ar * hr - ai * hi + bu_sc[pl.ds(row, nb), 0:half]
            ni = ar * hi + ai * hr + bu_sc[pl.ds(row, nb), half:2 * half]
            bu_sc[pl.ds(row, nb), 0:half] = nr
            bu_sc[pl.ds(row, nb), half:2 * half] = ni
            return nr, ni

        hr, hi = lax.fori_loop(0, tc, step, (h_sc[:, 0:half], h_sc[:, half:2 * half]), unroll=4)
        h_sc[:, 0:half] = hr
        h_sc[:, half:2 * half] = hi
        y_ref[...] = _dot(bu_sc[...].astype(BF16), c_ref[...])

    scan(uf_ref, bf_ref, cf_ref, yf_ref, buf_sc, hf_sc, 0, False)
    scan(ub_ref, bb_ref, cb_ref, yb_ref, bub_sc, hb_sc, 1, True)


def _s5_scan(u_tm, bmat, avec, cmat, *, nb, s_len, l_len):
    rows = u_tm.shape[0]
    tc = S5_TC
    cr = tc * nb
    nchunk = rows // cr
    nctx = l_len // tc
    half = S5_GROUPS * S5_STATE

    def bwd(j):
        return jnp.where(j < nctx, nctx - 1 - j, nchunk - 1 - (j - nctx))

    return pl.pallas_call(
        functools.partial(_s5_kernel, nb=nb),
        grid=(nchunk,),
        in_specs=[pl.BlockSpec((cr, 256), lambda j: (j, 0)),
                  pl.BlockSpec((cr, 256), lambda j: (bwd(j), 0)),
                  pl.BlockSpec((256, 2 * half), lambda j: (0, 0)),
                  pl.BlockSpec((256, 2 * half), lambda j: (0, 0)),
                  pl.BlockSpec((2, 2, half), lambda j: (0, 0, 0)),
                  pl.BlockSpec((2 * half, 256), lambda j: (0, 0)),
                  pl.BlockSpec((2 * half, 256), lambda j: (0, 0))],
        out_specs=[pl.BlockSpec((cr, 256), lambda j: (j, 0)),
                   pl.BlockSpec((cr, 256), lambda j: (bwd(j), 0))],
        out_shape=[jax.ShapeDtypeStruct((rows, 256), F32)] * 2,
        scratch_shapes=[pltpu.VMEM((cr, 2 * half), F32), pltpu.VMEM((cr, 2 * half), F32),
                        pltpu.VMEM((nb, 2 * half), F32), pltpu.VMEM((nb, 2 * half), F32)],
        compiler_params=_cparams(("arbitrary",)),
        name="s5_scan",
    )(u_tm, u_tm, bmat[0], bmat[1], avec, cmat[0], cmat[1])


def _s5_finish_kernel(yf_ref, yb_ref, u_ref, d_ref, w_ref, o_ref):
    y = yf_ref[...] + yb_ref[...] + d_ref[...] * u_ref[...]
    r = _dot(y.astype(BF16), w_ref[...])
    o_ref[...] = r[:, :BRANCH_W] * jax.nn.sigmoid(r[:, BRANCH_W:])


def _s5_finish(yf, yb, u_tm, d_skip, w_glu):
    rows = u_tm.shape[0]
    spec = pl.BlockSpec((TM, 256), lambda i: (i, 0))
    return pl.pallas_call(
        _s5_finish_kernel,
        grid=(rows // TM,),
        in_specs=[spec, spec, spec, pl.BlockSpec((1, 256), lambda i: (0, 0)),
                  pl.BlockSpec((256, 512), lambda i: (0, 0))],
        out_specs=spec,
        out_shape=jax.ShapeDtypeStruct((rows, 256), F32),
        compiler_params=_cparams(("parallel",)),
        name="s5_glu",
    )(yf, yb, u_tm, d_skip, w_glu)


def _s5_params(lam_re, lam_im, log_dt, b_re, b_im, c_re, c_im):
    lam = lax.complex(lam_re, lam_im)
    dt = jnp.exp(log_dt)[..., None]
    a_bar = jnp.exp(lam * dt)
    b_bar = ((a_bar - 1.0) / lam)[..., None] * lax.complex(b_re, b_im)
    eye = jnp.eye(S5_GROUPS, dtype=F32)
    g, p, ch = S5_GROUPS, S5_STATE, S5_GROUP_CH

    def bdiag_in(m):
        return jnp.einsum('gh,dgpc->dgchp', eye, m).reshape(2, g * ch, g * p)

    def bdiag_out(m):
        return jnp.einsum('gh,dgcp->dgphc', eye, m).reshape(2, g * p, g * ch)

    bmat = jnp.concatenate([bdiag_in(jnp.real(b_bar)), bdiag_in(jnp.imag(b_bar))], axis=2).astype(BF16)
    cmat = jnp.concatenate([bdiag_out(c_re), -bdiag_out(c_im)], axis=1).astype(BF16)
    avec = jnp.stack([jnp.real(a_bar).reshape(2, g * p), jnp.imag(a_bar).reshape(2, g * p)], axis=1)
    return bmat, avec, cmat


def _merge_kernel(x_ref, mod_ref, o0_ref, o1_ref, o2_ref, o3_ref, wg_ref, bg_ref, wb_ref, wo_ref,
                  g_ref, b_ref, x1_ref, h2_ref):
    d = D_MODEL
    m = mod_ref[0]
    x = x_ref[...]
    hb = (x * (1.0 + m[1:2]) + m[0:1]).astype(BF16)
    acc = None
    for i, o_ref in enumerate((o0_ref, o1_ref, o2_ref, o3_ref)):
        gate = jax.nn.sigmoid(_dot(hb, wg_ref[:, i * d:(i + 1) * d]) + bg_ref[:, i * d:(i + 1) * d])
        term = gate * _dot(o_ref[...].astype(BF16), wb_ref[i])
        acc = term if acc is None else acc + term
    y = _dot(acc.astype(BF16), wo_ref[...])
    x1 = _layer_norm(ALPHA * x + m[2:3] * y, g_ref[...], b_ref[...])
    x1_ref[...] = x1
    h2_ref[...] = x1 * (1.0 + m[4:5]) + m[3:4]


def _merge(xa, modl, outs, wg, bg, wb, wo, g, b, seg, n):
    d = D_MODEL
    tok = pl.BlockSpec((TM, d), lambda i: (i, 0))
    br = pl.BlockSpec((TM, 256), lambda i: (i, 0))
    const = lambda a: pl.BlockSpec(a.shape, lambda i: (0,) * a.ndim)
    return pl.pallas_call(
        _merge_kernel,
        grid=(n // TM,),
        in_specs=[tok, pl.BlockSpec((1, 6, d), lambda i: (seg(i), 0, 0)), br, br, br, br,
                  const(wg), const(bg), const(wb), const(wo), const(g), const(b)],
        out_specs=[tok, tok],
        out_shape=[jax.ShapeDtypeStruct((n, d), F32)] * 2,
        compiler_params=_cparams(("parallel",), 56),
        name="merge_ln1",
    )(xa, modl, *outs, wg, bg, wb, wo, g, b)


def _ffn_kernel(be_ref, x_ref, w1_ref, w3_ref, w2_ref, o_ref):
    xb = x_ref[...].astype(BF16)
    for c in range(D_FF // FF_CHUNK):
        cs = slice(c * FF_CHUNK, (c + 1) * FF_CHUNK)
        a = _dot(xb, w1_ref[0, :, cs])
        b = _dot(xb, w3_ref[0, :, cs])
        g = (a * jax.nn.sigmoid(a) * b).astype(BF16)
        y = _dot(g, w2_ref[0, cs, :])
        if c == 0:
            o_ref[...] = y
        else:
            o_ref[...] += y


def _ffn(xb, block_e, w1, w3, w2):
    n, d = xb.shape
    blk = MOE_BLK
    grid_spec = pltpu.PrefetchScalarGridSpec(
        num_scalar_prefetch=1,
        grid=(n // blk,),
        in_specs=[pl.BlockSpec((blk, d), lambda i, be: (i, 0)),
                  pl.BlockSpec((1, d, D_FF), lambda i, be: (be[i], 0, 0), pipeline_mode=pl.Buffered(1)),
                  pl.BlockSpec((1, d, D_FF), lambda i, be: (be[i], 0, 0), pipeline_mode=pl.Buffered(1)),
                  pl.BlockSpec((1, D_FF, d), lambda i, be: (be[i], 0, 0), pipeline_mode=pl.Buffered(1))],
        out_specs=pl.BlockSpec((blk, d), lambda i, be: (i, 0)),
    )
    return pl.pallas_call(
        _ffn_kernel,
        grid_spec=grid_spec,
        out_shape=jax.ShapeDtypeStruct((n, d), F32),
        compiler_params=_cparams(("arbitrary",), 56),
        name="swiglu",
    )(block_e, xb, w1, w3, w2)


def _router_kernel(h_ref, w_ref, o_ref):
    logits = jnp.dot(h_ref[...], w_ref[...], preferred_element_type=F32, precision=lax.Precision.HIGHEST)
    lane = lax.broadcasted_iota(jnp.int32, logits.shape, 1)
    lg = jnp.where(lane < N_EXPERTS, logits, NEG)
    v1 = jnp.max(lg, axis=-1, keepdims=True)
    i1 = jnp.min(jnp.where(lg == v1, lane, LANES), axis=-1, keepdims=True)
    lg2 = jnp.where(lane == i1, NEG, lg)
    v2 = jnp.max(lg2, axis=-1, keepdims=True)
    i2 = jnp.min(jnp.where(lg2 == v2, lane, LANES), axis=-1, keepdims=True)
    e = jnp.exp(v2 - v1)
    g1 = 1.0 / (1.0 + e)
    g2 = e / (1.0 + e)
    out = jnp.where(lane == 0, i1.astype(F32), jnp.where(lane == 1, i2.astype(F32),
                    jnp.where(lane == 2, g1, jnp.where(lane == 3, g2, 0.0))))
    o_ref[...] = out


def _router(h2, w_router_pad):
    n, d = h2.shape
    return pl.pallas_call(
        _router_kernel,
        grid=(n // TM,),
        in_specs=[pl.BlockSpec((TM, d), lambda i: (i, 0)), pl.BlockSpec((d, LANES), lambda i: (0, 0))],
        out_specs=pl.BlockSpec((TM, LANES), lambda i: (i, 0)),
        out_shape=jax.ShapeDtypeStruct((n, LANES), F32),
        compiler_params=_cparams(("parallel",)),
        name="router",
    )(h2, w_router_pad)


def _gather_kernel(idx_ref, tab_ref, o_ref, sem):
    base = pl.program_id(0) * GATHER_ROWS

    def row_copy(src_row, r):
        return pltpu.make_async_copy(tab_ref.at[pl.ds(src_row, 1)], o_ref.at[pl.ds(r, 1)], sem)

    def issue(r, c):
        row_copy(idx_ref[base + r], r).start()
        return c

    lax.fori_loop(0, GATHER_ROWS, issue, 0)

    def drain(r, c):
        row_copy(0, r).wait()
        return c

    lax.fori_loop(0, GATHER_ROWS, drain, 0)


def _gather_rows(table, idx):
    n = idx.shape[0]
    d = table.shape[1]
    grid_spec = pltpu.PrefetchScalarGridSpec(
        num_scalar_prefetch=1,
        grid=(n // GATHER_ROWS,),
        in_specs=[pl.BlockSpec(memory_space=pl.ANY)],
        out_specs=pl.BlockSpec((GATHER_ROWS, d), lambda i, idx: (i, 0)),
        scratch_shapes=[pltpu.SemaphoreType.DMA(())],
    )
    return pl.pallas_call(
        _gather_kernel,
        grid_spec=grid_spec,
        out_shape=jax.ShapeDtypeStruct((n, d), table.dtype),
        compiler_params=_cparams(("arbitrary",)),
        name="row_gather",
    )(idx, table)


def _combine_kernel(*refs, moe):
    if moe:
        x_ref, mod_ref, ya_ref, yb_ref, gt_ref, g_ref, b_ref, o_ref = refs
        gt = gt_ref[...]
        f = gt[:, 2:3] * ya_ref[...] + gt[:, 3:4] * yb_ref[...]
    else:
        x_ref, mod_ref, ya_ref, g_ref, b_ref, o_ref = refs
        f = ya_ref[...]
    m = mod_ref[0]
    o_ref[...] = _layer_norm(ALPHA * x_ref[...] + m[5:6] * f, g_ref[...], b_ref[...])


def _combine(x1, modl, ys, gates, g, b, seg):
    n, d = x1.shape
    tok = pl.BlockSpec((TM, d), lambda i: (i, 0))
    vec = pl.BlockSpec((1, d), lambda i: (0, 0))
    moe = gates is not None
    in_specs = [tok, pl.BlockSpec((1, 6, d), lambda i: (seg(i), 0, 0))] + [tok] * len(ys)
    args = [x1, modl, *ys]
    if moe:
        in_specs.append(pl.BlockSpec((TM, LANES), lambda i: (i, 0)))
        args.append(gates)
    return pl.pallas_call(
        functools.partial(_combine_kernel, moe=moe),
        grid=(n // TM,),
        in_specs=in_specs + [vec, vec],
        out_specs=tok,
        out_shape=jax.ShapeDtypeStruct((n, d), F32),
        compiler_params=_cparams(("parallel",)),
        name="combine_ln2",
    )(*args, g, b)


def _rope_tables(s_len, width, rope_lane, extra_rows):
    t = jnp.arange(s_len)
    lane = np.arange(width)
    inv = ROPE_BASE ** (-(jnp.asarray(lane % 8, F32)) / 8.0)
    pos = jnp.where((lane & 16) == 0, (t // GRID_W)[:, None], (t % GRID_W)[:, None]).astype(F32)
    ang = pos * inv[None, :]
    cos = jnp.where(rope_lane[None, :], jnp.cos(ang), 1.0)
    sin = jnp.where(rope_lane[None, :], jnp.where((lane & 8) == 0, -jnp.sin(ang), jnp.sin(ang)), 0.0)
    cos = jnp.concatenate([cos, jnp.ones((extra_rows, width), F32)])
    sin = jnp.concatenate([sin, jnp.zeros((extra_rows, width), F32)])
    return cos, sin


def _pad_heads(w, n_heads, width):
    k = w.shape[0]
    w = w.reshape(k, n_heads, -1)
    return jnp.pad(w, ((0, 0), (0, 0), (0, width - w.shape[-1]))).reshape(k, n_heads * width)


def _route(top, n_tok):
    e_flat = top[:, 0:2].astype(jnp.int32).reshape(-1)
    n_assign = 2 * n_tok
    onehot = (e_flat[:, None] == jnp.arange(N_EXPERTS)[None, :]).astype(jnp.int32)
    csum = jnp.cumsum(onehot, axis=0)
    rank = jnp.take_along_axis(csum, e_flat[:, None], axis=1)[:, 0] - 1
    counts = csum[-1]
    padded = (counts + MOE_BLK - 1) // MOE_BLK * MOE_BLK
    pad_end = jnp.cumsum(padded)
    pad_start = pad_end - padded
    dest = pad_start[e_flat] + rank
    n_rows = -(-n_assign // MOE_BLK) * MOE_BLK + N_EXPERTS * MOE_BLK
    buf_tok = jnp.zeros((n_rows,), jnp.int32).at[dest].set(jnp.arange(n_assign, dtype=jnp.int32) // 2)
    block_e = jnp.minimum(jnp.searchsorted(pad_end, jnp.arange(n_rows // MOE_BLK) * MOE_BLK, side='right'),
                          N_EXPERTS - 1).astype(jnp.int32)
    return buf_tok, block_e, dest.astype(jnp.int32)


def kernel(x, c, ctx, c_ctx, w_ada, b_ada, w_in, na_rpb, mla_q_norm, mla_kv_norm, mla_w_uq, mla_w_ukv,
           s5_lam_re, s5_lam_im, s5_log_dt, s5_b_re, s5_b_im, s5_c_re, s5_c_im, s5_d, s5_w_glu,
           diff_lam_q1, diff_lam_k1, diff_lam_q2, diff_lam_k2, diff_subln,
           w_branch, w_gate, b_gate, w_out, ln1_g, ln1_b, ln2_g, ln2_b,
           ffn_w1, ffn_w3, ffn_w2, moe_router, moe_w1, moe_w3, moe_w2):
    nb, s_len, d = x.shape
    l_len = ctx.shape[1]
    n_lat, n_ctx = nb * s_len, nb * l_len
    n_tot = n_lat + n_ctx
    rows = s_len // GRID_W
    assert d == D_MODEL and s_len % TM == 0 and n_ctx % TM == 0 and rows % NA_QROWS == 0 and rows >= NA_KROWS
    assert l_len % S5_TC == 0 and s_len % S5_TC == 0

    def seg(i):
        return jnp.minimum((i * TM) // s_len, nb)

    n_lat_tiles = n_lat // TM

    def tab_idx(i):
        return jnp.where(i < n_lat_tiles, i % (s_len // TM), s_len // TM)

    xa = jnp.concatenate([x.reshape(n_lat, d), ctx.reshape(n_ctx, d)], axis=0)
    nrow_mod = -(-(nb + 1) // 16) * 16
    cvec = jnp.zeros((nrow_mod, d), F32).at[:nb].set(c).at[nb].set(c_ctx)
    mod_all = _ada_all(cvec, w_ada, b_ada).reshape(DEPTH, nrow_mod, 6, d)

    lane512 = np.arange(512)
    q_rope = (lane512 % LANES >= MLA_NOPE) & (lane512 % LANES < MLA_NOPE + MLA_ROPE)
    cq, sq = _rope_tables(s_len, 512, q_rope, TM)
    ck, sk = _rope_tables(s_len, LANES, np.arange(LANES) < MLA_ROPE, TM)
    cd, sd = _rope_tables(s_len, 256, np.ones((256,), bool), TM)
    tabs = (cq, sq, ck, sk, cd, sd)
    place = np.zeros((LANES, MLA_HEADS * LANES), np.float32)
    for h in range(MLA_HEADS):
        place[np.arange(MLA_ROPE), h * LANES + MLA_NOPE + np.arange(MLA_ROPE)] = 1.0
    place = jnp.asarray(place, BF16)

    for layer in range(DEPTH):
        ctx_out = layer < DEPTH - 1
        lam_init = 0.8 - 0.6 * math.exp(-0.3 * layer)
        modl = mod_all[layer]

        wi = w_in[layer]
        w_in_pad = jnp.concatenate([wi[:, :1184], jnp.zeros((d, 96), F32), wi[:, 1184:]], axis=1).astype(BF16)
        wuq = _pad_heads(mla_w_uq[layer], MLA_HEADS, LANES).astype(BF16)
        wukv4 = mla_w_ukv[layer].reshape(-1, MLA_HEADS, MLA_NOPE + MLA_V)
        wuk = jnp.pad(wukv4[:, :, :MLA_NOPE], ((0, 0), (0, 0), (0, LANES - MLA_NOPE))).reshape(-1, MLA_HEADS * LANES)
        wuv = wukv4[:, :, MLA_NOPE:].reshape(-1, MLA_HEADS * MLA_V)
        wukv = jnp.concatenate([wuk, wuv], axis=1).astype(BF16)
        bias = _na_bias_tables(na_rpb[layer], rows)
        bmat, avec, cmat = _s5_params(s5_lam_re[layer], s5_lam_im[layer], s5_log_dt[layer], s5_b_re[layer],
                                      s5_b_im[layer], s5_c_re[layer], s5_c_im[layer])
        lam = (jnp.exp(jnp.sum(diff_lam_q1[layer] * diff_lam_k1[layer]))
               - jnp.exp(jnp.sum(diff_lam_q2[layer] * diff_lam_k2[layer])) + lam_init)
        lam_vec = jnp.full((1, LANES), lam, F32)
        subln = jnp.tile(diff_subln[layer], DIFF_HEADS).reshape(1, 256)

        z = _inproj(xa, modl, w_in_pad, seg)
        qm, km, vm, qd, kd, vd, qna = _prep(z, tabs, mla_q_norm[layer].reshape(1, -1),
                                            mla_kv_norm[layer].reshape(1, -1), wuq, wukv, place, tab_idx)
        common = dict(nb=nb, s_len=s_len, l_len=l_len)
        o_na = _na_attention(qna, z, bias, **common)
        o_mla = _flash(qm, km, vm, heads=MLA_HEADSPEC, scale=MLA_SCALE, kw=512, name="mla_attn", **common)
        o_diff = _flash(qd, kd, vd, heads=DIFF_HEADSPEC, scale=DIFF_SCALE, kw=256, diff=True, lam=lam_vec,
                        subln=subln, lam_init=lam_init, name="diff_attn", **common)
        u = z[:, Z_S5:Z_S5 + 256]
        u_tm = jnp.concatenate([u[n_lat:].reshape(nb, l_len, 256).transpose(1, 0, 2).reshape(n_ctx, 256),
                                u[:n_lat].reshape(nb, s_len, 256).transpose(1, 0, 2).reshape(n_lat, 256)], axis=0)
        yf, yb = _s5_scan(u_tm, bmat, avec, cmat, **common)
        o_s5_tm = _s5_finish(yf, yb, u_tm, s5_d[layer].reshape(1, 256), s5_w_glu[layer].astype(BF16))
        o_s5_lat = o_s5_tm[n_ctx:].reshape(s_len, nb, 256).transpose(1, 0, 2).reshape(n_lat, 256)
        if ctx_out:
            o_na_c = _flash(qna, z, z, heads=NA_HEADSPEC, scale=NA_SCALE, kcol=Z_NAK // 256, vcol=Z_NAV // 256,
                            kw=256, with_lat=False, name="na_ctx_attn", **common)
            o_mla_c = _flash(qm, km, vm, heads=MLA_HEADSPEC, scale=MLA_SCALE, kw=512, with_lat=False,
                             name="mla_ctx_attn", **common)
            o_diff_c = _flash(qd, kd, vd, heads=DIFF_HEADSPEC, scale=DIFF_SCALE, kw=256, diff=True, lam=lam_vec,
                              subln=subln, lam_init=lam_init, with_lat=False, name="diff_ctx_attn", **common)
            o_s5_c = o_s5_tm[:n_ctx].reshape(l_len, nb, 256).transpose(1, 0, 2).reshape(n_ctx, 256)
            outs = [jnp.concatenate([a, b_], axis=0) for a, b_ in
                    ((o_na, o_na_c), (o_mla, o_mla_c), (o_s5_lat, o_s5_c), (o_diff, o_diff_c))]
            n_act = n_tot
        else:
            outs = [o_na, o_mla, o_s5_lat, o_diff]
            n_act = n_lat
            xa = xa[:n_lat]
        x1, h2 = _merge(xa, modl, outs, w_gate[layer].astype(BF16), b_gate[layer].reshape(1, -1),
                        w_branch[layer].astype(BF16), w_out[layer].astype(BF16),
                        ln1_g[layer].reshape(1, d), ln1_b[layer].reshape(1, d), seg, n_act)

        jj = layer // 2
        g2, b2 = ln2_g[layer].reshape(1, d), ln2_b[layer].reshape(1, d)
        if layer % 2 == 0:
            be = jnp.zeros((n_act // MOE_BLK,), jnp.int32)
            f = _ffn(h2, be, ffn_w1[jj:jj + 1].astype(BF16), ffn_w3[jj:jj + 1].astype(BF16),
                     ffn_w2[jj:jj + 1].astype(BF16))
            xa = _combine(x1, modl, [f], None, g2, b2, seg)
        else:
            wr = jnp.pad(moe_router[jj], ((0, 0), (0, LANES - N_EXPERTS)))
            top = _router(h2, wr)
            buf_tok, block_e, dest = _route(top, n_act)
            xb = _gather_rows(h2, buf_tok)
            yb_rows = _ffn(xb, block_e, moe_w1[jj].astype(BF16), moe_w3[jj].astype(BF16), moe_w2[jj].astype(BF16))
            pad = (-n_act) % GATHER_ROWS
            ya = _gather_rows(yb_rows, jnp.pad(dest[0::2], (0, pad)))[:n_act]
            yb2 = _gather_rows(yb_rows, jnp.pad(dest[1::2], (0, pad)))[:n_act]
            xa = _combine(x1, modl, [ya, yb2], top, g2, b2, seg)
    return xa[:n_lat].reshape(nb, s_len, d)
```

```python
import functools
import math

import jax
import jax.numpy as jnp
import numpy as np
from jax import lax
from jax.experimental import pallas as pl
from jax.experimental.pallas import tpu as pltpu

F32 = jnp.float32
BF16 = jnp.bfloat16

D_MODEL = 1024
DEPTH = 4
GRID_W = 64
BRANCH_W = 256
NA_HEADS = 4
NA_HD = 64
NA_KH = 8
NA_KW = 16
MLA_HEADS = 4
MLA_NOPE = 64
MLA_ROPE = 32
MLA_V = 64
S5_GROUP_CH = 16
S5_GROUPS = 16
S5_STATE = 64
DIFF_HEADS = 4
DIFF_HD = 32
D_FF = 3584
N_EXPERTS = 8
ROPE_BASE = 10000.0
LN_EPS = 1e-5
RMS_EPS = 1e-6
ALPHA = (2 * DEPTH) ** 0.25
NA_SCALE = NA_HD ** -0.5
MLA_SCALE = (MLA_NOPE + MLA_ROPE) ** -0.5
DIFF_SCALE = DIFF_HD ** -0.5
LOG2E = 1.4426950408889634

LANES = 128
ZW = 2304
Z_NAQ, Z_NAK, Z_NAV, Z_MQ, Z_MKV, Z_MKR, Z_S5, Z_DQ, Z_DK, Z_DV = (
    0, 256, 512, 768, 1024, 1152, 1280, 1536, 1792, 2048)
NEG = -1e30
TM = 512
TQ = 512
TK = 512
NA_QROWS = 8
NA_KROWS = NA_QROWS + NA_KH
S5_TC = 128
FF_CHUNK = 512
MOE_BLK = 512


def _cparams(sem, vmem_mb=48):
    return pltpu.CompilerParams(dimension_semantics=sem, vmem_limit_bytes=vmem_mb << 20)


def _dot(a, b):
    return jnp.dot(a, b, preferred_element_type=F32)


def _dot_nt(a, b):
    return lax.dot_general(a, b, (((1,), (1,)), ((), ())), preferred_element_type=F32)


def _layer_norm(r, g, b):
    rc = r - jnp.mean(r, axis=-1, keepdims=True)
    var = jnp.mean(rc * rc, axis=-1, keepdims=True)
    return rc * lax.rsqrt(var + LN_EPS) * g + b


def _ada_kernel(c_ref, w_ref, b_ref, o_ref):
    c = c_ref[...]
    cond = c * jax.nn.sigmoid(c)
    o_ref[0] = _dot(cond.astype(BF16), w_ref[0].astype(BF16)) + b_ref[0]


def _ada_all(cvec, w_ada, b_ada):
    nrow = cvec.shape[0]
    d = D_MODEL
    return pl.pallas_call(
        _ada_kernel,
        grid=(DEPTH, 6),
        in_specs=[pl.BlockSpec((nrow, d), lambda l, n: (0, 0)),
                  pl.BlockSpec((1, d, d), lambda l, n: (l, 0, n)),
                  pl.BlockSpec((1, 1, d), lambda l, n: (l, 0, n))],
        out_specs=pl.BlockSpec((1, nrow, d), lambda l, n: (l, 0, n)),
        out_shape=jax.ShapeDtypeStruct((DEPTH, nrow, 6 * d), F32),
        compiler_params=_cparams(("parallel", "parallel")),
        name="ada_mod",
    )(cvec, w_ada, b_ada.reshape(DEPTH, 1, 6 * d))


def _inproj_kernel(x_ref, mod_ref, w_ref, o_ref):
    m = mod_ref[0]
    h = x_ref[...] * (1.0 + m[1:2]) + m[0:1]
    o_ref[...] = _dot(h.astype(BF16), w_ref[...])


def _inproj(xa, modl, w_in_pad, seg):
    n, d = xa.shape
    return pl.pallas_call(
        _inproj_kernel,
        grid=(n // TM,),
        in_specs=[pl.BlockSpec((TM, d), lambda i: (i, 0)),
                  pl.BlockSpec((1, 6, d), lambda i: (seg(i), 0, 0)),
                  pl.BlockSpec((d, ZW), lambda i: (0, 0))],
        out_specs=pl.BlockSpec((TM, ZW), lambda i: (i, 0)),
        out_shape=jax.ShapeDtypeStruct((n, ZW), F32),
        compiler_params=_cparams(("parallel",)),
        name="in_proj",
    )(xa, modl, w_in_pad)


def _swap8(z):
    w = z.shape[-1]
    lane = lax.broadcasted_iota(jnp.int32, z.shape, 1)
    up = pltpu.roll(z, w - 8, 1)
    dn = pltpu.roll(z, 8, 1)
    return jnp.where((lane & 8) == 0, up, dn)


def _rms(z, g):
    return z * lax.rsqrt(jnp.mean(z * z, axis=-1, keepdims=True) + RMS_EPS) * g


def _prep_kernel(naq_ref, mq_ref, mkv_ref, mkr_ref, dq_ref, dk_ref, dv_ref,
                 cq_ref, sq_ref, ck_ref, sk_ref, cd_ref, sd_ref,
                 qn_ref, kvn_ref, wuq_ref, wukv_ref, place_ref,
                 qm_ref, km_ref, vm_ref, qd_ref, kd_ref, vd_ref, qna_ref):
    aq = _rms(mq_ref[...], qn_ref[...])
    q = _dot(aq.astype(BF16), wuq_ref[...])
    q = q * cq_ref[...] + _swap8(q) * sq_ref[...]
    qm_ref[...] = (q * (MLA_SCALE * LOG2E)).astype(BF16)
    akv = _rms(mkv_ref[...], kvn_ref[...])
    kv = _dot(akv.astype(BF16), wukv_ref[...])
    kr = mkr_ref[...]
    kr = kr * ck_ref[...] + _swap8(kr) * sk_ref[...]
    km_ref[...] = (kv[:, :4 * LANES] + _dot(kr.astype(BF16), place_ref[...])).astype(BF16)
    vm_ref[...] = kv[:, 4 * LANES:].astype(BF16)
    dq = dq_ref[...]
    dq = (dq * cd_ref[...] + _swap8(dq) * sd_ref[...]) * (DIFF_SCALE * LOG2E)
    dk = dk_ref[...]
    dk = dk * cd_ref[...] + _swap8(dk) * sd_ref[...]
    kd_ref[...] = dk.astype(BF16)
    vd_ref[...] = dv_ref[...].astype(BF16)
    lane = lax.broadcasted_iota(jnp.int32, (dq.shape[0], LANES), 1)
    for g in range(2 * DIFF_HEADS):
        blk = dq[:, (g // 4) * LANES:(g // 4 + 1) * LANES]
        qd_ref[:, g * LANES:(g + 1) * LANES] = jnp.where((lane // DIFF_HD) == (g % 4), blk, 0.0).astype(BF16)
    naq = naq_ref[...] * (NA_SCALE * LOG2E)
    for h in range(NA_HEADS):
        blk = naq[:, (h // 2) * LANES:(h // 2 + 1) * LANES]
        qna_ref[:, h * LANES:(h + 1) * LANES] = jnp.where((lane // NA_HD) == (h % 2), blk, 0.0).astype(BF16)


def _prep(z, tabs, qn, kvn, wuq, wukv, place, tab_idx):
    n = z.shape[0]
    cq, sq, ck, sk, cd, sd = tabs

    def zspec(width, col):
        return pl.BlockSpec((TM, width), lambda i: (i, col))

    def tspec(width):
        return pl.BlockSpec((TM, width), lambda i: (tab_idx(i), 0))

    def wspec(a):
        return pl.BlockSpec(a.shape, lambda i: (0,) * a.ndim)

    def ospec(width):
        return pl.BlockSpec((TM, width), lambda i: (i, 0))

    outs = [(512, "qm"), (512, "km"), (256, "vm"), (1024, "qd"), (256, "kd"), (256, "vd"), (512, "qna")]
    return pl.pallas_call(
        _prep_kernel,
        grid=(n // TM,),
        in_specs=[zspec(256, Z_NAQ // 256), zspec(256, Z_MQ // 256), zspec(128, Z_MKV // 128),
                  zspec(128, Z_MKR // 128), zspec(256, Z_DQ // 256), zspec(256, Z_DK // 256),
                  zspec(256, Z_DV // 256),
                  tspec(512), tspec(512), tspec(128), tspec(128), tspec(256), tspec(256),
                  wspec(qn), wspec(kvn), wspec(wuq), wspec(wukv), wspec(place)],
        out_specs=[ospec(w) for w, _ in outs],
        out_shape=[jax.ShapeDtypeStruct((n, w), BF16) for w, _ in outs],
        compiler_params=_cparams(("parallel",)),
        name="attn_prep",
    )(z, z, z, z, z, z, z, cq, sq, ck, sk, cd, sd, qn, kvn, wuq, wukv, place)


def _flash_kernel(*refs, heads, n_acc, with_lat, diff, lam_init):
    refs = list(refs)
    q_ref = refs.pop(0)
    if with_lat:
        kl_ref, vl_ref = refs.pop(0), refs.pop(0)
    kc_ref, vc_ref = refs.pop(0), refs.pop(0)
    if diff:
        lam_ref, sub_ref = refs.pop(0), refs.pop(0)
    o_ref, m_sc, l_sc, acc_sc = refs
    tq = q_ref.shape[0]
    j = pl.program_id(2)
    lane = lax.broadcasted_iota(jnp.int32, (tq, LANES), 1)
    lo_half = lane < 64

    def tile(k_ref, v_ref):
        nkb = k_ref.shape[0] // LANES
        for g, (kb, vb, vh, ai) in enumerate(heads):
            q = q_ref[:, g * LANES:(g + 1) * LANES]
            k = k_ref[:, kb * LANES:(kb + 1) * LANES].astype(BF16)
            s = _dot_nt(q, k)
            blocks = [s[:, c * LANES:(c + 1) * LANES] for c in range(nkb)]
            mx = blocks[0]
            for blk in blocks[1:]:
                mx = jnp.maximum(mx, blk)
            m_prev = m_sc[g]
            m_new = jnp.maximum(m_prev, jnp.max(mx, axis=-1, keepdims=True))
            alpha = jnp.exp2(m_prev - m_new)
            ps = [jnp.exp2(blk - m_new) for blk in blocks]
            lsum = ps[0]
            for pb in ps[1:]:
                lsum = lsum + pb
            l_sc[g] = alpha * l_sc[g] + lsum
            m_sc[g] = m_new
            p = jnp.concatenate([pb.astype(BF16) for pb in ps], axis=1)
            v = v_ref[:, vb * LANES:(vb + 1) * LANES].astype(BF16)
            pv = _dot(p, v)
            old = acc_sc[ai, :, vb * LANES:(vb + 1) * LANES]
            mine = lo_half if vh == 0 else jnp.logical_not(lo_half)
            acc_sc[ai, :, vb * LANES:(vb + 1) * LANES] = jnp.where(mine, alpha * old + pv, old)

    @pl.when(j == 0)
    def _():
        m_sc[...] = jnp.full(m_sc.shape, NEG, F32)
        l_sc[...] = jnp.zeros(l_sc.shape, F32)
        acc_sc[...] = jnp.zeros(acc_sc.shape, F32)
        tile(kc_ref, vc_ref)

    if with_lat:
        tile(kl_ref, vl_ref)

    @pl.when(j == pl.num_programs(2) - 1)
    def _():
        def inv_l(g):
            return 1.0 / jnp.sum(l_sc[g], axis=-1, keepdims=True)

        for c in range(2):
            if not diff:
                o = jnp.where(lo_half, acc_sc[0, :, c * LANES:(c + 1) * LANES] * inv_l(2 * c),
                              acc_sc[0, :, c * LANES:(c + 1) * LANES] * inv_l(2 * c + 1))
            else:
                o1 = jnp.where(lo_half, acc_sc[0, :, c * LANES:(c + 1) * LANES] * inv_l(4 * c),
                               acc_sc[0, :, c * LANES:(c + 1) * LANES] * inv_l(4 * c + 2))
                o2 = jnp.where(lo_half, acc_sc[1, :, c * LANES:(c + 1) * LANES] * inv_l(4 * c + 1),
                               acc_sc[1, :, c * LANES:(c + 1) * LANES] * inv_l(4 * c + 3))
                o = o1 - lam_ref[...] * o2
                sq = o * o
                ms_lo = jnp.sum(jnp.where(lo_half, sq, 0.0), axis=-1, keepdims=True) * (1.0 / 64)
                ms_hi = jnp.sum(jnp.where(lo_half, 0.0, sq), axis=-1, keepdims=True) * (1.0 / 64)
                rs = jnp.where(lo_half, lax.rsqrt(ms_lo + RMS_EPS), lax.rsqrt(ms_hi + RMS_EPS))
                o = o * rs * sub_ref[:, c * LANES:(c + 1) * LANES] * (1.0 - lam_init)
            o_ref[:, c * LANES:(c + 1) * LANES] = o


def _flash(q, k, v, *, nb, s_len, l_len, heads, kcol=0, vcol=0, kw=None, vw=256,
           with_lat=True, diff=False, lam=None, subln=None, lam_init=0.0, name="flash"):
    nh = len(heads)
    n_acc = 2 if diff else 1
    ctx0 = nb * s_len // l_len
    if with_lat:
        tq, tk = min(TQ, s_len), min(TK, s_len)
        grid = (nb, s_len // tq, s_len // tk)
        rows = nb * s_len
        qmap = lambda b, i, j: (b * (s_len // tq) + i, 0)
    else:
        tq = l_len
        grid = (nb, 1, 1)
        rows = nb * l_len
        qmap = lambda b, i, j: (ctx0 + b, 0)
    in_specs = [pl.BlockSpec((tq, nh * LANES), qmap)]
    args = [q]
    if with_lat:
        in_specs += [pl.BlockSpec((tk, kw), lambda b, i, j: (b * (s_len // tk) + j, kcol)),
                     pl.BlockSpec((tk, vw), lambda b, i, j: (b * (s_len // tk) + j, vcol))]
        args += [k, v]
    in_specs += [pl.BlockSpec((l_len, kw), lambda b, i, j: (ctx0 + b, kcol)),
                 pl.BlockSpec((l_len, vw), lambda b, i, j: (ctx0 + b, vcol))]
    args += [k, v]
    if diff:
        in_specs += [pl.BlockSpec((1, LANES), lambda b, i, j: (0, 0)),
                     pl.BlockSpec((1, 256), lambda b, i, j: (0, 0))]
        args += [lam, subln]
    if with_lat:
        omap = lambda b, i, j: (b * (s_len // tq) + i, 0)
    else:
        omap = lambda b, i, j: (b, 0)
    return pl.pallas_call(
        functools.partial(_flash_kernel, heads=heads, n_acc=n_acc, with_lat=with_lat,
                          diff=diff, lam_init=lam_init),
        grid=grid,
        in_specs=in_specs,
        out_specs=pl.BlockSpec((tq, 256), omap),
        out_shape=jax.ShapeDtypeStruct((rows, 256), F32),
        scratch_shapes=[pltpu.VMEM((nh, tq, LANES), F32), pltpu.VMEM((nh, tq, LANES), F32),
                        pltpu.VMEM((n_acc, tq, 256), F32)],
        compiler_params=_cparams(("parallel", "parallel", "arbitrary")),
        name=name,
    )(*args)


MLA_HEADSPEC = tuple((h, h // 2, h % 2, 0) for h in range(MLA_HEADS))
NA_HEADSPEC = tuple((h // 2, h // 2, h % 2, 0) for h in range(NA_HEADS))
DIFF_HEADSPEC = tuple((g // 4, g // 4, (g // 2) % 2, g % 2) for g in range(2 * DIFF_HEADS))


def _na_kernel(q_ref, k_ref, v_ref, kc_ref, vc_ref, bias_ref, o_ref, *, rows):
    rb = pl.program_id(1)
    k0 = jnp.clip(NA_QROWS * rb - NA_KH // 2, 0, rows - NA_KROWS)
    start = pl.multiple_of(k0 * GRID_W, GRID_W)
    nk = NA_KROWS * GRID_W
    kw = k_ref[pl.ds(start, nk), :].astype(BF16)
    vw = v_ref[pl.ds(start, nk), :].astype(BF16)
    kc = kc_ref[...].astype(BF16)
    vc = vc_ref[...].astype(BF16)
    tq = q_ref.shape[0]
    lane = lax.broadcasted_iota(jnp.int32, (tq, LANES), 1)
    lo_half = lane < 64
    for c in range(2):
        o_c = jnp.zeros((tq, LANES), F32)
        for half in range(2):
            h = 2 * c + half
            q = q_ref[:, h * LANES:(h + 1) * LANES]
            s_loc = _dot_nt(q, kw[:, c * LANES:(c + 1) * LANES]) + bias_ref[0, h]
            s_ctx = _dot_nt(q, kc[:, c * LANES:(c + 1) * LANES])
            m = jnp.maximum(jnp.max(s_loc, axis=-1, keepdims=True), jnp.max(s_ctx, axis=-1, keepdims=True))
            p_loc = jnp.exp2(s_loc - m)
            p_ctx = jnp.exp2(s_ctx - m)
            l = jnp.sum(p_loc, axis=-1, keepdims=True) + jnp.sum(p_ctx, axis=-1, keepdims=True)
            pv = (_dot(p_loc.astype(BF16), vw[:, c * LANES:(c + 1) * LANES])
                  + _dot(p_ctx.astype(BF16), vc[:, c * LANES:(c + 1) * LANES]))
            mine = lo_half if half == 0 else jnp.logical_not(lo_half)
            o_c = jnp.where(mine, pv * (1.0 / l), o_c)
        o_ref[:, c * LANES:(c + 1) * LANES] = o_c


def _na_attention(qna, z, bias, *, nb, s_len, l_len):
    rows = s_len // GRID_W
    tq = NA_QROWS * GRID_W
    nrb = rows // NA_QROWS
    ctx0 = nb * s_len // l_len

    def variant(rb):
        return jnp.where(rb == 0, 0, jnp.where(rb == nrb - 1, 2, 1))

    return pl.pallas_call(
        functools.partial(_na_kernel, rows=rows),
        grid=(nb, nrb),
        in_specs=[pl.BlockSpec((tq, NA_HEADS * LANES), lambda b, r: (b * nrb + r, 0)),
                  pl.BlockSpec((s_len, 256), lambda b, r: (b, Z_NAK // 256)),
                  pl.BlockSpec((s_len, 256), lambda b, r: (b, Z_NAV // 256)),
                  pl.BlockSpec((l_len, 256), lambda b, r: (ctx0 + b, Z_NAK // 256)),
                  pl.BlockSpec((l_len, 256), lambda b, r: (ctx0 + b, Z_NAV // 256)),
                  pl.BlockSpec((1, NA_HEADS, tq, NA_KROWS * GRID_W), lambda b, r: (variant(r), 0, 0, 0))],
        out_specs=pl.BlockSpec((tq, 256), lambda b, r: (b * nrb + r, 0)),
        out_shape=jax.ShapeDtypeStruct((nb * s_len, 256), F32),
        compiler_params=_cparams(("parallel", "arbitrary"), 56),
        name="na_attn",
    )(qna, z, z, z, z, bias)


def _na_bias_tables(rpb, rows):
    a = np.arange(NA_QROWS)
    qc = np.arange(GRID_W)
    kr_rel = np.arange(NA_KROWS)
    kc = np.arange(GRID_W)
    col0 = np.clip(qc - NA_KW // 2, 0, GRID_W - NA_KW)
    col_valid = (kc[None, :] >= col0[:, None]) & (kc[None, :] < col0[:, None] + NA_KW)
    col_off = np.clip(kc[None, :] - qc[:, None] + (NA_KW - 1), 0, 2 * NA_KW - 2)
    oh_col = (col_off[:, :, None] == np.arange(2 * NA_KW - 1)).astype(np.float32)
    big = 10 ** 6
    out = []
    for r_blk, k0, rows_eff in ((0, 0, big), (NA_QROWS, NA_KH // 2, big), (rows - NA_QROWS, rows - NA_KROWS, rows)):
        qr = r_blk + a
        r0 = np.clip(qr - NA_KH // 2, 0, rows_eff - NA_KH)
        kr = k0 + kr_rel
        row_valid = (kr[None, :] >= r0[:, None]) & (kr[None, :] < r0[:, None] + NA_KH)
        row_off = np.clip(kr[None, :] - qr[:, None] + (NA_KH - 1), 0, 2 * NA_KH - 2)
        oh_row = (row_off[:, :, None] == np.arange(2 * NA_KH - 1)).astype(np.float32)
        b = jnp.einsum('akr,hrc,qlc->haqkl', oh_row, rpb, oh_col, precision=lax.Precision.HIGHEST)
        valid = row_valid[:, None, :, None] & col_valid[None, :, None, :]
        b = jnp.where(valid[None], b * LOG2E, NEG)
        out.append(b.reshape(NA_HEADS, NA_QROWS * GRID_W, NA_KROWS * GRID_W))
    return jnp.stack(out)


def _s5_kernel(uf_ref, ub_ref, bf_ref, bb_ref, a_ref, cf_ref, cb_ref, yf_ref, yb_ref,
               buf_sc, bub_sc, hf_sc, hb_sc, *, nb):
    j = pl.program_id(0)
    tc = uf_ref.shape[0] // nb
    half = S5_GROUPS * S5_STATE

    @pl.when(j == 0)
    def _():
        hf_sc[...] = jnp.zeros(hf_sc.shape, F32)
        hb_sc[...] = jnp.zeros(hb_sc.shape, F32)

    def scan(u_ref, b_ref, c_ref, y_ref, bu_sc, h_sc, d, reverse):
        bu_sc[...] = _dot(u_ref[...].astype(BF16), b_ref[...])
        ar = jnp.broadcast_to(a_ref[d, 0:1, :], (nb, half))
        ai = jnp.broadcast_to(a_ref[d, 1:2, :], (nb, half))

        def step(t, carry):
            hr, hi = carry
            tt = (tc - 1 - t) if reverse else t
            row = pl.multiple_of(tt * nb, nb)
            nr = ar * hr - ai * hi + bu_sc[pl.ds(row, nb), 0:half]
            ni = ar * hi + ai * hr + bu_sc[pl.ds(row, nb), half:2 * half]
            bu_sc[pl.ds(row, nb), 0:half] = nr
            bu_sc[pl.ds(row, nb), half:2 * half] = ni
            return nr, ni

        hr, hi = lax.fori_loop(0, tc, step, (h_sc[:, 0:half], h_sc[:, half:2 * half]), unroll=4)
        h_sc[:, 0:half] = hr
        h_sc[:, half:2 * half] = hi
        y_ref[...] = _dot(bu_sc[...].astype(BF16), c_ref[...])

    scan(uf_ref, bf_ref, cf_ref, yf_ref, buf_sc, hf_sc, 0, False)
    scan(ub_ref, bb_ref, cb_ref, yb_ref, bub_sc, hb_sc, 1, True)


def _s5_scan(u_tm, bmat, avec, cmat, *, nb, s_len, l_len):
    rows = u_tm.shape[0]
    tc = S5_TC
    cr = tc * nb
    nchunk = rows // cr
    nctx = l_len // tc
    half = S5_GROUPS * S5_STATE

    def bwd(j):
        return jnp.where(j < nctx, nctx - 1 - j, nchunk - 1 - (j - nctx))

    return pl.pallas_call(
        functools.partial(_s5_kernel, nb=nb),
        grid=(nchunk,),
        in_specs=[pl.BlockSpec((cr, 256), lambda j: (j, 0)),
                  pl.BlockSpec((cr, 256), lambda j: (bwd(j), 0)),
                  pl.BlockSpec((256, 2 * half), lambda j: (0, 0)),
                  pl.BlockSpec((256, 2 * half), lambda j: (0, 0)),
                  pl.BlockSpec((2, 2, half), lambda j: (0, 0, 0)),
                  pl.BlockSpec((2 * half, 256), lambda j: (0, 0)),
                  pl.BlockSpec((2 * half, 256), lambda j: (0, 0))],
        out_specs=[pl.BlockSpec((cr, 256), lambda j: (j, 0)),
                   pl.BlockSpec((cr, 256), lambda j: (bwd(j), 0))],
        out_shape=[jax.ShapeDtypeStruct((rows, 256), F32)] * 2,
        scratch_shapes=[pltpu.VMEM((cr, 2 * half), F32), pltpu.VMEM((cr, 2 * half), F32),
                        pltpu.VMEM((nb, 2 * half), F32), pltpu.VMEM((nb, 2 * half), F32)],
        compiler_params=_cparams(("arbitrary",)),
        name="s5_scan",
    )(u_tm, u_tm, bmat[0], bmat[1], avec, cmat[0], cmat[1])


def _s5_finish_kernel(yf_ref, yb_ref, u_ref, d_ref, w_ref, o_ref):
    y = yf_ref[...] + yb_ref[...] + d_ref[...] * u_ref[...]
    r = _dot(y.astype(BF16), w_ref[...])
    o_ref[...] = r[:, :BRANCH_W] * jax.nn.sigmoid(r[:, BRANCH_W:])


def _s5_finish(yf, yb, u_tm, d_skip, w_glu):
    rows = u_tm.shape[0]
    spec = pl.BlockSpec((TM, 256), lambda i: (i, 0))
    return pl.pallas_call(
        _s5_finish_kernel,
        grid=(rows // TM,),
        in_specs=[spec, spec, spec, pl.BlockSpec((1, 256), lambda i: (0, 0)),
                  pl.BlockSpec((256, 512), lambda i: (0, 0))],
        out_specs=spec,
        out_shape=jax.ShapeDtypeStruct((rows, 256), F32),
        compiler_params=_cparams(("parallel",)),
        name="s5_glu",
    )(yf, yb, u_tm, d_skip, w_glu)


def _s5_params(lam_re, lam_im, log_dt, b_re, b_im, c_re, c_im):
    lam = lax.complex(lam_re, lam_im)
    dt = jnp.exp(log_dt)[..., None]
    a_bar = jnp.exp(lam * dt)
    b_bar = ((a_bar - 1.0) / lam)[..., None] * lax.complex(b_re, b_im)
    eye = jnp.eye(S5_GROUPS, dtype=F32)
    g, p, ch = S5_GROUPS, S5_STATE, S5_GROUP_CH

    def bdiag_in(m):
        return jnp.einsum('gh,dgpc->dgchp', eye, m).reshape(2, g * ch, g * p)

    def bdiag_out(m):
        return jnp.einsum('gh,dgcp->dgphc', eye, m).reshape(2, g * p, g * ch)

    bmat = jnp.concatenate([bdiag_in(jnp.real(b_bar)), bdiag_in(jnp.imag(b_bar))], axis=2).astype(BF16)
    cmat = jnp.concatenate([bdiag_out(c_re), -bdiag_out(c_im)], axis=1).astype(BF16)
    avec = jnp.stack([jnp.real(a_bar).reshape(2, g * p), jnp.imag(a_bar).reshape(2, g * p)], axis=1)
    return bmat, avec, cmat


def _merge_kernel(x_ref, mod_ref, o0_ref, o1_ref, o2_ref, o3_ref, wg_ref, bg_ref, wb_ref, wo_ref,
                  g_ref, b_ref, x1_ref, h2_ref):
    d = D_MODEL
    m = mod_ref[0]
    x = x_ref[...]
    hb = (x * (1.0 + m[1:2]) + m[0:1]).astype(BF16)
    acc = None
    for i, o_ref in enumerate((o0_ref, o1_ref, o2_ref, o3_ref)):
        gate = jax.nn.sigmoid(_dot(hb, wg_ref[:, i * d:(i + 1) * d]) + bg_ref[:, i * d:(i + 1) * d])
        term = gate * _dot(o_ref[...].astype(BF16), wb_ref[i])
        acc = term if acc is None else acc + term
    y = _dot(acc.astype(BF16), wo_ref[...])
    x1 = _layer_norm(ALPHA * x + m[2:3] * y, g_ref[...], b_ref[...])
    x1_ref[...] = x1
    h2_ref[...] = x1 * (1.0 + m[4:5]) + m[3:4]


def _merge(xa, modl, outs, wg, bg, wb, wo, g, b, seg, n):
    d = D_MODEL
    tok = pl.BlockSpec((TM, d), lambda i: (i, 0))
    br = pl.BlockSpec((TM, 256), lambda i: (i, 0))
    const = lambda a: pl.BlockSpec(a.shape, lambda i: (0,) * a.ndim)
    return pl.pallas_call(
        _merge_kernel,
        grid=(n // TM,),
        in_specs=[tok, pl.BlockSpec((1, 6, d), lambda i: (seg(i), 0, 0)), br, br, br, br,
                  const(wg), const(bg), const(wb), const(wo), const(g), const(b)],
        out_specs=[tok, tok],
        out_shape=[jax.ShapeDtypeStruct((n, d), F32)] * 2,
        compiler_params=_cparams(("parallel",), 56),
        name="merge_ln1",
    )(xa, modl, *outs, wg, bg, wb, wo, g, b)


def _start_row_gather(src_hbm, dst, idx_ref, base, n_rows, sem):
    def body(r, c):
        pltpu.make_async_copy(src_hbm.at[pl.ds(idx_ref[base + r], 1)], dst.at[pl.ds(r, 1)], sem).start()
        return c

    lax.fori_loop(0, n_rows, body, 0, unroll=8)


def _wait_row_gather(src_hbm, dst, sem):
    pltpu.make_async_copy(src_hbm.at[pl.ds(0, dst.shape[0])], dst, sem).wait()


def _swiglu_block(xb, w1_ref, w3_ref, w2_ref, o_ref):
    for c in range(D_FF // FF_CHUNK):
        cs = slice(c * FF_CHUNK, (c + 1) * FF_CHUNK)
        a = _dot(xb, w1_ref[0, :, cs])
        b = _dot(xb, w3_ref[0, :, cs])
        g = (a * jax.nn.sigmoid(a) * b).astype(BF16)
        y = _dot(g, w2_ref[0, cs, :])
        if c == 0:
            o_ref[...] = y
        else:
            o_ref[...] += y


def _ffn_kernel(be_ref, x_ref, w1_ref, w3_ref, w2_ref, o_ref):
    _swiglu_block(x_ref[...].astype(BF16), w1_ref, w3_ref, w2_ref, o_ref)


def _moe_ffn_kernel(be_ref, tok_ref, nused_ref, h_hbm, w1_ref, w3_ref, w2_ref, o_ref, xbuf, sem):
    i = pl.program_id(0)
    slot = i % 2

    @pl.when(i == 0)
    def _():
        _start_row_gather(h_hbm, xbuf.at[0], tok_ref, 0, MOE_BLK, sem.at[0])

    @pl.when(i + 1 < pl.num_programs(0))
    def _():
        _start_row_gather(h_hbm, xbuf.at[1 - slot], tok_ref, (i + 1) * MOE_BLK, MOE_BLK, sem.at[1 - slot])

    _wait_row_gather(h_hbm, xbuf.at[slot], sem.at[slot])

    @pl.when(i < nused_ref[0])
    def _():
        _swiglu_block(xbuf[slot].astype(BF16), w1_ref, w3_ref, w2_ref, o_ref)

    @pl.when(i >= nused_ref[0])
    def _():
        o_ref[...] = jnp.zeros(o_ref.shape, F32)


def _moe_ffn(h, block_e, buf_tok, n_used, w1, w3, w2):
    d = h.shape[1]
    n_rows = buf_tok.shape[0]
    wspec = lambda shape: pl.BlockSpec(shape, lambda i, be, tok, nu: (be[i], 0, 0), pipeline_mode=pl.Buffered(1))
    grid_spec = pltpu.PrefetchScalarGridSpec(
        num_scalar_prefetch=3,
        grid=(n_rows // MOE_BLK,),
        in_specs=[pl.BlockSpec(memory_space=pl.ANY), wspec((1, d, D_FF)), wspec((1, d, D_FF)), wspec((1, D_FF, d))],
        out_specs=pl.BlockSpec((MOE_BLK, d), lambda i, be, tok, nu: (i, 0)),
        scratch_shapes=[pltpu.VMEM((2, MOE_BLK, d), F32), pltpu.SemaphoreType.DMA((2,))],
    )
    return pl.pallas_call(
        _moe_ffn_kernel,
        grid_spec=grid_spec,
        out_shape=jax.ShapeDtypeStruct((n_rows, d), F32),
        compiler_params=_cparams(("arbitrary",), 56),
        name="moe_swiglu",
    )(block_e, buf_tok, n_used, h, w1, w3, w2)


def _ffn(xb, block_e, w1, w3, w2):
    n, d = xb.shape
    blk = MOE_BLK
    grid_spec = pltpu.PrefetchScalarGridSpec(
        num_scalar_prefetch=1,
        grid=(n // blk,),
        in_specs=[pl.BlockSpec((blk, d), lambda i, be: (i, 0)),
                  pl.BlockSpec((1, d, D_FF), lambda i, be: (be[i], 0, 0), pipeline_mode=pl.Buffered(1)),
                  pl.BlockSpec((1, d, D_FF), lambda i, be: (be[i], 0, 0), pipeline_mode=pl.Buffered(1)),
                  pl.BlockSpec((1, D_FF, d), lambda i, be: (be[i], 0, 0), pipeline_mode=pl.Buffered(1))],
        out_specs=pl.BlockSpec((blk, d), lambda i, be: (i, 0)),
    )
    return pl.pallas_call(
        _ffn_kernel,
        grid_spec=grid_spec,
        out_shape=jax.ShapeDtypeStruct((n, d), F32),
        compiler_params=_cparams(("arbitrary",), 56),
        name="swiglu",
    )(block_e, xb, w1, w3, w2)


def _router_kernel(h_ref, w_ref, o_ref):
    logits = jnp.dot(h_ref[...], w_ref[...], preferred_element_type=F32, precision=lax.Precision.HIGHEST)
    lane = lax.broadcasted_iota(jnp.int32, logits.shape, 1)
    lg = jnp.where(lane < N_EXPERTS, logits, NEG)
    v1 = jnp.max(lg, axis=-1, keepdims=True)
    i1 = jnp.min(jnp.where(lg == v1, lane, LANES), axis=-1, keepdims=True)
    lg2 = jnp.where(lane == i1, NEG, lg)
    v2 = jnp.max(lg2, axis=-1, keepdims=True)
    i2 = jnp.min(jnp.where(lg2 == v2, lane, LANES), axis=-1, keepdims=True)
    e = jnp.exp(v2 - v1)
    g1 = 1.0 / (1.0 + e)
    g2 = e / (1.0 + e)
    out = jnp.where(lane == 0, i1.astype(F32), jnp.where(lane == 1, i2.astype(F32),
                    jnp.where(lane == 2, g1, jnp.where(lane == 3, g2, 0.0))))
    o_ref[...] = out


def _router(h2, w_router_pad):
    n, d = h2.shape
    return pl.pallas_call(
        _router_kernel,
        grid=(n // TM,),
        in_specs=[pl.BlockSpec((TM, d), lambda i: (i, 0)), pl.BlockSpec((d, LANES), lambda i: (0, 0))],
        out_specs=pl.BlockSpec((TM, LANES), lambda i: (i, 0)),
        out_shape=jax.ShapeDtypeStruct((n, LANES), F32),
        compiler_params=_cparams(("parallel",)),
        name="router",
    )(h2, w_router_pad)


def _combine_kernel(x_ref, mod_ref, f_ref, g_ref, b_ref, o_ref):
    m = mod_ref[0]
    o_ref[...] = _layer_norm(ALPHA * x_ref[...] + m[5:6] * f_ref[...], g_ref[...], b_ref[...])


def _combine(x1, modl, f, g, b, seg):
    n, d = x1.shape
    tok = pl.BlockSpec((TM, d), lambda i: (i, 0))
    vec = pl.BlockSpec((1, d), lambda i: (0, 0))
    return pl.pallas_call(
        _combine_kernel,
        grid=(n // TM,),
        in_specs=[tok, pl.BlockSpec((1, 6, d), lambda i: (seg(i), 0, 0)), tok, vec, vec],
        out_specs=tok,
        out_shape=jax.ShapeDtypeStruct((n, d), F32),
        compiler_params=_cparams(("parallel",)),
        name="combine_ln2",
    )(x1, modl, f, g, b)


def _moe_combine_kernel(dest_ref, x_ref, mod_ref, gt_ref, y_hbm, g_ref, b_ref, o_ref, ybuf, sem):
    i = pl.program_id(0)
    slot = i % 2

    @pl.when(i == 0)
    def _():
        _start_row_gather(y_hbm, ybuf.at[0], dest_ref, 0, 2 * TM, sem.at[0])

    @pl.when(i + 1 < pl.num_programs(0))
    def _():
        _start_row_gather(y_hbm, ybuf.at[1 - slot], dest_ref, (i + 1) * 2 * TM, 2 * TM, sem.at[1 - slot])

    _wait_row_gather(y_hbm, ybuf.at[slot], sem.at[slot])
    gt = gt_ref[...]
    f = gt[:, 2:3] * ybuf[slot, 0:TM, :] + gt[:, 3:4] * ybuf[slot, TM:2 * TM, :]
    m = mod_ref[0]
    o_ref[...] = _layer_norm(ALPHA * x_ref[...] + m[5:6] * f, g_ref[...], b_ref[...])


def _moe_combine(x1, modl, y_rows, dest_tiles, gates, g, b, seg):
    n, d = x1.shape
    tok = pl.BlockSpec((TM, d), lambda i, dst: (i, 0))
    vec = pl.BlockSpec((1, d), lambda i, dst: (0, 0))
    grid_spec = pltpu.PrefetchScalarGridSpec(
        num_scalar_prefetch=1,
        grid=(n // TM,),
        in_specs=[tok, pl.BlockSpec((1, 6, d), lambda i, dst: (seg(i), 0, 0)),
                  pl.BlockSpec((TM, LANES), lambda i, dst: (i, 0)), pl.BlockSpec(memory_space=pl.ANY), vec, vec],
        out_specs=tok,
        scratch_shapes=[pltpu.VMEM((2, 2 * TM, d), F32), pltpu.SemaphoreType.DMA((2,))],
    )
    return pl.pallas_call(
        _moe_combine_kernel,
        grid_spec=grid_spec,
        out_shape=jax.ShapeDtypeStruct((n, d), F32),
        compiler_params=_cparams(("arbitrary",)),
        name="moe_combine_ln2",
    )(dest_tiles, x1, modl, gates, y_rows, g, b)


def _rope_tables(s_len, width, rope_lane, extra_rows):
    t = jnp.arange(s_len)
    lane = np.arange(width)
    inv = ROPE_BASE ** (-(jnp.asarray(lane % 8, F32)) / 8.0)
    pos = jnp.where((lane & 16) == 0, (t // GRID_W)[:, None], (t % GRID_W)[:, None]).astype(F32)
    ang = pos * inv[None, :]
    cos = jnp.where(rope_lane[None, :], jnp.cos(ang), 1.0)
    sin = jnp.where(rope_lane[None, :], jnp.where((lane & 8) == 0, -jnp.sin(ang), jnp.sin(ang)), 0.0)
    cos = jnp.concatenate([cos, jnp.ones((extra_rows, width), F32)])
    sin = jnp.concatenate([sin, jnp.zeros((extra_rows, width), F32)])
    return cos, sin


def _pad_heads(w, n_heads, width):
    k = w.shape[0]
    w = w.reshape(k, n_heads, -1)
    return jnp.pad(w, ((0, 0), (0, 0), (0, width - w.shape[-1]))).reshape(k, n_heads * width)


def _route(top, n_tok):
    e_flat = top[:, 0:2].astype(jnp.int32).reshape(-1)
    n_assign = 2 * n_tok
    onehot = (e_flat[:, None] == jnp.arange(N_EXPERTS)[None, :]).astype(jnp.int32)
    csum = jnp.cumsum(onehot, axis=0)
    rank = jnp.take_along_axis(csum, e_flat[:, None], axis=1)[:, 0] - 1
    counts = csum[-1]
    padded = (counts + MOE_BLK - 1) // MOE_BLK * MOE_BLK
    pad_end = jnp.cumsum(padded)
    pad_start = pad_end - padded
    dest = pad_start[e_flat] + rank
    n_rows = -(-n_assign // MOE_BLK) * MOE_BLK + N_EXPERTS * MOE_BLK
    buf_tok = jnp.zeros((n_rows,), jnp.int32).at[dest].set(jnp.arange(n_assign, dtype=jnp.int32) // 2)
    block_e = jnp.minimum(jnp.searchsorted(pad_end, jnp.arange(n_rows // MOE_BLK) * MOE_BLK, side='right'),
                          N_EXPERTS - 1).astype(jnp.int32)
    n_used = (pad_end[-1:] // MOE_BLK).astype(jnp.int32)
    dest_tiles = dest.astype(jnp.int32).reshape(n_tok // TM, TM, 2).transpose(0, 2, 1).reshape(-1)
    return buf_tok, block_e, n_used, dest_tiles


def kernel(x, c, ctx, c_ctx, w_ada, b_ada, w_in, na_rpb, mla_q_norm, mla_kv_norm, mla_w_uq, mla_w_ukv,
           s5_lam_re, s5_lam_im, s5_log_dt, s5_b_re, s5_b_im, s5_c_re, s5_c_im, s5_d, s5_w_glu,
           diff_lam_q1, diff_lam_k1, diff_lam_q2, diff_lam_k2, diff_subln,
           w_branch, w_gate, b_gate, w_out, ln1_g, ln1_b, ln2_g, ln2_b,
           ffn_w1, ffn_w3, ffn_w2, moe_router, moe_w1, moe_w3, moe_w2):
    nb, s_len, d = x.shape
    l_len = ctx.shape[1]
    n_lat, n_ctx = nb * s_len, nb * l_len
    n_tot = n_lat + n_ctx
    rows = s_len // GRID_W
    assert d == D_MODEL and s_len % TM == 0 and n_ctx % TM == 0 and rows % NA_QROWS == 0 and rows >= NA_KROWS
    assert l_len % S5_TC == 0 and s_len % S5_TC == 0

    def seg(i):
        return jnp.minimum((i * TM) // s_len, nb)

    n_lat_tiles = n_lat // TM

    def tab_idx(i):
        return jnp.where(i < n_lat_tiles, i % (s_len // TM), s_len // TM)

    xa = jnp.concatenate([x.reshape(n_lat, d), ctx.reshape(n_ctx, d)], axis=0)
    nrow_mod = -(-(nb + 1) // 16) * 16
    cvec = jnp.zeros((nrow_mod, d), F32).at[:nb].set(c).at[nb].set(c_ctx)
    mod_all = _ada_all(cvec, w_ada, b_ada).reshape(DEPTH, nrow_mod, 6, d)

    lane512 = np.arange(512)
    q_rope = (lane512 % LANES >= MLA_NOPE) & (lane512 % LANES < MLA_NOPE + MLA_ROPE)
    cq, sq = _rope_tables(s_len, 512, q_rope, TM)
    ck, sk = _rope_tables(s_len, LANES, np.arange(LANES) < MLA_ROPE, TM)
    cd, sd = _rope_tables(s_len, 256, np.ones((256,), bool), TM)
    tabs = (cq, sq, ck, sk, cd, sd)
    place = np.zeros((LANES, MLA_HEADS * LANES), np.float32)
    for h in range(MLA_HEADS):
        place[np.arange(MLA_ROPE), h * LANES + MLA_NOPE + np.arange(MLA_ROPE)] = 1.0
    place = jnp.asarray(place, BF16)

    for layer in range(DEPTH):
        ctx_out = layer < DEPTH - 1
        lam_init = 0.8 - 0.6 * math.exp(-0.3 * layer)
        modl = mod_all[layer]

        wi = w_in[layer]
        w_in_pad = jnp.concatenate([wi[:, :1184], jnp.zeros((d, 96), F32), wi[:, 1184:]], axis=1).astype(BF16)
        wuq = _pad_heads(mla_w_uq[layer], MLA_HEADS, LANES).astype(BF16)
        wukv4 = mla_w_ukv[layer].reshape(-1, MLA_HEADS, MLA_NOPE + MLA_V)
        wuk = jnp.pad(wukv4[:, :, :MLA_NOPE], ((0, 0), (0, 0), (0, LANES - MLA_NOPE))).reshape(-1, MLA_HEADS * LANES)
        wuv = wukv4[:, :, MLA_NOPE:].reshape(-1, MLA_HEADS * MLA_V)
        wukv = jnp.concatenate([wuk, wuv], axis=1).astype(BF16)
        bias = _na_bias_tables(na_rpb[layer], rows)
        bmat, avec, cmat = _s5_params(s5_lam_re[layer], s5_lam_im[layer], s5_log_dt[layer], s5_b_re[layer],
                                      s5_b_im[layer], s5_c_re[layer], s5_c_im[layer])
        lam = (jnp.exp(jnp.sum(diff_lam_q1[layer] * diff_lam_k1[layer]))
               - jnp.exp(jnp.sum(diff_lam_q2[layer] * diff_lam_k2[layer])) + lam_init)
        lam_vec = jnp.full((1, LANES), lam, F32)
        subln = jnp.tile(diff_subln[layer], DIFF_HEADS).reshape(1, 256)

        z = _inproj(xa, modl, w_in_pad, seg)
        qm, km, vm, qd, kd, vd, qna = _prep(z, tabs, mla_q_norm[layer].reshape(1, -1),
                                            mla_kv_norm[layer].reshape(1, -1), wuq, wukv, place, tab_idx)
        common = dict(nb=nb, s_len=s_len, l_len=l_len)
        o_na = _na_attention(qna, z, bias, **common)
        o_mla = _flash(qm, km, vm, heads=MLA_HEADSPEC, kw=512, name="mla_attn", **common)
        o_diff = _flash(qd, kd, vd, heads=DIFF_HEADSPEC, kw=256, diff=True, lam=lam_vec,
                        subln=subln, lam_init=lam_init, name="diff_attn", **common)
        u = z[:, Z_S5:Z_S5 + 256]
        u_tm = jnp.concatenate([u[n_lat:].reshape(nb, l_len, 256).transpose(1, 0, 2).reshape(n_ctx, 256),
                                u[:n_lat].reshape(nb, s_len, 256).transpose(1, 0, 2).reshape(n_lat, 256)], axis=0)
        yf, yb = _s5_scan(u_tm, bmat, avec, cmat, **common)
        o_s5_tm = _s5_finish(yf, yb, u_tm, s5_d[layer].reshape(1, 256), s5_w_glu[layer].astype(BF16))
        o_s5_lat = o_s5_tm[n_ctx:].reshape(s_len, nb, 256).transpose(1, 0, 2).reshape(n_lat, 256)
        if ctx_out:
            o_na_c = _flash(qna, z, z, heads=NA_HEADSPEC, kcol=Z_NAK // 256, vcol=Z_NAV // 256,
                            kw=256, with_lat=False, name="na_ctx_attn", **common)
            o_mla_c = _flash(qm, km, vm, heads=MLA_HEADSPEC, kw=512, with_lat=False,
                             name="mla_ctx_attn", **common)
            o_diff_c = _flash(qd, kd, vd, heads=DIFF_HEADSPEC, kw=256, diff=True, lam=lam_vec,
                              subln=subln, lam_init=lam_init, with_lat=False, name="diff_ctx_attn", **common)
            o_s5_c = o_s5_tm[:n_ctx].reshape(l_len, nb, 256).transpose(1, 0, 2).reshape(n_ctx, 256)
            outs = [jnp.concatenate([a, b_], axis=0) for a, b_ in
                    ((o_na, o_na_c), (o_mla, o_mla_c), (o_s5_lat, o_s5_c), (o_diff, o_diff_c))]
            n_act = n_tot
        else:
            outs = [o_na, o_mla, o_s5_lat, o_diff]
            n_act = n_lat
            xa = xa[:n_lat]
        x1, h2 = _merge(xa, modl, outs, w_gate[layer].astype(BF16), b_gate[layer].reshape(1, -1),
                        w_branch[layer].astype(BF16), w_out[layer].astype(BF16),
                        ln1_g[layer].reshape(1, d), ln1_b[layer].reshape(1, d), seg, n_act)

        jj = layer // 2
        g2, b2 = ln2_g[layer].reshape(1, d), ln2_b[layer].reshape(1, d)
        if layer % 2 == 0:
            be = jnp.zeros((n_act // MOE_BLK,), jnp.int32)
            f = _ffn(h2, be, ffn_w1[jj:jj + 1].astype(BF16), ffn_w3[jj:jj + 1].astype(BF16),
                     ffn_w2[jj:jj + 1].astype(BF16))
            xa = _combine(x1, modl, f, g2, b2, seg)
        else:
            wr = jnp.pad(moe_router[jj], ((0, 0), (0, LANES - N_EXPERTS)))
            top = _router(h2, wr)
            buf_tok, block_e, n_used, dest_tiles = _route(top, n_act)
            y_rows = _moe_ffn(h2, block_e, buf_tok, n_used, moe_w1[jj].astype(BF16), moe_w3[jj].astype(BF16),
                              moe_w2[jj].astype(BF16))
            xa = _moe_combine(x1, modl, y_rows, dest_tiles, top, g2, b2, seg)
    return xa[:n_lat].reshape(nb, s_len, d)
```

```python
import functools
import math

import jax
import jax.numpy as jnp
import numpy as np
from jax import lax
from jax.experimental import pallas as pl
from jax.experimental.pallas import tpu as pltpu

F32 = jnp.float32
BF16 = jnp.bfloat16

D_MODEL = 1024
DEPTH = 4
GRID_W = 64
BRANCH_W = 256
NA_HEADS = 4
NA_HD = 64
NA_KH = 8
NA_KW = 16
MLA_HEADS = 4
MLA_NOPE = 64
MLA_ROPE = 32
MLA_V = 64
S5_GROUP_CH = 16
S5_GROUPS = 16
S5_STATE = 64
DIFF_HEADS = 4
DIFF_HD = 32
D_FF = 3584
N_EXPERTS = 8
ROPE_BASE = 10000.0
LN_EPS = 1e-5
RMS_EPS = 1e-6
ALPHA = (2 * DEPTH) ** 0.25
NA_SCALE = NA_HD ** -0.5
MLA_SCALE = (MLA_NOPE + MLA_ROPE) ** -0.5
DIFF_SCALE = DIFF_HD ** -0.5
LOG2E = 1.4426950408889634

LANES = 128
ZW = 2304
Z_NAQ, Z_NAK, Z_NAV, Z_MQ, Z_MKV, Z_MKR, Z_S5, Z_DQ, Z_DK, Z_DV = (
    0, 256, 512, 768, 1024, 1152, 1280, 1536, 1792, 2048)
NEG = -1e30
TM = 512
TQ = 512
TK = 256
NA_QROWS = 8
NA_KROWS = NA_QROWS + NA_KH
S5_TC = 128
FF_CHUNK = 512
MOE_BLK = 512


def _cparams(sem, vmem_mb=48):
    return pltpu.CompilerParams(dimension_semantics=sem, vmem_limit_bytes=vmem_mb << 20)


def _dot(a, b):
    return jnp.dot(a, b, preferred_element_type=F32)


def _dot_nt(a, b):
    return lax.dot_general(a, b, (((1,), (1,)), ((), ())), preferred_element_type=F32)


def _layer_norm(r, g, b):
    rc = r - jnp.mean(r, axis=-1, keepdims=True)
    var = jnp.mean(rc * rc, axis=-1, keepdims=True)
    return rc * lax.rsqrt(var + LN_EPS) * g + b


def _ada_kernel(c_ref, w_ref, b_ref, o_ref):
    c = c_ref[...]
    cond = c * jax.nn.sigmoid(c)
    o_ref[0] = _dot(cond.astype(BF16), w_ref[0].astype(BF16)) + b_ref[0]


def _ada_all(cvec, w_ada, b_ada):
    nrow = cvec.shape[0]
    d = D_MODEL
    return pl.pallas_call(
        _ada_kernel,
        grid=(DEPTH, 6),
        in_specs=[pl.BlockSpec((nrow, d), lambda l, n: (0, 0)),
                  pl.BlockSpec((1, d, d), lambda l, n: (l, 0, n)),
                  pl.BlockSpec((1, 1, d), lambda l, n: (l, 0, n))],
        out_specs=pl.BlockSpec((1, nrow, d), lambda l, n: (l, 0, n)),
        out_shape=jax.ShapeDtypeStruct((DEPTH, nrow, 6 * d), F32),
        compiler_params=_cparams(("parallel", "parallel")),
        name="ada_mod",
    )(cvec, w_ada, b_ada.reshape(DEPTH, 1, 6 * d))


def _inproj_kernel(x_ref, mod_ref, w_ref, o_ref):
    m = mod_ref[0]
    h = x_ref[...] * (1.0 + m[1:2]) + m[0:1]
    o_ref[...] = _dot(h.astype(BF16), w_ref[...])


def _inproj(xa, modl, w_in_pad, seg):
    n, d = xa.shape
    return pl.pallas_call(
        _inproj_kernel,
        grid=(n // TM,),
        in_specs=[pl.BlockSpec((TM, d), lambda i: (i, 0)),
                  pl.BlockSpec((1, 6, d), lambda i: (seg(i), 0, 0)),
                  pl.BlockSpec((d, ZW), lambda i: (0, 0))],
        out_specs=pl.BlockSpec((TM, ZW), lambda i: (i, 0)),
        out_shape=jax.ShapeDtypeStruct((n, ZW), F32),
        compiler_params=_cparams(("parallel",)),
        name="in_proj",
    )(xa, modl, w_in_pad)


def _swap8(z):
    w = z.shape[-1]
    lane = lax.broadcasted_iota(jnp.int32, z.shape, 1)
    up = pltpu.roll(z, w - 8, 1)
    dn = pltpu.roll(z, 8, 1)
    return jnp.where((lane & 8) == 0, up, dn)


def _rms(z, g):
    return z * lax.rsqrt(jnp.mean(z * z, axis=-1, keepdims=True) + RMS_EPS) * g


def _prep_kernel(naq_ref, mq_ref, mkv_ref, mkr_ref, dq_ref, dk_ref, dv_ref,
                 cq_ref, sq_ref, ck_ref, sk_ref, cd_ref, sd_ref,
                 qn_ref, kvn_ref, wuq_ref, wukv_ref, place_ref,
                 qm_ref, km_ref, vm_ref, qd_ref, kd_ref, vd_ref, qna_ref):
    aq = _rms(mq_ref[...], qn_ref[...])
    q = _dot(aq.astype(BF16), wuq_ref[...])
    q = q * cq_ref[...] + _swap8(q) * sq_ref[...]
    qm_ref[...] = (q * (MLA_SCALE * LOG2E)).astype(BF16)
    akv = _rms(mkv_ref[...], kvn_ref[...])
    kv = _dot(akv.astype(BF16), wukv_ref[...])
    kr = mkr_ref[...]
    kr = kr * ck_ref[...] + _swap8(kr) * sk_ref[...]
    km_ref[...] = (kv[:, :4 * LANES] + _dot(kr.astype(BF16), place_ref[...])).astype(BF16)
    vm_ref[...] = kv[:, 4 * LANES:].astype(BF16)
    dq = dq_ref[...]
    dq = (dq * cd_ref[...] + _swap8(dq) * sd_ref[...]) * (DIFF_SCALE * LOG2E)
    dk = dk_ref[...]
    dk = dk * cd_ref[...] + _swap8(dk) * sd_ref[...]
    kd_ref[...] = dk.astype(BF16)
    vd_ref[...] = dv_ref[...].astype(BF16)
    lane = lax.broadcasted_iota(jnp.int32, (dq.shape[0], LANES), 1)
    for g in range(2 * DIFF_HEADS):
        blk = dq[:, (g // 4) * LANES:(g // 4 + 1) * LANES]
        qd_ref[:, g * LANES:(g + 1) * LANES] = jnp.where((lane // DIFF_HD) == (g % 4), blk, 0.0).astype(BF16)
    naq = naq_ref[...] * (NA_SCALE * LOG2E)
    for h in range(NA_HEADS):
        blk = naq[:, (h // 2) * LANES:(h // 2 + 1) * LANES]
        qna_ref[:, h * LANES:(h + 1) * LANES] = jnp.where((lane // NA_HD) == (h % 2), blk, 0.0).astype(BF16)


def _prep(z, tabs, qn, kvn, wuq, wukv, place, tab_idx):
    n = z.shape[0]
    cq, sq, ck, sk, cd, sd = tabs

    def zspec(width, col):
        return pl.BlockSpec((TM, width), lambda i: (i, col))

    def tspec(width):
        return pl.BlockSpec((TM, width), lambda i: (tab_idx(i), 0))

    def wspec(a):
        return pl.BlockSpec(a.shape, lambda i: (0,) * a.ndim)

    def ospec(width):
        return pl.BlockSpec((TM, width), lambda i: (i, 0))

    outs = [(512, "qm"), (512, "km"), (256, "vm"), (1024, "qd"), (256, "kd"), (256, "vd"), (512, "qna")]
    return pl.pallas_call(
        _prep_kernel,
        grid=(n // TM,),
        in_specs=[zspec(256, Z_NAQ // 256), zspec(256, Z_MQ // 256), zspec(128, Z_MKV // 128),
                  zspec(128, Z_MKR // 128), zspec(256, Z_DQ // 256), zspec(256, Z_DK // 256),
                  zspec(256, Z_DV // 256),
                  tspec(512), tspec(512), tspec(128), tspec(128), tspec(256), tspec(256),
                  wspec(qn), wspec(kvn), wspec(wuq), wspec(wukv), wspec(place)],
        out_specs=[ospec(w) for w, _ in outs],
        out_shape=[jax.ShapeDtypeStruct((n, w), BF16) for w, _ in outs],
        compiler_params=_cparams(("parallel",)),
        name="attn_prep",
    )(z, z, z, z, z, z, z, cq, sq, ck, sk, cd, sd, qn, kvn, wuq, wukv, place)


def _flash_kernel(*refs, heads, n_acc, with_lat, diff, lam_init, tk):
    refs = list(refs)
    q_ref = refs.pop(0)
    if with_lat:
        kl_ref, vl_ref = refs.pop(0), refs.pop(0)
    kc_ref, vc_ref = refs.pop(0), refs.pop(0)
    if diff:
        lam_ref, sub_ref = refs.pop(0), refs.pop(0)
    o_ref, m_sc, l_sc, acc_sc = refs
    tq = q_ref.shape[0]
    lane = lax.broadcasted_iota(jnp.int32, (tq, LANES), 1)
    lo_half = lane < 64

    def tile(k_ref, v_ref, rows):
        nkb = (rows.stop - rows.start if isinstance(rows, slice) else rows.size) // LANES
        for g, (kb, vb, vh, ai) in enumerate(heads):
            q = q_ref[:, g * LANES:(g + 1) * LANES]
            k = k_ref[rows, kb * LANES:(kb + 1) * LANES].astype(BF16)
            s = _dot_nt(q, k)
            blocks = [s[:, c * LANES:(c + 1) * LANES] for c in range(nkb)]
            mx = blocks[0]
            for blk in blocks[1:]:
                mx = jnp.maximum(mx, blk)
            m_prev = m_sc[g]
            m_new = jnp.maximum(m_prev, jnp.max(mx, axis=-1, keepdims=True))
            alpha = jnp.exp2(m_prev - m_new)
            ps = [jnp.exp2(blk - m_new) for blk in blocks]
            lsum = ps[0]
            for pb in ps[1:]:
                lsum = lsum + pb
            l_sc[g] = alpha * l_sc[g] + lsum
            m_sc[g] = m_new
            p = jnp.concatenate([pb.astype(BF16) for pb in ps], axis=1)
            v = v_ref[rows, vb * LANES:(vb + 1) * LANES].astype(BF16)
            pv = _dot(p, v)
            old = acc_sc[ai, :, vb * LANES:(vb + 1) * LANES]
            mine = lo_half if vh == 0 else jnp.logical_not(lo_half)
            acc_sc[ai, :, vb * LANES:(vb + 1) * LANES] = jnp.where(mine, alpha * old + pv, old)

    m_sc[...] = jnp.full(m_sc.shape, NEG, F32)
    l_sc[...] = jnp.zeros(l_sc.shape, F32)
    acc_sc[...] = jnp.zeros(acc_sc.shape, F32)
    tile(kc_ref, vc_ref, slice(0, kc_ref.shape[0]))

    if with_lat:
        def kv_step(j, carry):
            tile(kl_ref, vl_ref, pl.ds(pl.multiple_of(j * tk, tk), tk))
            return carry

        lax.fori_loop(0, kl_ref.shape[0] // tk, kv_step, 0, unroll=4)

    def inv_l(g):
        return 1.0 / jnp.sum(l_sc[g], axis=-1, keepdims=True)

    for c in range(2):
        if not diff:
            o = jnp.where(lo_half, acc_sc[0, :, c * LANES:(c + 1) * LANES] * inv_l(2 * c),
                          acc_sc[0, :, c * LANES:(c + 1) * LANES] * inv_l(2 * c + 1))
        else:
            o1 = jnp.where(lo_half, acc_sc[0, :, c * LANES:(c + 1) * LANES] * inv_l(4 * c),
                           acc_sc[0, :, c * LANES:(c + 1) * LANES] * inv_l(4 * c + 2))
            o2 = jnp.where(lo_half, acc_sc[1, :, c * LANES:(c + 1) * LANES] * inv_l(4 * c + 1),
                           acc_sc[1, :, c * LANES:(c + 1) * LANES] * inv_l(4 * c + 3))
            o = o1 - lam_ref[...] * o2
            sq = o * o
            ms_lo = jnp.sum(jnp.where(lo_half, sq, 0.0), axis=-1, keepdims=True) * (1.0 / 64)
            ms_hi = jnp.sum(jnp.where(lo_half, 0.0, sq), axis=-1, keepdims=True) * (1.0 / 64)
            rs = jnp.where(lo_half, lax.rsqrt(ms_lo + RMS_EPS), lax.rsqrt(ms_hi + RMS_EPS))
            o = o * rs * sub_ref[:, c * LANES:(c + 1) * LANES] * (1.0 - lam_init)
        o_ref[:, c * LANES:(c + 1) * LANES] = o


def _flash(q, k, v, *, nb, s_len, l_len, heads, kcol=0, vcol=0, kw=None, vw=256,
           with_lat=True, diff=False, lam=None, subln=None, lam_init=0.0, name="flash"):
    nh = len(heads)
    n_acc = 2 if diff else 1
    ctx0 = nb * s_len // l_len
    tk = min(TK, s_len)
    if with_lat:
        tq = min(TQ, s_len)
        grid = (nb, s_len // tq)
        rows = nb * s_len
        qmap = lambda b, i: (b * (s_len // tq) + i, 0)
    else:
        tq = l_len
        grid = (nb, 1)
        rows = nb * l_len
        qmap = lambda b, i: (ctx0 + b, 0)
    in_specs = [pl.BlockSpec((tq, nh * LANES), qmap)]
    args = [q]
    if with_lat:
        in_specs += [pl.BlockSpec((s_len, kw), lambda b, i: (b, kcol)),
                     pl.BlockSpec((s_len, vw), lambda b, i: (b, vcol))]
        args += [k, v]
    in_specs += [pl.BlockSpec((l_len, kw), lambda b, i: (ctx0 + b, kcol)),
                 pl.BlockSpec((l_len, vw), lambda b, i: (ctx0 + b, vcol))]
    args += [k, v]
    if diff:
        in_specs += [pl.BlockSpec((1, LANES), lambda b, i: (0, 0)),
                     pl.BlockSpec((1, 256), lambda b, i: (0, 0))]
        args += [lam, subln]
    return pl.pallas_call(
        functools.partial(_flash_kernel, heads=heads, n_acc=n_acc, with_lat=with_lat,
                          diff=diff, lam_init=lam_init, tk=tk),
        grid=grid,
        in_specs=in_specs,
        out_specs=pl.BlockSpec((tq, 256), qmap if with_lat else (lambda b, i: (b, 0))),
        out_shape=jax.ShapeDtypeStruct((rows, 256), F32),
        scratch_shapes=[pltpu.VMEM((nh, tq, LANES), F32), pltpu.VMEM((nh, tq, LANES), F32),
                        pltpu.VMEM((n_acc, tq, 256), F32)],
        compiler_params=_cparams(("parallel", "parallel")),
        name=name,
    )(*args)


MLA_HEADSPEC = tuple((h, h // 2, h % 2, 0) for h in range(MLA_HEADS))
NA_HEADSPEC = tuple((h // 2, h // 2, h % 2, 0) for h in range(NA_HEADS))
DIFF_HEADSPEC = tuple((g // 4, g // 4, (g // 2) % 2, g % 2) for g in range(2 * DIFF_HEADS))


def _na_kernel(q_ref, k_ref, v_ref, kc_ref, vc_ref, bias_ref, o_ref, *, rows):
    rb = pl.program_id(1)
    k0 = jnp.clip(NA_QROWS * rb - NA_KH // 2, 0, rows - NA_KROWS)
    start = pl.multiple_of(k0 * GRID_W, GRID_W)
    nk = NA_KROWS * GRID_W
    kw = k_ref[pl.ds(start, nk), :].astype(BF16)
    vw = v_ref[pl.ds(start, nk), :].astype(BF16)
    kc = kc_ref[...].astype(BF16)
    vc = vc_ref[...].astype(BF16)
    tq = q_ref.shape[0]
    lane = lax.broadcasted_iota(jnp.int32, (tq, LANES), 1)
    lo_half = lane < 64
    for c in range(2):
        o_c = jnp.zeros((tq, LANES), F32)
        for half in range(2):
            h = 2 * c + half
            q = q_ref[:, h * LANES:(h + 1) * LANES]
            s_loc = _dot_nt(q, kw[:, c * LANES:(c + 1) * LANES]) + bias_ref[0, h]
            s_ctx = _dot_nt(q, kc[:, c * LANES:(c + 1) * LANES])
            m = jnp.maximum(jnp.max(s_loc, axis=-1, keepdims=True), jnp.max(s_ctx, axis=-1, keepdims=True))
            p_loc = jnp.exp2(s_loc - m)
            p_ctx = jnp.exp2(s_ctx - m)
            l = jnp.sum(p_loc, axis=-1, keepdims=True) + jnp.sum(p_ctx, axis=-1, keepdims=True)
            pv = (_dot(p_loc.astype(BF16), vw[:, c * LANES:(c + 1) * LANES])
                  + _dot(p_ctx.astype(BF16), vc[:, c * LANES:(c + 1) * LANES]))
            mine = lo_half if half == 0 else jnp.logical_not(lo_half)
            o_c = jnp.where(mine, pv * (1.0 / l), o_c)
        o_ref[:, c * LANES:(c + 1) * LANES] = o_c


def _na_attention(qna, z, bias, *, nb, s_len, l_len):
    rows = s_len // GRID_W
    tq = NA_QROWS * GRID_W
    nrb = rows // NA_QROWS
    ctx0 = nb * s_len // l_len

    def variant(rb):
        return jnp.where(rb == 0, 0, jnp.where(rb == nrb - 1, 2, 1))

    return pl.pallas_call(
        functools.partial(_na_kernel, rows=rows),
        grid=(nb, nrb),
        in_specs=[pl.BlockSpec((tq, NA_HEADS * LANES), lambda b, r: (b * nrb + r, 0)),
                  pl.BlockSpec((s_len, 256), lambda b, r: (b, Z_NAK // 256)),
                  pl.BlockSpec((s_len, 256), lambda b, r: (b, Z_NAV // 256)),
                  pl.BlockSpec((l_len, 256), lambda b, r: (ctx0 + b, Z_NAK // 256)),
                  pl.BlockSpec((l_len, 256), lambda b, r: (ctx0 + b, Z_NAV // 256)),
                  pl.BlockSpec((1, NA_HEADS, tq, NA_KROWS * GRID_W), lambda b, r: (variant(r), 0, 0, 0))],
        out_specs=pl.BlockSpec((tq, 256), lambda b, r: (b * nrb + r, 0)),
        out_shape=jax.ShapeDtypeStruct((nb * s_len, 256), F32),
        compiler_params=_cparams(("parallel", "arbitrary"), 56),
        name="na_attn",
    )(qna, z, z, z, z, bias)


def _na_bias_tables(rpb, rows):
    a = np.arange(NA_QROWS)
    qc = np.arange(GRID_W)
    kr_rel = np.arange(NA_KROWS)
    kc = np.arange(GRID_W)
    col0 = np.clip(qc - NA_KW // 2, 0, GRID_W - NA_KW)
    col_valid = (kc[None, :] >= col0[:, None]) & (kc[None, :] < col0[:, None] + NA_KW)
    col_off = np.clip(kc[None, :] - qc[:, None] + (NA_KW - 1), 0, 2 * NA_KW - 2)
    oh_col = (col_off[:, :, None] == np.arange(2 * NA_KW - 1)).astype(np.float32)
    big = 10 ** 6
    out = []
    for r_blk, k0, rows_eff in ((0, 0, big), (NA_QROWS, NA_KH // 2, big), (rows - NA_QROWS, rows - NA_KROWS, rows)):
        qr = r_blk + a
        r0 = np.clip(qr - NA_KH // 2, 0, rows_eff - NA_KH)
        kr = k0 + kr_rel
        row_valid = (kr[None, :] >= r0[:, None]) & (kr[None, :] < r0[:, None] + NA_KH)
        row_off = np.clip(kr[None, :] - qr[:, None] + (NA_KH - 1), 0, 2 * NA_KH - 2)
        oh_row = (row_off[:, :, None] == np.arange(2 * NA_KH - 1)).astype(np.float32)
        b = jnp.einsum('akr,hrc,qlc->haqkl', oh_row, rpb, oh_col, precision=lax.Precision.HIGHEST)
        valid = row_valid[:, None, :, None] & col_valid[None, :, None, :]
        b = jnp.where(valid[None], b * LOG2E, NEG)
        out.append(b.reshape(NA_HEADS, NA_QROWS * GRID_W, NA_KROWS * GRID_W))
    return jnp.stack(out)


def _s5_kernel(uf_ref, ub_ref, bf_ref, bb_ref, a_ref, cf_ref, cb_ref, yf_ref, yb_ref,
               buf_sc, bub_sc, hf_sc, hb_sc, *, nb):
    j = pl.program_id(0)
    tc = uf_ref.shape[0] // nb
    half = S5_GROUPS * S5_STATE

    @pl.when(j == 0)
    def _():
        hf_sc[...] = jnp.zeros(hf_sc.shape, F32)
        hb_sc[...] = jnp.zeros(hb_sc.shape, F32)

    def scan(u_ref, b_ref, c_ref, y_ref, bu_sc, h_sc, d, reverse):
        bu_sc[...] = _dot(u_ref[...].astype(BF16), b_ref[...])
        ar = jnp.broadcast_to(a_ref[d, 0:1, :], (nb, half))
        ai = jnp.broadcast_to(a_ref[d, 1:2, :], (nb, half))

        def step(t, carry):
            hr, hi = carry
            tt = (tc - 1 - t) if reverse else t
            row = pl.multiple_of(tt * nb, nb)
            nr = ar * hr - ai * hi + bu_sc[pl.ds(row, nb), 0:half]
            ni = ar * hi + ai * hr + bu_sc[pl.ds(row, nb), half:2 * half]
            bu_sc[pl.ds(row, nb), 0:half] = nr
            bu_sc[pl.ds(row, nb), half:2 * half] = ni
            return nr, ni

        hr, hi = lax.fori_loop(0, tc, step, (h_sc[:, 0:half], h_sc[:, half:2 * half]), unroll=4)
        h_sc[:, 0:half] = hr
        h_sc[:, half:2 * half] = hi
        y_ref[...] = _dot(bu_sc[...].astype(BF16), c_ref[...])

    scan(uf_ref, bf_ref, cf_ref, yf_ref, buf_sc, hf_sc, 0, False)
    scan(ub_ref, bb_ref, cb_ref, yb_ref, bub_sc, hb_sc, 1, True)


def _s5_scan(u_tm, bmat, avec, cmat, *, nb, s_len, l_len):
    rows = u_tm.shape[0]
    tc = S5_TC
    cr = tc * nb
    nchunk = rows // cr
    nctx = l_len // tc
    half = S5_GROUPS * S5_STATE

    def bwd(j):
        return jnp.where(j < nctx, nctx - 1 - j, nchunk - 1 - (j - nctx))

    return pl.pallas_call(
        functools.partial(_s5_kernel, nb=nb),
        grid=(nchunk,),
        in_specs=[pl.BlockSpec((cr, 256), lambda j: (j, 0)),
                  pl.BlockSpec((cr, 256), lambda j: (bwd(j), 0)),
                  pl.BlockSpec((256, 2 * half), lambda j: (0, 0)),
                  pl.BlockSpec((256, 2 * half), lambda j: (0, 0)),
                  pl.BlockSpec((2, 2, half), lambda j: (0, 0, 0)),
                  pl.BlockSpec((2 * half, 256), lambda j: (0, 0)),
                  pl.BlockSpec((2 * half, 256), lambda j: (0, 0))],
        out_specs=[pl.BlockSpec((cr, 256), lambda j: (j, 0)),
                   pl.BlockSpec((cr, 256), lambda j: (bwd(j), 0))],
        out_shape=[jax.ShapeDtypeStruct((rows, 256), F32)] * 2,
        scratch_shapes=[pltpu.VMEM((cr, 2 * half), F32), pltpu.VMEM((cr, 2 * half), F32),
                        pltpu.VMEM((nb, 2 * half), F32), pltpu.VMEM((nb, 2 * half), F32)],
        compiler_params=_cparams(("arbitrary",)),
        name="s5_scan",
    )(u_tm, u_tm, bmat[0], bmat[1], avec, cmat[0], cmat[1])


def _s5_finish_kernel(yf_ref, yb_ref, u_ref, d_ref, w_ref, o_ref):
    y = yf_ref[...] + yb_ref[...] + d_ref[...] * u_ref[...]
    r = _dot(y.astype(BF16), w_ref[...])
    o_ref[...] = r[:, :BRANCH_W] * jax.nn.sigmoid(r[:, BRANCH_W:])


def _s5_finish(yf, yb, u_tm, d_skip, w_glu):
    rows = u_tm.shape[0]
    spec = pl.BlockSpec((TM, 256), lambda i: (i, 0))
    return pl.pallas_call(
        _s5_finish_kernel,
        grid=(rows // TM,),
        in_specs=[spec, spec, spec, pl.BlockSpec((1, 256), lambda i: (0, 0)),
                  pl.BlockSpec((256, 512), lambda i: (0, 0))],
        out_specs=spec,
        out_shape=jax.ShapeDtypeStruct((rows, 256), F32),
        compiler_params=_cparams(("parallel",)),
        name="s5_glu",
    )(yf, yb, u_tm, d_skip, w_glu)


def _s5_params(lam_re, lam_im, log_dt, b_re, b_im, c_re, c_im):
    lam = lax.complex(lam_re, lam_im)
    dt = jnp.exp(log_dt)[..., None]
    a_bar = jnp.exp(lam * dt)
    b_bar = ((a_bar - 1.0) / lam)[..., None] * lax.complex(b_re, b_im)
    eye = jnp.eye(S5_GROUPS, dtype=F32)
    g, p, ch = S5_GROUPS, S5_STATE, S5_GROUP_CH

    def bdiag_in(m):
        return jnp.einsum('gh,dgpc->dgchp', eye, m).reshape(2, g * ch, g * p)

    def bdiag_out(m):
        return jnp.einsum('gh,dgcp->dgphc', eye, m).reshape(2, g * p, g * ch)

    bmat = jnp.concatenate([bdiag_in(jnp.real(b_bar)), bdiag_in(jnp.imag(b_bar))], axis=2).astype(BF16)
    cmat = jnp.concatenate([bdiag_out(c_re), -bdiag_out(c_im)], axis=1).astype(BF16)
    avec = jnp.stack([jnp.real(a_bar).reshape(2, g * p), jnp.imag(a_bar).reshape(2, g * p)], axis=1)
    return bmat, avec, cmat


def _merge_kernel(*refs, n_lat_tiles, has_ctx):
    x_ref, mod_ref = refs[0], refs[1]
    n_br = 8 if has_ctx else 4
    br_refs = refs[2:2 + n_br]
    wg_ref, bg_ref, wb_ref, wo_ref, g_ref, b_ref, x1_ref, h2_ref = refs[2 + n_br:]
    d = D_MODEL
    m = mod_ref[0]
    x = x_ref[...]
    hb = (x * (1.0 + m[1:2]) + m[0:1]).astype(BF16)
    is_ctx = pl.program_id(0) >= n_lat_tiles
    acc = None
    for i in range(4):
        if has_ctx:
            o = jnp.where(is_ctx, br_refs[2 * i + 1][...], br_refs[2 * i][...])
        else:
            o = br_refs[i][...]
        gate = jax.nn.sigmoid(_dot(hb, wg_ref[:, i * d:(i + 1) * d]) + bg_ref[:, i * d:(i + 1) * d])
        term = gate * _dot(o.astype(BF16), wb_ref[i])
        acc = term if acc is None else acc + term
    y = _dot(acc.astype(BF16), wo_ref[...])
    x1 = _layer_norm(ALPHA * x + m[2:3] * y, g_ref[...], b_ref[...])
    x1_ref[...] = x1
    h2_ref[...] = x1 * (1.0 + m[4:5]) + m[3:4]


def _merge(xa, modl, outs_lat, outs_ctx, wg, bg, wb, wo, g, b, seg, n, n_lat_tiles):
    d = D_MODEL
    tok = pl.BlockSpec((TM, d), lambda i: (i, 0))
    lat = pl.BlockSpec((TM, 256), lambda i: (jnp.minimum(i, n_lat_tiles - 1), 0))
    ctx = pl.BlockSpec((TM, 256), lambda i: (jnp.maximum(i - n_lat_tiles, 0), 0))
    const = lambda a: pl.BlockSpec(a.shape, lambda i: (0,) * a.ndim)
    has_ctx = outs_ctx is not None
    if has_ctx:
        branches = [a for pair in zip(outs_lat, outs_ctx) for a in pair]
        br_specs = [lat, ctx] * 4
    else:
        branches, br_specs = list(outs_lat), [lat] * 4
    return pl.pallas_call(
        functools.partial(_merge_kernel, n_lat_tiles=n_lat_tiles, has_ctx=has_ctx),
        grid=(n // TM,),
        in_specs=[tok, pl.BlockSpec((1, 6, d), lambda i: (seg(i), 0, 0))] + br_specs
                 + [const(wg), const(bg), const(wb), const(wo), const(g), const(b)],
        out_specs=[tok, tok],
        out_shape=[jax.ShapeDtypeStruct((n, d), F32)] * 2,
        compiler_params=_cparams(("parallel",), 56),
        name="merge_ln1",
    )(xa, modl, *branches, wg, bg, wb, wo, g, b)


def _start_row_gather(src_hbm, dst, idx_ref, base, n_rows, sem):
    def body(r, c):
        pltpu.make_async_copy(src_hbm.at[pl.ds(idx_ref[base + r], 1)], dst.at[pl.ds(r, 1)], sem).start()
        return c

    lax.fori_loop(0, n_rows, body, 0, unroll=8)


def _wait_row_gather(src_hbm, dst, sem):
    pltpu.make_async_copy(src_hbm.at[pl.ds(0, dst.shape[0])], dst, sem).wait()


def _swiglu_block(xb, w1_ref, w3_ref, w2_ref, o_ref, between=None):
    n_chunks = D_FF // FF_CHUNK
    for c in range(n_chunks):
        cs = slice(c * FF_CHUNK, (c + 1) * FF_CHUNK)
        a = _dot(xb, w1_ref[0, :, cs])
        b = _dot(xb, w3_ref[0, :, cs])
        g = (a * jax.nn.sigmoid(a) * b).astype(BF16)
        y = _dot(g, w2_ref[0, cs, :])
        if c == 0:
            o_ref[...] = y
        else:
            o_ref[...] += y
        if between is not None:
            between(c, n_chunks)


def _ffn_kernel(be_ref, x_ref, w1_ref, w3_ref, w2_ref, o_ref):
    _swiglu_block(x_ref[...].astype(BF16), w1_ref, w3_ref, w2_ref, o_ref)


def _moe_ffn_kernel(be_ref, tok_ref, nused_ref, h_hbm, w1_ref, w3_ref, w2_ref, o_ref, xbuf, sem):
    i = pl.program_id(0)
    last = pl.num_programs(0) - 1
    slot = i % 2
    nxt_base = jnp.minimum(i + 1, last) * MOE_BLK
    nxt_buf, nxt_sem = xbuf.at[1 - slot], sem.at[1 - slot]

    @pl.when(i == 0)
    def _():
        _start_row_gather(h_hbm, xbuf.at[0], tok_ref, 0, MOE_BLK, sem.at[0])

    _wait_row_gather(h_hbm, xbuf.at[slot], sem.at[slot])

    def start_piece(c, n_chunks):
        per = -(-MOE_BLK // n_chunks)
        for r in range(c * per, min((c + 1) * per, MOE_BLK)):
            pltpu.make_async_copy(h_hbm.at[pl.ds(tok_ref[nxt_base + r], 1)], nxt_buf.at[pl.ds(r, 1)],
                                  nxt_sem).start()

    @pl.when(i < nused_ref[0])
    def _():
        _swiglu_block(xbuf[slot].astype(BF16), w1_ref, w3_ref, w2_ref, o_ref, between=start_piece)

    @pl.when(i >= nused_ref[0])
    def _():
        o_ref[...] = jnp.zeros(o_ref.shape, F32)
        _start_row_gather(h_hbm, nxt_buf, tok_ref, nxt_base, MOE_BLK, nxt_sem)

    @pl.when(i == last)
    def _():
        _wait_row_gather(h_hbm, nxt_buf, nxt_sem)


def _moe_ffn(h, block_e, buf_tok, n_used, w1, w3, w2):
    d = h.shape[1]
    n_rows = buf_tok.shape[0]
    wspec = lambda shape: pl.BlockSpec(shape, lambda i, be, tok, nu: (be[i], 0, 0), pipeline_mode=pl.Buffered(1))
    grid_spec = pltpu.PrefetchScalarGridSpec(
        num_scalar_prefetch=3,
        grid=(n_rows // MOE_BLK,),
        in_specs=[pl.BlockSpec(memory_space=pl.ANY), wspec((1, d, D_FF)), wspec((1, d, D_FF)), wspec((1, D_FF, d))],
        out_specs=pl.BlockSpec((MOE_BLK, d), lambda i, be, tok, nu: (i, 0)),
        scratch_shapes=[pltpu.VMEM((2, MOE_BLK, d), F32), pltpu.SemaphoreType.DMA((2,))],
    )
    return pl.pallas_call(
        _moe_ffn_kernel,
        grid_spec=grid_spec,
        out_shape=jax.ShapeDtypeStruct((n_rows, d), F32),
        compiler_params=_cparams(("arbitrary",), 56),
        name="moe_swiglu",
    )(block_e, buf_tok, n_used, h, w1, w3, w2)


def _ffn(xb, block_e, w1, w3, w2):
    n, d = xb.shape
    blk = MOE_BLK
    grid_spec = pltpu.PrefetchScalarGridSpec(
        num_scalar_prefetch=1,
        grid=(n // blk,),
        in_specs=[pl.BlockSpec((blk, d), lambda i, be: (i, 0)),
                  pl.BlockSpec((1, d, D_FF), lambda i, be: (be[i], 0, 0), pipeline_mode=pl.Buffered(1)),
                  pl.BlockSpec((1, d, D_FF), lambda i, be: (be[i], 0, 0), pipeline_mode=pl.Buffered(1)),
                  pl.BlockSpec((1, D_FF, d), lambda i, be: (be[i], 0, 0), pipeline_mode=pl.Buffered(1))],
        out_specs=pl.BlockSpec((blk, d), lambda i, be: (i, 0)),
    )
    return pl.pallas_call(
        _ffn_kernel,
        grid_spec=grid_spec,
        out_shape=jax.ShapeDtypeStruct((n, d), F32),
        compiler_params=_cparams(("arbitrary",), 56),
        name="swiglu",
    )(block_e, xb, w1, w3, w2)


def _router_kernel(h_ref, w_ref, o_ref):
    logits = jnp.dot(h_ref[...], w_ref[...], preferred_element_type=F32, precision=lax.Precision.HIGHEST)
    lane = lax.broadcasted_iota(jnp.int32, logits.shape, 1)
    lg = jnp.where(lane < N_EXPERTS, logits, NEG)
    v1 = jnp.max(lg, axis=-1, keepdims=True)
    i1 = jnp.min(jnp.where(lg == v1, lane, LANES), axis=-1, keepdims=True)
    lg2 = jnp.where(lane == i1, NEG, lg)
    v2 = jnp.max(lg2, axis=-1, keepdims=True)
    i2 = jnp.min(jnp.where(lg2 == v2, lane, LANES), axis=-1, keepdims=True)
    e = jnp.exp(v2 - v1)
    g1 = 1.0 / (1.0 + e)
    g2 = e / (1.0 + e)
    out = jnp.where(lane == 0, i1.astype(F32), jnp.where(lane == 1, i2.astype(F32),
                    jnp.where(lane == 2, g1, jnp.where(lane == 3, g2, 0.0))))
    o_ref[...] = out


def _router(h2, w_router_pad):
    n, d = h2.shape
    return pl.pallas_call(
        _router_kernel,
        grid=(n // TM,),
        in_specs=[pl.BlockSpec((TM, d), lambda i: (i, 0)), pl.BlockSpec((d, LANES), lambda i: (0, 0))],
        out_specs=pl.BlockSpec((TM, LANES), lambda i: (i, 0)),
        out_shape=jax.ShapeDtypeStruct((n, LANES), F32),
        compiler_params=_cparams(("parallel",)),
        name="router",
    )(h2, w_router_pad)


def _combine_kernel(x_ref, mod_ref, f_ref, g_ref, b_ref, o_ref):
    m = mod_ref[0]
    o_ref[...] = _layer_norm(ALPHA * x_ref[...] + m[5:6] * f_ref[...], g_ref[...], b_ref[...])


def _combine(x1, modl, f, g, b, seg):
    n, d = x1.shape
    tok = pl.BlockSpec((TM, d), lambda i: (i, 0))
    vec = pl.BlockSpec((1, d), lambda i: (0, 0))
    return pl.pallas_call(
        _combine_kernel,
        grid=(n // TM,),
        in_specs=[tok, pl.BlockSpec((1, 6, d), lambda i: (seg(i), 0, 0)), tok, vec, vec],
        out_specs=tok,
        out_shape=jax.ShapeDtypeStruct((n, d), F32),
        compiler_params=_cparams(("parallel",)),
        name="combine_ln2",
    )(x1, modl, f, g, b)


def _moe_combine_kernel(dest_ref, x_ref, mod_ref, gt_ref, y_hbm, g_ref, b_ref, o_ref, ybuf, sem):
    i = pl.program_id(0)
    slot = i % 2

    @pl.when(i == 0)
    def _():
        _start_row_gather(y_hbm, ybuf.at[0], dest_ref, 0, 2 * TM, sem.at[0])

    @pl.when(i + 1 < pl.num_programs(0))
    def _():
        _start_row_gather(y_hbm, ybuf.at[1 - slot], dest_ref, (i + 1) * 2 * TM, 2 * TM, sem.at[1 - slot])

    _wait_row_gather(y_hbm, ybuf.at[slot], sem.at[slot])
    gt = gt_ref[...]
    f = gt[:, 2:3] * ybuf[slot, 0:TM, :] + gt[:, 3:4] * ybuf[slot, TM:2 * TM, :]
    m = mod_ref[0]
    o_ref[...] = _layer_norm(ALPHA * x_ref[...] + m[5:6] * f, g_ref[...], b_ref[...])


def _moe_combine(x1, modl, y_rows, dest_tiles, gates, g, b, seg):
    n, d = x1.shape
    tok = pl.BlockSpec((TM, d), lambda i, dst: (i, 0))
    vec = pl.BlockSpec((1, d), lambda i, dst: (0, 0))
    grid_spec = pltpu.PrefetchScalarGridSpec(
        num_scalar_prefetch=1,
        grid=(n // TM,),
        in_specs=[tok, pl.BlockSpec((1, 6, d), lambda i, dst: (seg(i), 0, 0)),
                  pl.BlockSpec((TM, LANES), lambda i, dst: (i, 0)), pl.BlockSpec(memory_space=pl.ANY), vec, vec],
        out_specs=tok,
        scratch_shapes=[pltpu.VMEM((2, 2 * TM, d), F32), pltpu.SemaphoreType.DMA((2,))],
    )
    return pl.pallas_call(
        _moe_combine_kernel,
        grid_spec=grid_spec,
        out_shape=jax.ShapeDtypeStruct((n, d), F32),
        compiler_params=_cparams(("arbitrary",)),
        name="moe_combine_ln2",
    )(dest_tiles, x1, modl, gates, y_rows, g, b)


def _rope_tables(s_len, width, rope_lane, extra_rows):
    t = jnp.arange(s_len)
    lane = np.arange(width)
    inv = ROPE_BASE ** (-(jnp.asarray(lane % 8, F32)) / 8.0)
    pos = jnp.where((lane & 16) == 0, (t // GRID_W)[:, None], (t % GRID_W)[:, None]).astype(F32)
    ang = pos * inv[None, :]
    cos = jnp.where(rope_lane[None, :], jnp.cos(ang), 1.0)
    sin = jnp.where(rope_lane[None, :], jnp.where((lane & 8) == 0, -jnp.sin(ang), jnp.sin(ang)), 0.0)
    cos = jnp.concatenate([cos, jnp.ones((extra_rows, width), F32)])
    sin = jnp.concatenate([sin, jnp.zeros((extra_rows, width), F32)])
    return cos, sin


def _pad_heads(w, n_heads, width):
    k = w.shape[0]
    w = w.reshape(k, n_heads, -1)
    return jnp.pad(w, ((0, 0), (0, 0), (0, width - w.shape[-1]))).reshape(k, n_heads * width)


def _route(top, n_tok):
    e_flat = top[:, 0:2].astype(jnp.int32).reshape(-1)
    n_assign = 2 * n_tok
    onehot = (e_flat[:, None] == jnp.arange(N_EXPERTS)[None, :]).astype(jnp.int32)
    csum = jnp.cumsum(onehot, axis=0)
    rank = jnp.take_along_axis(csum, e_flat[:, None], axis=1)[:, 0] - 1
    counts = csum[-1]
    padded = (counts + MOE_BLK - 1) // MOE_BLK * MOE_BLK
    pad_end = jnp.cumsum(padded)
    pad_start = pad_end - padded
    dest = pad_start[e_flat] + rank
    n_rows = -(-n_assign // MOE_BLK) * MOE_BLK + N_EXPERTS * MOE_BLK
    buf_tok = jnp.zeros((n_rows,), jnp.int32).at[dest].set(jnp.arange(n_assign, dtype=jnp.int32) // 2)
    block_e = jnp.minimum(jnp.searchsorted(pad_end, jnp.arange(n_rows // MOE_BLK) * MOE_BLK, side='right'),
                          N_EXPERTS - 1).astype(jnp.int32)
    n_used = (pad_end[-1:] // MOE_BLK).astype(jnp.int32)
    dest_tiles = dest.astype(jnp.int32).reshape(n_tok // TM, TM, 2).transpose(0, 2, 1).reshape(-1)
    return buf_tok, block_e, n_used, dest_tiles


def kernel(x, c, ctx, c_ctx, w_ada, b_ada, w_in, na_rpb, mla_q_norm, mla_kv_norm, mla_w_uq, mla_w_ukv,
           s5_lam_re, s5_lam_im, s5_log_dt, s5_b_re, s5_b_im, s5_c_re, s5_c_im, s5_d, s5_w_glu,
           diff_lam_q1, diff_lam_k1, diff_lam_q2, diff_lam_k2, diff_subln,
           w_branch, w_gate, b_gate, w_out, ln1_g, ln1_b, ln2_g, ln2_b,
           ffn_w1, ffn_w3, ffn_w2, moe_router, moe_w1, moe_w3, moe_w2):
    nb, s_len, d = x.shape
    l_len = ctx.shape[1]
    n_lat, n_ctx = nb * s_len, nb * l_len
    n_tot = n_lat + n_ctx
    rows = s_len // GRID_W
    assert d == D_MODEL and s_len % TM == 0 and n_ctx % TM == 0 and rows % NA_QROWS == 0 and rows >= NA_KROWS
    assert l_len % S5_TC == 0 and s_len % S5_TC == 0

    def seg(i):
        return jnp.minimum((i * TM) // s_len, nb)

    n_lat_tiles = n_lat // TM

    def tab_idx(i):
        return jnp.where(i < n_lat_tiles, i % (s_len // TM), s_len // TM)

    xa = jnp.concatenate([x.reshape(n_lat, d), ctx.reshape(n_ctx, d)], axis=0)
    nrow_mod = -(-(nb + 1) // 16) * 16
    cvec = jnp.zeros((nrow_mod, d), F32).at[:nb].set(c).at[nb].set(c_ctx)
    mod_all = _ada_all(cvec, w_ada, b_ada).reshape(DEPTH, nrow_mod, 6, d)

    lane512 = np.arange(512)
    q_rope = (lane512 % LANES >= MLA_NOPE) & (lane512 % LANES < MLA_NOPE + MLA_ROPE)
    cq, sq = _rope_tables(s_len, 512, q_rope, TM)
    ck, sk = _rope_tables(s_len, LANES, np.arange(LANES) < MLA_ROPE, TM)
    cd, sd = _rope_tables(s_len, 256, np.ones((256,), bool), TM)
    tabs = (cq, sq, ck, sk, cd, sd)
    place = np.zeros((LANES, MLA_HEADS * LANES), np.float32)
    for h in range(MLA_HEADS):
        place[np.arange(MLA_ROPE), h * LANES + MLA_NOPE + np.arange(MLA_ROPE)] = 1.0
    place = jnp.asarray(place, BF16)

    for layer in range(DEPTH):
        ctx_out = layer < DEPTH - 1
        lam_init = 0.8 - 0.6 * math.exp(-0.3 * layer)
        modl = mod_all[layer]

        wi = w_in[layer]
        w_in_pad = jnp.concatenate([wi[:, :1184], jnp.zeros((d, 96), F32), wi[:, 1184:]], axis=1).astype(BF16)
        wuq = _pad_heads(mla_w_uq[layer], MLA_HEADS, LANES).astype(BF16)
        wukv4 = mla_w_ukv[layer].reshape(-1, MLA_HEADS, MLA_NOPE + MLA_V)
        wuk = jnp.pad(wukv4[:, :, :MLA_NOPE], ((0, 0), (0, 0), (0, LANES - MLA_NOPE))).reshape(-1, MLA_HEADS * LANES)
        wuv = wukv4[:, :, MLA_NOPE:].reshape(-1, MLA_HEADS * MLA_V)
        wukv = jnp.concatenate([wuk, wuv], axis=1).astype(BF16)
        bias = _na_bias_tables(na_rpb[layer], rows)
        bmat, avec, cmat = _s5_params(s5_lam_re[layer], s5_lam_im[layer], s5_log_dt[layer], s5_b_re[layer],
                                      s5_b_im[layer], s5_c_re[layer], s5_c_im[layer])
        lam = (jnp.exp(jnp.sum(diff_lam_q1[layer] * diff_lam_k1[layer]))
               - jnp.exp(jnp.sum(diff_lam_q2[layer] * diff_lam_k2[layer])) + lam_init)
        lam_vec = jnp.full((1, LANES), lam, F32)
        subln = jnp.tile(diff_subln[layer], DIFF_HEADS).reshape(1, 256)

        z = _inproj(xa, modl, w_in_pad, seg)
        qm, km, vm, qd, kd, vd, qna = _prep(z, tabs, mla_q_norm[layer].reshape(1, -1),
                                            mla_kv_norm[layer].reshape(1, -1), wuq, wukv, place, tab_idx)
        common = dict(nb=nb, s_len=s_len, l_len=l_len)
        o_na = _na_attention(qna, z, bias, **common)
        o_mla = _flash(qm, km, vm, heads=MLA_HEADSPEC, kw=512, name="mla_attn", **common)
        o_diff = _flash(qd, kd, vd, heads=DIFF_HEADSPEC, kw=256, diff=True, lam=lam_vec,
                        subln=subln, lam_init=lam_init, name="diff_attn", **common)
        u = z[:, Z_S5:Z_S5 + 256]
        u_tm = jnp.concatenate([u[n_lat:].reshape(nb, l_len, 256).transpose(1, 0, 2).reshape(n_ctx, 256),
                                u[:n_lat].reshape(nb, s_len, 256).transpose(1, 0, 2).reshape(n_lat, 256)], axis=0)
        yf, yb = _s5_scan(u_tm, bmat, avec, cmat, **common)
        o_s5_tm = _s5_finish(yf, yb, u_tm, s5_d[layer].reshape(1, 256), s5_w_glu[layer].astype(BF16))
        o_s5_lat = o_s5_tm[n_ctx:].reshape(s_len, nb, 256).transpose(1, 0, 2).reshape(n_lat, 256)
        if ctx_out:
            o_na_c = _flash(qna, z, z, heads=NA_HEADSPEC, kcol=Z_NAK // 256, vcol=Z_NAV // 256,
                            kw=256, with_lat=False, name="na_ctx_attn", **common)
            o_mla_c = _flash(qm, km, vm, heads=MLA_HEADSPEC, kw=512, with_lat=False,
                             name="mla_ctx_attn", **common)
            o_diff_c = _flash(qd, kd, vd, heads=DIFF_HEADSPEC, kw=256, diff=True, lam=lam_vec,
                              subln=subln, lam_init=lam_init, with_lat=False, name="diff_ctx_attn", **common)
            o_s5_c = o_s5_tm[:n_ctx].reshape(l_len, nb, 256).transpose(1, 0, 2).reshape(n_ctx, 256)
            outs_ctx = [o_na_c, o_mla_c, o_s5_c, o_diff_c]
            n_act = n_tot
        else:
            outs_ctx = None
            n_act = n_lat
        x1, h2 = _merge(xa, modl, [o_na, o_mla, o_s5_lat, o_diff], outs_ctx,
                        w_gate[layer].astype(BF16), b_gate[layer].reshape(1, -1),
                        w_branch[layer].astype(BF16), w_out[layer].astype(BF16),
                        ln1_g[layer].reshape(1, d), ln1_b[layer].reshape(1, d), seg, n_act, n_lat_tiles)

        jj = layer // 2
        g2, b2 = ln2_g[layer].reshape(1, d), ln2_b[layer].reshape(1, d)
        if layer % 2 == 0:
            be = jnp.zeros((n_act // MOE_BLK,), jnp.int32)
            f = _ffn(h2, be, ffn_w1[jj:jj + 1].astype(BF16), ffn_w3[jj:jj + 1].astype(BF16),
                     ffn_w2[jj:jj + 1].astype(BF16))
            xa = _combine(x1, modl, f, g2, b2, seg)
        else:
            wr = jnp.pad(moe_router[jj], ((0, 0), (0, LANES - N_EXPERTS)))
            top = _router(h2, wr)
            buf_tok, block_e, n_used, dest_tiles = _route(top, n_act)
            y_rows = _moe_ffn(h2, block_e, buf_tok, n_used, moe_w1[jj].astype(BF16), moe_w3[jj].astype(BF16),
                              moe_w2[jj].astype(BF16))
            xa = _moe_combine(x1, modl, y_rows, dest_tiles, top, g2, b2, seg)
    return xa[:n_lat].reshape(nb, s_len, d)
```

```python
import functools
import math

import jax
import jax.numpy as jnp
import numpy as np
from jax import lax
from jax.experimental import pallas as pl
from jax.experimental.pallas import tpu as pltpu

F32 = jnp.float32
BF16 = jnp.bfloat16

D_MODEL = 1024
DEPTH = 4
GRID_W = 64
BRANCH_W = 256
NA_HEADS = 4
NA_HD = 64
NA_KH = 8
NA_KW = 16
MLA_HEADS = 4
MLA_NOPE = 64
MLA_ROPE = 32
MLA_V = 64
S5_GROUP_CH = 16
S5_GROUPS = 16
S5_STATE = 64
DIFF_HEADS = 4
DIFF_HD = 32
D_FF = 3584
N_EXPERTS = 8
ROPE_BASE = 10000.0
LN_EPS = 1e-5
RMS_EPS = 1e-6
ALPHA = (2 * DEPTH) ** 0.25
NA_SCALE = NA_HD ** -0.5
MLA_SCALE = (MLA_NOPE + MLA_ROPE) ** -0.5
DIFF_SCALE = DIFF_HD ** -0.5
LOG2E = 1.4426950408889634

LANES = 128
ZW = 2304
Z_NAQ, Z_NAK, Z_NAV, Z_MQ, Z_MKV, Z_MKR, Z_S5, Z_DQ, Z_DK, Z_DV = (
    0, 256, 512, 768, 1024, 1152, 1280, 1536, 1792, 2048)
NEG = -1e30
TM = 512
TQ = 512
TK = 256
NA_QROWS = 8
NA_KROWS = NA_QROWS + NA_KH
S5_TC = 128
FF_CHUNK = 512
MOE_BLK = 512


def _cparams(sem, vmem_mb=48):
    return pltpu.CompilerParams(dimension_semantics=sem, vmem_limit_bytes=vmem_mb << 20)


def _dot(a, b):
    return jnp.dot(a, b, preferred_element_type=F32)


def _dot_nt(a, b):
    return lax.dot_general(a, b, (((1,), (1,)), ((), ())), preferred_element_type=F32)


def _layer_norm(r, g, b):
    rc = r - jnp.mean(r, axis=-1, keepdims=True)
    var = jnp.mean(rc * rc, axis=-1, keepdims=True)
    return rc * lax.rsqrt(var + LN_EPS) * g + b


def _ada_kernel(c_ref, w_ref, b_ref, o_ref):
    c = c_ref[...]
    cond = c * jax.nn.sigmoid(c)
    o_ref[0] = _dot(cond.astype(BF16), w_ref[0].astype(BF16)) + b_ref[0]


def _ada_all(cvec, w_ada, b_ada):
    nrow = cvec.shape[0]
    d = D_MODEL
    return pl.pallas_call(
        _ada_kernel,
        grid=(DEPTH, 6),
        in_specs=[pl.BlockSpec((nrow, d), lambda l, n: (0, 0)),
                  pl.BlockSpec((1, d, d), lambda l, n: (l, 0, n)),
                  pl.BlockSpec((1, 1, d), lambda l, n: (l, 0, n))],
        out_specs=pl.BlockSpec((1, nrow, d), lambda l, n: (l, 0, n)),
        out_shape=jax.ShapeDtypeStruct((DEPTH, nrow, 6 * d), F32),
        compiler_params=_cparams(("parallel", "parallel")),
        name="ada_mod",
    )(cvec, w_ada, b_ada.reshape(DEPTH, 1, 6 * d))


def _inproj_kernel(x_ref, mod_ref, w_ref, o_ref):
    m = mod_ref[0]
    h = x_ref[...] * (1.0 + m[1:2]) + m[0:1]
    o_ref[...] = _dot(h.astype(BF16), w_ref[...])


def _inproj(xa, modl, w_in_pad, seg):
    n, d = xa.shape
    return pl.pallas_call(
        _inproj_kernel,
        grid=(n // TM,),
        in_specs=[pl.BlockSpec((TM, d), lambda i: (i, 0)),
                  pl.BlockSpec((1, 6, d), lambda i: (seg(i), 0, 0)),
                  pl.BlockSpec((d, ZW), lambda i: (0, 0))],
        out_specs=pl.BlockSpec((TM, ZW), lambda i: (i, 0)),
        out_shape=jax.ShapeDtypeStruct((n, ZW), F32),
        compiler_params=_cparams(("parallel",)),
        name="in_proj",
    )(xa, modl, w_in_pad)


def _swap8(z):
    w = z.shape[-1]
    lane = lax.broadcasted_iota(jnp.int32, z.shape, 1)
    up = pltpu.roll(z, w - 8, 1)
    dn = pltpu.roll(z, 8, 1)
    return jnp.where((lane & 8) == 0, up, dn)


def _rms(z, g):
    return z * lax.rsqrt(jnp.mean(z * z, axis=-1, keepdims=True) + RMS_EPS) * g


def _prep_kernel(naq_ref, mq_ref, mkv_ref, mkr_ref, dq_ref, dk_ref, dv_ref,
                 cq_ref, sq_ref, ck_ref, sk_ref, cd_ref, sd_ref,
                 qn_ref, kvn_ref, wuq_ref, wukv_ref, place_ref,
                 qm_ref, km_ref, vm_ref, qd_ref, kd_ref, vd_ref, qna_ref):
    aq = _rms(mq_ref[...], qn_ref[...])
    q = _dot(aq.astype(BF16), wuq_ref[...])
    q = q * cq_ref[...] + _swap8(q) * sq_ref[...]
    qm_ref[...] = (q * (MLA_SCALE * LOG2E)).astype(BF16)
    akv = _rms(mkv_ref[...], kvn_ref[...])
    kv = _dot(akv.astype(BF16), wukv_ref[...])
    kr = mkr_ref[...]
    kr = kr * ck_ref[...] + _swap8(kr) * sk_ref[...]
    km_ref[...] = (kv[:, :4 * LANES] + _dot(kr.astype(BF16), place_ref[...])).astype(BF16)
    vm_ref[...] = kv[:, 4 * LANES:].astype(BF16)
    dq = dq_ref[...]
    dq = (dq * cd_ref[...] + _swap8(dq) * sd_ref[...]) * (DIFF_SCALE * LOG2E)
    dk = dk_ref[...]
    dk = dk * cd_ref[...] + _swap8(dk) * sd_ref[...]
    kd_ref[...] = dk.astype(BF16)
    vd_ref[...] = dv_ref[...].astype(BF16)
    lane = lax.broadcasted_iota(jnp.int32, (dq.shape[0], LANES), 1)
    for g in range(2 * DIFF_HEADS):
        blk = dq[:, (g // 4) * LANES:(g // 4 + 1) * LANES]
        qd_ref[:, g * LANES:(g + 1) * LANES] = jnp.where((lane // DIFF_HD) == (g % 4), blk, 0.0).astype(BF16)
    naq = naq_ref[...] * (NA_SCALE * LOG2E)
    for h in range(NA_HEADS):
        blk = naq[:, (h // 2) * LANES:(h // 2 + 1) * LANES]
        qna_ref[:, h * LANES:(h + 1) * LANES] = jnp.where((lane // NA_HD) == (h % 2), blk, 0.0).astype(BF16)


def _prep(z, tabs, qn, kvn, wuq, wukv, place, tab_idx):
    n = z.shape[0]
    cq, sq, ck, sk, cd, sd = tabs

    def zspec(width, col):
        return pl.BlockSpec((TM, width), lambda i: (i, col))

    def tspec(width):
        return pl.BlockSpec((TM, width), lambda i: (tab_idx(i), 0))

    def wspec(a):
        return pl.BlockSpec(a.shape, lambda i: (0,) * a.ndim)

    def ospec(width):
        return pl.BlockSpec((TM, width), lambda i: (i, 0))

    outs = [(512, "qm"), (512, "km"), (256, "vm"), (1024, "qd"), (256, "kd"), (256, "vd"), (512, "qna")]
    return pl.pallas_call(
        _prep_kernel,
        grid=(n // TM,),
        in_specs=[zspec(256, Z_NAQ // 256), zspec(256, Z_MQ // 256), zspec(128, Z_MKV // 128),
                  zspec(128, Z_MKR // 128), zspec(256, Z_DQ // 256), zspec(256, Z_DK // 256),
                  zspec(256, Z_DV // 256),
                  tspec(512), tspec(512), tspec(128), tspec(128), tspec(256), tspec(256),
                  wspec(qn), wspec(kvn), wspec(wuq), wspec(wukv), wspec(place)],
        out_specs=[ospec(w) for w, _ in outs],
        out_shape=[jax.ShapeDtypeStruct((n, w), BF16) for w, _ in outs],
        compiler_params=_cparams(("parallel",)),
        name="attn_prep",
    )(z, z, z, z, z, z, z, cq, sq, ck, sk, cd, sd, qn, kvn, wuq, wukv, place)


def _flash_kernel(*refs, heads, n_acc, with_lat, diff, lam_init, tk):
    refs = list(refs)
    q_ref = refs.pop(0)
    if with_lat:
        kl_ref, vl_ref = refs.pop(0), refs.pop(0)
    kc_ref, vc_ref = refs.pop(0), refs.pop(0)
    if diff:
        lam_ref, sub_ref = refs.pop(0), refs.pop(0)
    o_ref, m_sc, l_sc, acc_sc = refs
    tq = q_ref.shape[0]
    lane = lax.broadcasted_iota(jnp.int32, (tq, LANES), 1)
    lo_half = lane < 64

    def tile(k_ref, v_ref, rows):
        nkb = (rows.stop - rows.start if isinstance(rows, slice) else rows.size) // LANES
        for g, (kb, vb, vh, ai) in enumerate(heads):
            q = q_ref[:, g * LANES:(g + 1) * LANES]
            k = k_ref[rows, kb * LANES:(kb + 1) * LANES].astype(BF16)
            s = _dot_nt(q, k)
            blocks = [s[:, c * LANES:(c + 1) * LANES] for c in range(nkb)]
            mx = blocks[0]
            for blk in blocks[1:]:
                mx = jnp.maximum(mx, blk)
            m_prev = m_sc[g]
            m_new = jnp.maximum(m_prev, jnp.max(mx, axis=-1, keepdims=True))
            alpha = jnp.exp2(m_prev - m_new)
            ps = [jnp.exp2(blk - m_new) for blk in blocks]
            lsum = ps[0]
            for pb in ps[1:]:
                lsum = lsum + pb
            l_sc[g] = alpha * l_sc[g] + lsum
            m_sc[g] = m_new
            p = jnp.concatenate([pb.astype(BF16) for pb in ps], axis=1)
            v = v_ref[rows, vb * LANES:(vb + 1) * LANES].astype(BF16)
            pv = _dot(p, v)
            old = acc_sc[ai, :, vb * LANES:(vb + 1) * LANES]
            mine = lo_half if vh == 0 else jnp.logical_not(lo_half)
            acc_sc[ai, :, vb * LANES:(vb + 1) * LANES] = jnp.where(mine, alpha * old + pv, old)

    m_sc[...] = jnp.full(m_sc.shape, NEG, F32)
    l_sc[...] = jnp.zeros(l_sc.shape, F32)
    acc_sc[...] = jnp.zeros(acc_sc.shape, F32)
    tile(kc_ref, vc_ref, slice(0, kc_ref.shape[0]))

    if with_lat:
        def kv_step(j, carry):
            tile(kl_ref, vl_ref, pl.ds(pl.multiple_of(j * tk, tk), tk))
            return carry

        lax.fori_loop(0, kl_ref.shape[0] // tk, kv_step, 0, unroll=4)

    def inv_l(g):
        return 1.0 / jnp.sum(l_sc[g], axis=-1, keepdims=True)

    for c in range(2):
        if not diff:
            o = jnp.where(lo_half, acc_sc[0, :, c * LANES:(c + 1) * LANES] * inv_l(2 * c),
                          acc_sc[0, :, c * LANES:(c + 1) * LANES] * inv_l(2 * c + 1))
        else:
            o1 = jnp.where(lo_half, acc_sc[0, :, c * LANES:(c + 1) * LANES] * inv_l(4 * c),
                           acc_sc[0, :, c * LANES:(c + 1) * LANES] * inv_l(4 * c + 2))
            o2 = jnp.where(lo_half, acc_sc[1, :, c * LANES:(c + 1) * LANES] * inv_l(4 * c + 1),
                           acc_sc[1, :, c * LANES:(c + 1) * LANES] * inv_l(4 * c + 3))
            o = o1 - lam_ref[...] * o2
            sq = o * o
            ms_lo = jnp.sum(jnp.where(lo_half, sq, 0.0), axis=-1, keepdims=True) * (1.0 / 64)
            ms_hi = jnp.sum(jnp.where(lo_half, 0.0, sq), axis=-1, keepdims=True) * (1.0 / 64)
            rs = jnp.where(lo_half, lax.rsqrt(ms_lo + RMS_EPS), lax.rsqrt(ms_hi + RMS_EPS))
            o = o * rs * sub_ref[:, c * LANES:(c + 1) * LANES] * (1.0 - lam_init)
        o_ref[:, c * LANES:(c + 1) * LANES] = o


def _flash(q, k, v, *, nb, s_len, l_len, heads, kcol=0, vcol=0, kw=None, vw=256,
           with_lat=True, diff=False, lam=None, subln=None, lam_init=0.0, name="flash"):
    nh = len(heads)
    n_acc = 2 if diff else 1
    ctx0 = nb * s_len // l_len
    tk = min(TK, s_len)
    if with_lat:
        tq = min(TQ, s_len)
        grid = (nb, s_len // tq)
        rows = nb * s_len
        qmap = lambda b, i: (b * (s_len // tq) + i, 0)
    else:
        tq = l_len
        grid = (nb, 1)
        rows = nb * l_len
        qmap = lambda b, i: (ctx0 + b, 0)
    in_specs = [pl.BlockSpec((tq, nh * LANES), qmap)]
    args = [q]
    if with_lat:
        in_specs += [pl.BlockSpec((s_len, kw), lambda b, i: (b, kcol)),
                     pl.BlockSpec((s_len, vw), lambda b, i: (b, vcol))]
        args += [k, v]
    in_specs += [pl.BlockSpec((l_len, kw), lambda b, i: (ctx0 + b, kcol)),
                 pl.BlockSpec((l_len, vw), lambda b, i: (ctx0 + b, vcol))]
    args += [k, v]
    if diff:
        in_specs += [pl.BlockSpec((1, LANES), lambda b, i: (0, 0)),
                     pl.BlockSpec((1, 256), lambda b, i: (0, 0))]
        args += [lam, subln]
    return pl.pallas_call(
        functools.partial(_flash_kernel, heads=heads, n_acc=n_acc, with_lat=with_lat,
                          diff=diff, lam_init=lam_init, tk=tk),
        grid=grid,
        in_specs=in_specs,
        out_specs=pl.BlockSpec((tq, 256), qmap if with_lat else (lambda b, i: (b, 0))),
        out_shape=jax.ShapeDtypeStruct((rows, 256), F32),
        scratch_shapes=[pltpu.VMEM((nh, tq, LANES), F32), pltpu.VMEM((nh, tq, LANES), F32),
                        pltpu.VMEM((n_acc, tq, 256), F32)],
        compiler_params=_cparams(("parallel", "parallel")),
        name=name,
    )(*args)


MLA_HEADSPEC = tuple((h, h // 2, h % 2, 0) for h in range(MLA_HEADS))
NA_HEADSPEC = tuple((h // 2, h // 2, h % 2, 0) for h in range(NA_HEADS))
DIFF_HEADSPEC = tuple((g // 4, g // 4, (g // 2) % 2, g % 2) for g in range(2 * DIFF_HEADS))


def _na_kernel(q_ref, k_ref, v_ref, kc_ref, vc_ref, bias_ref, o_ref, *, rows):
    rb = pl.program_id(1)
    k0 = jnp.clip(NA_QROWS * rb - NA_KH // 2, 0, rows - NA_KROWS)
    start = pl.multiple_of(k0 * GRID_W, GRID_W)
    nk = NA_KROWS * GRID_W
    kw = k_ref[pl.ds(start, nk), :].astype(BF16)
    vw = v_ref[pl.ds(start, nk), :].astype(BF16)
    kc = kc_ref[...].astype(BF16)
    vc = vc_ref[...].astype(BF16)
    tq = q_ref.shape[0]
    lane = lax.broadcasted_iota(jnp.int32, (tq, LANES), 1)
    lo_half = lane < 64
    for c in range(2):
        o_c = jnp.zeros((tq, LANES), F32)
        for half in range(2):
            h = 2 * c + half
            q = q_ref[:, h * LANES:(h + 1) * LANES]
            s_loc = _dot_nt(q, kw[:, c * LANES:(c + 1) * LANES]) + bias_ref[0, h]
            s_ctx = _dot_nt(q, kc[:, c * LANES:(c + 1) * LANES])
            m = jnp.maximum(jnp.max(s_loc, axis=-1, keepdims=True), jnp.max(s_ctx, axis=-1, keepdims=True))
            p_loc = jnp.exp2(s_loc - m)
            p_ctx = jnp.exp2(s_ctx - m)
            l = jnp.sum(p_loc, axis=-1, keepdims=True) + jnp.sum(p_ctx, axis=-1, keepdims=True)
            pv = (_dot(p_loc.astype(BF16), vw[:, c * LANES:(c + 1) * LANES])
                  + _dot(p_ctx.astype(BF16), vc[:, c * LANES:(c + 1) * LANES]))
            mine = lo_half if half == 0 else jnp.logical_not(lo_half)
            o_c = jnp.where(mine, pv * (1.0 / l), o_c)
        o_ref[:, c * LANES:(c + 1) * LANES] = o_c


def _na_attention(qna, z, bias, *, nb, s_len, l_len):
    rows = s_len // GRID_W
    tq = NA_QROWS * GRID_W
    nrb = rows // NA_QROWS
    ctx0 = nb * s_len // l_len

    def variant(rb):
        return jnp.where(rb == 0, 0, jnp.where(rb == nrb - 1, 2, 1))

    return pl.pallas_call(
        functools.partial(_na_kernel, rows=rows),
        grid=(nb, nrb),
        in_specs=[pl.BlockSpec((tq, NA_HEADS * LANES), lambda b, r: (b * nrb + r, 0)),
                  pl.BlockSpec((s_len, 256), lambda b, r: (b, Z_NAK // 256)),
                  pl.BlockSpec((s_len, 256), lambda b, r: (b, Z_NAV // 256)),
                  pl.BlockSpec((l_len, 256), lambda b, r: (ctx0 + b, Z_NAK // 256)),
                  pl.BlockSpec((l_len, 256), lambda b, r: (ctx0 + b, Z_NAV // 256)),
                  pl.BlockSpec((1, NA_HEADS, tq, NA_KROWS * GRID_W), lambda b, r: (variant(r), 0, 0, 0))],
        out_specs=pl.BlockSpec((tq, 256), lambda b, r: (b * nrb + r, 0)),
        out_shape=jax.ShapeDtypeStruct((nb * s_len, 256), F32),
        compiler_params=_cparams(("parallel", "arbitrary"), 56),
        name="na_attn",
    )(qna, z, z, z, z, bias)


def _na_bias_tables(rpb, rows):
    a = np.arange(NA_QROWS)
    qc = np.arange(GRID_W)
    kr_rel = np.arange(NA_KROWS)
    kc = np.arange(GRID_W)
    col0 = np.clip(qc - NA_KW // 2, 0, GRID_W - NA_KW)
    col_valid = (kc[None, :] >= col0[:, None]) & (kc[None, :] < col0[:, None] + NA_KW)
    col_off = np.clip(kc[None, :] - qc[:, None] + (NA_KW - 1), 0, 2 * NA_KW - 2)
    oh_col = (col_off[:, :, None] == np.arange(2 * NA_KW - 1)).astype(np.float32)
    big = 10 ** 6
    out = []
    for r_blk, k0, rows_eff in ((0, 0, big), (NA_QROWS, NA_KH // 2, big), (rows - NA_QROWS, rows - NA_KROWS, rows)):
        qr = r_blk + a
        r0 = np.clip(qr - NA_KH // 2, 0, rows_eff - NA_KH)
        kr = k0 + kr_rel
        row_valid = (kr[None, :] >= r0[:, None]) & (kr[None, :] < r0[:, None] + NA_KH)
        row_off = np.clip(kr[None, :] - qr[:, None] + (NA_KH - 1), 0, 2 * NA_KH - 2)
        oh_row = (row_off[:, :, None] == np.arange(2 * NA_KH - 1)).astype(np.float32)
        b = jnp.einsum('akr,hrc,qlc->haqkl', oh_row, rpb, oh_col, precision=lax.Precision.HIGHEST)
        valid = row_valid[:, None, :, None] & col_valid[None, :, None, :]
        b = jnp.where(valid[None], b * LOG2E, NEG)
        out.append(b.reshape(NA_HEADS, NA_QROWS * GRID_W, NA_KROWS * GRID_W))
    return jnp.stack(out)


def _s5_kernel(*refs, nb):
    uf_refs, ub_refs = refs[:nb], refs[nb:2 * nb]
    (bf_ref, bb_ref, a_ref, cf_ref, cb_ref, d_ref, yf_ref, yb_ref,
     uf_sc, ub_sc, buf_sc, bub_sc, hf_sc, hb_sc) = refs[2 * nb:]
    j = pl.program_id(0)
    tc = uf_refs[0].shape[0]
    half = S5_GROUPS * S5_STATE

    @pl.when(j == 0)
    def _():
        hf_sc[...] = jnp.zeros(hf_sc.shape, F32)
        hb_sc[...] = jnp.zeros(hb_sc.shape, F32)

    for b in range(nb):
        for hv in range(2):
            uf_sc[hv, pl.ds(b, tc, stride=nb), :] = uf_refs[b][:, hv * LANES:(hv + 1) * LANES]
            ub_sc[hv, pl.ds(b, tc, stride=nb), :] = ub_refs[b][:, hv * LANES:(hv + 1) * LANES]

    def scan(u_sc, b_ref, c_ref, y_ref, bu_sc, h_sc, d, reverse):
        u = jnp.concatenate([u_sc[0], u_sc[1]], axis=1)
        bu_sc[...] = _dot(u.astype(BF16), b_ref[...])
        ar = jnp.broadcast_to(a_ref[d, 0:1, :], (nb, half))
        ai = jnp.broadcast_to(a_ref[d, 1:2, :], (nb, half))

        def step(t, carry):
            hr, hi = carry
            tt = (tc - 1 - t) if reverse else t
            row = pl.multiple_of(tt * nb, nb)
            nr = ar * hr - ai * hi + bu_sc[pl.ds(row, nb), 0:half]
            ni = ar * hi + ai * hr + bu_sc[pl.ds(row, nb), half:2 * half]
            bu_sc[pl.ds(row, nb), 0:half] = nr
            bu_sc[pl.ds(row, nb), half:2 * half] = ni
            return nr, ni

        hr, hi = lax.fori_loop(0, tc, step, (h_sc[:, 0:half], h_sc[:, half:2 * half]), unroll=4)
        h_sc[:, 0:half] = hr
        h_sc[:, half:2 * half] = hi
        y = _dot(bu_sc[...].astype(BF16), c_ref[...])
        y_ref[...] = y if reverse else y + d_ref[...] * u

    scan(uf_sc, bf_ref, cf_ref, yf_ref, buf_sc, hf_sc, 0, False)
    scan(ub_sc, bb_ref, cb_ref, yb_ref, bub_sc, hb_sc, 1, True)


def _s5_scan(z, bmat, avec, cmat, d_skip, *, nb, s_len, l_len):
    tc = S5_TC
    cr = tc * nb
    nctx, nlat = l_len // tc, s_len // tc
    nchunk = nctx + nlat
    half = S5_GROUPS * S5_STATE
    n_lat = nb * s_len

    def bwd(j):
        return jnp.where(j < nctx, nctx - 1 - j, nchunk - 1 - (j - nctx))

    def u_spec(b, order):
        def row_block(j):
            c = order(j)
            return jnp.where(c < nctx, (n_lat + b * l_len) // tc + c, (b * s_len) // tc + c - nctx)
        return pl.BlockSpec((tc, 256), lambda j: (row_block(j), Z_S5 // 256))

    const = lambda shape: pl.BlockSpec(shape, lambda j: (0,) * len(shape))
    return pl.pallas_call(
        functools.partial(_s5_kernel, nb=nb),
        grid=(nchunk,),
        in_specs=[u_spec(b, lambda j: j) for b in range(nb)] + [u_spec(b, bwd) for b in range(nb)]
                 + [const((256, 2 * half)), const((256, 2 * half)), const((2, 2, half)),
                    const((2 * half, 256)), const((2 * half, 256)), const((1, 256))],
        out_specs=[pl.BlockSpec((cr, 256), lambda j: (j, 0)),
                   pl.BlockSpec((cr, 256), lambda j: (bwd(j), 0))],
        out_shape=[jax.ShapeDtypeStruct((nchunk * cr, 256), F32)] * 2,
        scratch_shapes=[pltpu.VMEM((2, cr, LANES), F32), pltpu.VMEM((2, cr, LANES), F32),
                        pltpu.VMEM((cr, 2 * half), F32), pltpu.VMEM((cr, 2 * half), F32),
                        pltpu.VMEM((nb, 2 * half), F32), pltpu.VMEM((nb, 2 * half), F32)],
        compiler_params=_cparams(("arbitrary",)),
        name="s5_scan",
    )(*([z] * (2 * nb)), bmat[0], bmat[1], avec, cmat[0], cmat[1], d_skip)


def _s5_finish_kernel(yf_ref, yb_ref, w_ref, o_ref, r_sc):
    nb, steps = o_ref.shape[0], o_ref.shape[1]
    r = _dot((yf_ref[...] + yb_ref[...]).astype(BF16), w_ref[...])
    o = r[:, :BRANCH_W] * jax.nn.sigmoid(r[:, BRANCH_W:])
    for hv in range(2):
        r_sc[hv] = o[:, hv * LANES:(hv + 1) * LANES]
    for b in range(nb):
        for hv in range(2):
            o_ref[b, :, hv * LANES:(hv + 1) * LANES] = r_sc[hv, pl.ds(b, steps, stride=nb), :]


def _s5_finish(yf, yb, w_glu, *, nb, t0, t_len):
    steps = TM // nb
    tile0 = t0 // steps
    spec = pl.BlockSpec((TM, 256), lambda i: (tile0 + i, 0))
    out = pl.pallas_call(
        _s5_finish_kernel,
        grid=(t_len // steps,),
        in_specs=[spec, spec, pl.BlockSpec((256, 512), lambda i: (0, 0))],
        out_specs=pl.BlockSpec((nb, steps, 256), lambda i: (0, i, 0)),
        out_shape=jax.ShapeDtypeStruct((nb, t_len, 256), F32),
        scratch_shapes=[pltpu.VMEM((2, TM, LANES), F32)],
        compiler_params=_cparams(("parallel",)),
        name="s5_glu",
    )(yf, yb, w_glu)
    return out.reshape(nb * t_len, 256)


def _s5_params(lam_re, lam_im, log_dt, b_re, b_im, c_re, c_im):
    lam = lax.complex(lam_re, lam_im)
    dt = jnp.exp(log_dt)[..., None]
    a_bar = jnp.exp(lam * dt)
    b_bar = ((a_bar - 1.0) / lam)[..., None] * lax.complex(b_re, b_im)
    eye = jnp.eye(S5_GROUPS, dtype=F32)
    g, p, ch = S5_GROUPS, S5_STATE, S5_GROUP_CH

    def bdiag_in(m):
        return jnp.einsum('gh,dgpc->dgchp', eye, m).reshape(2, g * ch, g * p)

    def bdiag_out(m):
        return jnp.einsum('gh,dgcp->dgphc', eye, m).reshape(2, g * p, g * ch)

    bmat = jnp.concatenate([bdiag_in(jnp.real(b_bar)), bdiag_in(jnp.imag(b_bar))], axis=2).astype(BF16)
    cmat = jnp.concatenate([bdiag_out(c_re), -bdiag_out(c_im)], axis=1).astype(BF16)
    avec = jnp.stack([jnp.real(a_bar).reshape(2, g * p), jnp.imag(a_bar).reshape(2, g * p)], axis=1)
    return bmat, avec, cmat


def _merge_kernel(*refs, n_lat_tiles, has_ctx):
    x_ref, mod_ref = refs[0], refs[1]
    n_br = 8 if has_ctx else 4
    br_refs = refs[2:2 + n_br]
    wg_ref, bg_ref, wb_ref, wo_ref, g_ref, b_ref, x1_ref, h2_ref = refs[2 + n_br:]
    d = D_MODEL
    m = mod_ref[0]
    x = x_ref[...]
    hb = (x * (1.0 + m[1:2]) + m[0:1]).astype(BF16)
    is_ctx = pl.program_id(0) >= n_lat_tiles
    acc = None
    for i in range(4):
        if has_ctx:
            o = jnp.where(is_ctx, br_refs[2 * i + 1][...], br_refs[2 * i][...])
        else:
            o = br_refs[i][...]
        gate = jax.nn.sigmoid(_dot(hb, wg_ref[:, i * d:(i + 1) * d]) + bg_ref[:, i * d:(i + 1) * d])
        term = gate * _dot(o.astype(BF16), wb_ref[i])
        acc = term if acc is None else acc + term
    y = _dot(acc.astype(BF16), wo_ref[...])
    x1 = _layer_norm(ALPHA * x + m[2:3] * y, g_ref[...], b_ref[...])
    x1_ref[...] = x1
    h2_ref[...] = x1 * (1.0 + m[4:5]) + m[3:4]


def _merge(xa, modl, outs_lat, outs_ctx, wg, bg, wb, wo, g, b, seg, n, n_lat_tiles):
    d = D_MODEL
    tok = pl.BlockSpec((TM, d), lambda i: (i, 0))
    lat = pl.BlockSpec((TM, 256), lambda i: (jnp.minimum(i, n_lat_tiles - 1), 0))
    ctx = pl.BlockSpec((TM, 256), lambda i: (jnp.maximum(i - n_lat_tiles, 0), 0))
    const = lambda a: pl.BlockSpec(a.shape, lambda i: (0,) * a.ndim)
    has_ctx = outs_ctx is not None
    if has_ctx:
        branches = [a for pair in zip(outs_lat, outs_ctx) for a in pair]
        br_specs = [lat, ctx] * 4
    else:
        branches, br_specs = list(outs_lat), [lat] * 4
    return pl.pallas_call(
        functools.partial(_merge_kernel, n_lat_tiles=n_lat_tiles, has_ctx=has_ctx),
        grid=(n // TM,),
        in_specs=[tok, pl.BlockSpec((1, 6, d), lambda i: (seg(i), 0, 0))] + br_specs
                 + [const(wg), const(bg), const(wb), const(wo), const(g), const(b)],
        out_specs=[tok, tok],
        out_shape=[jax.ShapeDtypeStruct((n, d), F32)] * 2,
        compiler_params=_cparams(("parallel",), 56),
        name="merge_ln1",
    )(xa, modl, *branches, wg, bg, wb, wo, g, b)


def _start_row_gather(src_hbm, dst, idx_ref, base, n_rows, sem):
    def body(r, c):
        pltpu.make_async_copy(src_hbm.at[pl.ds(idx_ref[base + r], 1)], dst.at[pl.ds(r, 1)], sem).start()
        return c

    lax.fori_loop(0, n_rows, body, 0, unroll=8)


def _wait_row_gather(src_hbm, dst, sem):
    pltpu.make_async_copy(src_hbm.at[pl.ds(0, dst.shape[0])], dst, sem).wait()


def _swiglu_block(xb, w1_ref, w3_ref, w2_ref, o_ref, between=None):
    n_chunks = D_FF // FF_CHUNK
    for c in range(n_chunks):
        cs = slice(c * FF_CHUNK, (c + 1) * FF_CHUNK)
        a = _dot(xb, w1_ref[0, :, cs])
        b = _dot(xb, w3_ref[0, :, cs])
        g = (a * jax.nn.sigmoid(a) * b).astype(BF16)
        y = _dot(g, w2_ref[0, cs, :])
        if c == 0:
            o_ref[...] = y
        else:
            o_ref[...] += y
        if between is not None:
            between(c, n_chunks)


def _ffn_kernel(be_ref, x_ref, w1_ref, w3_ref, w2_ref, o_ref):
    _swiglu_block(x_ref[...].astype(BF16), w1_ref, w3_ref, w2_ref, o_ref)


def _moe_ffn_kernel(be_ref, tok_ref, nused_ref, h_hbm, w1_ref, w3_ref, w2_ref, o_ref, xbuf, sem):
    i = pl.program_id(0)
    last = pl.num_programs(0) - 1
    slot = i % 2
    nxt_base = jnp.minimum(i + 1, last) * MOE_BLK
    nxt_buf, nxt_sem = xbuf.at[1 - slot], sem.at[1 - slot]

    @pl.when(i == 0)
    def _():
        _start_row_gather(h_hbm, xbuf.at[0], tok_ref, 0, MOE_BLK, sem.at[0])

    _wait_row_gather(h_hbm, xbuf.at[slot], sem.at[slot])

    def start_piece(c, n_chunks):
        per = -(-MOE_BLK // n_chunks)
        for r in range(c * per, min((c + 1) * per, MOE_BLK)):
            pltpu.make_async_copy(h_hbm.at[pl.ds(tok_ref[nxt_base + r], 1)], nxt_buf.at[pl.ds(r, 1)],
                                  nxt_sem).start()

    @pl.when(i < nused_ref[0])
    def _():
        _swiglu_block(xbuf[slot].astype(BF16), w1_ref, w3_ref, w2_ref, o_ref, between=start_piece)

    @pl.when(i >= nused_ref[0])
    def _():
        o_ref[...] = jnp.zeros(o_ref.shape, F32)
        _start_row_gather(h_hbm, nxt_buf, tok_ref, nxt_base, MOE_BLK, nxt_sem)

    @pl.when(i == last)
    def _():
        _wait_row_gather(h_hbm, nxt_buf, nxt_sem)


def _moe_ffn(h, block_e, buf_tok, n_used, w1, w3, w2):
    d = h.shape[1]
    n_rows = buf_tok.shape[0]
    wspec = lambda shape: pl.BlockSpec(shape, lambda i, be, tok, nu: (be[i], 0, 0), pipeline_mode=pl.Buffered(1))
    grid_spec = pltpu.PrefetchScalarGridSpec(
        num_scalar_prefetch=3,
        grid=(n_rows // MOE_BLK,),
        in_specs=[pl.BlockSpec(memory_space=pl.ANY), wspec((1, d, D_FF)), wspec((1, d, D_FF)), wspec((1, D_FF, d))],
        out_specs=pl.BlockSpec((MOE_BLK, d), lambda i, be, tok, nu: (i, 0)),
        scratch_shapes=[pltpu.VMEM((2, MOE_BLK, d), F32), pltpu.SemaphoreType.DMA((2,))],
    )
    return pl.pallas_call(
        _moe_ffn_kernel,
        grid_spec=grid_spec,
        out_shape=jax.ShapeDtypeStruct((n_rows, d), F32),
        compiler_params=_cparams(("arbitrary",), 56),
        name="moe_swiglu",
    )(block_e, buf_tok, n_used, h, w1, w3, w2)


def _ffn(xb, block_e, w1, w3, w2):
    n, d = xb.shape
    blk = MOE_BLK
    grid_spec = pltpu.PrefetchScalarGridSpec(
        num_scalar_prefetch=1,
        grid=(n // blk,),
        in_specs=[pl.BlockSpec((blk, d), lambda i, be: (i, 0)),
                  pl.BlockSpec((1, d, D_FF), lambda i, be: (be[i], 0, 0), pipeline_mode=pl.Buffered(1)),
                  pl.BlockSpec((1, d, D_FF), lambda i, be: (be[i], 0, 0), pipeline_mode=pl.Buffered(1)),
                  pl.BlockSpec((1, D_FF, d), lambda i, be: (be[i], 0, 0), pipeline_mode=pl.Buffered(1))],
        out_specs=pl.BlockSpec((blk, d), lambda i, be: (i, 0)),
    )
    return pl.pallas_call(
        _ffn_kernel,
        grid_spec=grid_spec,
        out_shape=jax.ShapeDtypeStruct((n, d), F32),
        compiler_params=_cparams(("arbitrary",), 56),
        name="swiglu",
    )(block_e, xb, w1, w3, w2)


def _router_kernel(h_ref, w_ref, o_ref):
    logits = jnp.dot(h_ref[...], w_ref[...], preferred_element_type=F32, precision=lax.Precision.HIGHEST)
    lane = lax.broadcasted_iota(jnp.int32, logits.shape, 1)
    lg = jnp.where(lane < N_EXPERTS, logits, NEG)
    v1 = jnp.max(lg, axis=-1, keepdims=True)
    i1 = jnp.min(jnp.where(lg == v1, lane, LANES), axis=-1, keepdims=True)
    lg2 = jnp.where(lane == i1, NEG, lg)
    v2 = jnp.max(lg2, axis=-1, keepdims=True)
    i2 = jnp.min(jnp.where(lg2 == v2, lane, LANES), axis=-1, keepdims=True)
    e = jnp.exp(v2 - v1)
    g1 = 1.0 / (1.0 + e)
    g2 = e / (1.0 + e)
    out = jnp.where(lane == 0, i1.astype(F32), jnp.where(lane == 1, i2.astype(F32),
                    jnp.where(lane == 2, g1, jnp.where(lane == 3, g2, 0.0))))
    o_ref[...] = out


def _router(h2, w_router_pad):
    n, d = h2.shape
    return pl.pallas_call(
        _router_kernel,
        grid=(n // TM,),
        in_specs=[pl.BlockSpec((TM, d), lambda i: (i, 0)), pl.BlockSpec((d, LANES), lambda i: (0, 0))],
        out_specs=pl.BlockSpec((TM, LANES), lambda i: (i, 0)),
        out_shape=jax.ShapeDtypeStruct((n, LANES), F32),
        compiler_params=_cparams(("parallel",)),
        name="router",
    )(h2, w_router_pad)


def _combine_kernel(x_ref, mod_ref, f_ref, g_ref, b_ref, o_ref):
    m = mod_ref[0]
    o_ref[...] = _layer_norm(ALPHA * x_ref[...] + m[5:6] * f_ref[...], g_ref[...], b_ref[...])


def _combine(x1, modl, f, g, b, seg):
    n, d = x1.shape
    tok = pl.BlockSpec((TM, d), lambda i: (i, 0))
    vec = pl.BlockSpec((1, d), lambda i: (0, 0))
    return pl.pallas_call(
        _combine_kernel,
        grid=(n // TM,),
        in_specs=[tok, pl.BlockSpec((1, 6, d), lambda i: (seg(i), 0, 0)), tok, vec, vec],
        out_specs=tok,
        out_shape=jax.ShapeDtypeStruct((n, d), F32),
        compiler_params=_cparams(("parallel",)),
        name="combine_ln2",
    )(x1, modl, f, g, b)


def _moe_combine_kernel(dest_ref, x_ref, mod_ref, gt_ref, y_hbm, g_ref, b_ref, o_ref, ybuf, sem):
    i = pl.program_id(0)
    last = pl.num_programs(0) - 1
    slot = i % 2
    nxt_base = jnp.minimum(i + 1, last) * 2 * TM
    nxt_buf, nxt_sem = ybuf.at[1 - slot], sem.at[1 - slot]

    @pl.when(i == 0)
    def _():
        _start_row_gather(y_hbm, ybuf.at[0], dest_ref, 0, 2 * TM, sem.at[0])

    for r in range(2 * TM):
        pltpu.make_async_copy(y_hbm.at[pl.ds(dest_ref[nxt_base + r], 1)], nxt_buf.at[pl.ds(r, 1)], nxt_sem).start()

    _wait_row_gather(y_hbm, ybuf.at[slot], sem.at[slot])
    gt = gt_ref[...]
    f = gt[:, 2:3] * ybuf[slot, 0:TM, :] + gt[:, 3:4] * ybuf[slot, TM:2 * TM, :]
    m = mod_ref[0]
    o_ref[...] = _layer_norm(ALPHA * x_ref[...] + m[5:6] * f, g_ref[...], b_ref[...])

    @pl.when(i == last)
    def _():
        _wait_row_gather(y_hbm, nxt_buf, nxt_sem)


def _moe_combine(x1, modl, y_rows, dest_tiles, gates, g, b, seg):
    n, d = x1.shape
    tok = pl.BlockSpec((TM, d), lambda i, dst: (i, 0))
    vec = pl.BlockSpec((1, d), lambda i, dst: (0, 0))
    grid_spec = pltpu.PrefetchScalarGridSpec(
        num_scalar_prefetch=1,
        grid=(n // TM,),
        in_specs=[tok, pl.BlockSpec((1, 6, d), lambda i, dst: (seg(i), 0, 0)),
                  pl.BlockSpec((TM, LANES), lambda i, dst: (i, 0)), pl.BlockSpec(memory_space=pl.ANY), vec, vec],
        out_specs=tok,
        scratch_shapes=[pltpu.VMEM((2, 2 * TM, d), F32), pltpu.SemaphoreType.DMA((2,))],
    )
    return pl.pallas_call(
        _moe_combine_kernel,
        grid_spec=grid_spec,
        out_shape=jax.ShapeDtypeStruct((n, d), F32),
        compiler_params=_cparams(("arbitrary",)),
        name="moe_combine_ln2",
    )(dest_tiles, x1, modl, gates, y_rows, g, b)


def _rope_tables(s_len, width, rope_lane, extra_rows):
    t = jnp.arange(s_len)
    lane = np.arange(width)
    inv = ROPE_BASE ** (-(jnp.asarray(lane % 8, F32)) / 8.0)
    pos = jnp.where((lane & 16) == 0, (t // GRID_W)[:, None], (t % GRID_W)[:, None]).astype(F32)
    ang = pos * inv[None, :]
    cos = jnp.where(rope_lane[None, :], jnp.cos(ang), 1.0)
    sin = jnp.where(rope_lane[None, :], jnp.where((lane & 8) == 0, -jnp.sin(ang), jnp.sin(ang)), 0.0)
    cos = jnp.concatenate([cos, jnp.ones((extra_rows, width), F32)])
    sin = jnp.concatenate([sin, jnp.zeros((extra_rows, width), F32)])
    return cos, sin


def _pad_heads(w, n_heads, width):
    k = w.shape[0]
    w = w.reshape(k, n_heads, -1)
    return jnp.pad(w, ((0, 0), (0, 0), (0, width - w.shape[-1]))).reshape(k, n_heads * width)


def _route(top, n_tok):
    e_flat = top[:, 0:2].astype(jnp.int32).reshape(-1)
    n_assign = 2 * n_tok
    onehot = (e_flat[:, None] == jnp.arange(N_EXPERTS)[None, :]).astype(jnp.int32)
    csum = jnp.cumsum(onehot, axis=0)
    rank = jnp.take_along_axis(csum, e_flat[:, None], axis=1)[:, 0] - 1
    counts = csum[-1]
    padded = (counts + MOE_BLK - 1) // MOE_BLK * MOE_BLK
    pad_end = jnp.cumsum(padded)
    pad_start = pad_end - padded
    dest = pad_start[e_flat] + rank
    n_rows = -(-n_assign // MOE_BLK) * MOE_BLK + N_EXPERTS * MOE_BLK
    buf_tok = jnp.zeros((n_rows,), jnp.int32).at[dest].set(jnp.arange(n_assign, dtype=jnp.int32) // 2)
    block_e = jnp.minimum(jnp.searchsorted(pad_end, jnp.arange(n_rows // MOE_BLK) * MOE_BLK, side='right'),
                          N_EXPERTS - 1).astype(jnp.int32)
    n_used = (pad_end[-1:] // MOE_BLK).astype(jnp.int32)
    dest_tiles = dest.astype(jnp.int32).reshape(n_tok // TM, TM, 2).transpose(0, 2, 1).reshape(-1)
    return buf_tok, block_e, n_used, dest_tiles


def kernel(x, c, ctx, c_ctx, w_ada, b_ada, w_in, na_rpb, mla_q_norm, mla_kv_norm, mla_w_uq, mla_w_ukv,
           s5_lam_re, s5_lam_im, s5_log_dt, s5_b_re, s5_b_im, s5_c_re, s5_c_im, s5_d, s5_w_glu,
           diff_lam_q1, diff_lam_k1, diff_lam_q2, diff_lam_k2, diff_subln,
           w_branch, w_gate, b_gate, w_out, ln1_g, ln1_b, ln2_g, ln2_b,
           ffn_w1, ffn_w3, ffn_w2, moe_router, moe_w1, moe_w3, moe_w2):
    nb, s_len, d = x.shape
    l_len = ctx.shape[1]
    n_lat, n_ctx = nb * s_len, nb * l_len
    n_tot = n_lat + n_ctx
    rows = s_len // GRID_W
    assert d == D_MODEL and s_len % TM == 0 and n_ctx % TM == 0 and rows % NA_QROWS == 0 and rows >= NA_KROWS
    assert l_len % S5_TC == 0 and s_len % S5_TC == 0

    def seg(i):
        return jnp.minimum((i * TM) // s_len, nb)

    n_lat_tiles = n_lat // TM

    def tab_idx(i):
        return jnp.where(i < n_lat_tiles, i % (s_len // TM), s_len // TM)

    xa = jnp.concatenate([x.reshape(n_lat, d), ctx.reshape(n_ctx, d)], axis=0)
    nrow_mod = -(-(nb + 1) // 16) * 16
    cvec = jnp.zeros((nrow_mod, d), F32).at[:nb].set(c).at[nb].set(c_ctx)
    mod_all = _ada_all(cvec, w_ada, b_ada).reshape(DEPTH, nrow_mod, 6, d)

    lane512 = np.arange(512)
    q_rope = (lane512 % LANES >= MLA_NOPE) & (lane512 % LANES < MLA_NOPE + MLA_ROPE)
    cq, sq = _rope_tables(s_len, 512, q_rope, TM)
    ck, sk = _rope_tables(s_len, LANES, np.arange(LANES) < MLA_ROPE, TM)
    cd, sd = _rope_tables(s_len, 256, np.ones((256,), bool), TM)
    tabs = (cq, sq, ck, sk, cd, sd)
    place = np.zeros((LANES, MLA_HEADS * LANES), np.float32)
    for h in range(MLA_HEADS):
        place[np.arange(MLA_ROPE), h * LANES + MLA_NOPE + np.arange(MLA_ROPE)] = 1.0
    place = jnp.asarray(place, BF16)

    for layer in range(DEPTH):
        ctx_out = layer < DEPTH - 1
        lam_init = 0.8 - 0.6 * math.exp(-0.3 * layer)
        modl = mod_all[layer]

        wi = w_in[layer]
        w_in_pad = jnp.concatenate([wi[:, :1184], jnp.zeros((d, 96), F32), wi[:, 1184:]], axis=1).astype(BF16)
        wuq = _pad_heads(mla_w_uq[layer], MLA_HEADS, LANES).astype(BF16)
        wukv4 = mla_w_ukv[layer].reshape(-1, MLA_HEADS, MLA_NOPE + MLA_V)
        wuk = jnp.pad(wukv4[:, :, :MLA_NOPE], ((0, 0), (0, 0), (0, LANES - MLA_NOPE))).reshape(-1, MLA_HEADS * LANES)
        wuv = wukv4[:, :, MLA_NOPE:].reshape(-1, MLA_HEADS * MLA_V)
        wukv = jnp.concatenate([wuk, wuv], axis=1).astype(BF16)
        bias = _na_bias_tables(na_rpb[layer], rows)
        bmat, avec, cmat = _s5_params(s5_lam_re[layer], s5_lam_im[layer], s5_log_dt[layer], s5_b_re[layer],
                                      s5_b_im[layer], s5_c_re[layer], s5_c_im[layer])
        lam = (jnp.exp(jnp.sum(diff_lam_q1[layer] * diff_lam_k1[layer]))
               - jnp.exp(jnp.sum(diff_lam_q2[layer] * diff_lam_k2[layer])) + lam_init)
        lam_vec = jnp.full((1, LANES), lam, F32)
        subln = jnp.tile(diff_subln[layer], DIFF_HEADS).reshape(1, 256)

        z = _inproj(xa, modl, w_in_pad, seg)
        qm, km, vm, qd, kd, vd, qna = _prep(z, tabs, mla_q_norm[layer].reshape(1, -1),
                                            mla_kv_norm[layer].reshape(1, -1), wuq, wukv, place, tab_idx)
        common = dict(nb=nb, s_len=s_len, l_len=l_len)
        o_na = _na_attention(qna, z, bias, **common)
        o_mla = _flash(qm, km, vm, heads=MLA_HEADSPEC, kw=512, name="mla_attn", **common)
        o_diff = _flash(qd, kd, vd, heads=DIFF_HEADSPEC, kw=256, diff=True, lam=lam_vec,
                        subln=subln, lam_init=lam_init, name="diff_attn", **common)
        yf, yb = _s5_scan(z, bmat, avec, cmat, s5_d[layer].reshape(1, 256), **common)
        w_glu = s5_w_glu[layer].astype(BF16)
        o_s5_lat = _s5_finish(yf, yb, w_glu, nb=nb, t0=l_len, t_len=s_len)
        if ctx_out:
            o_na_c = _flash(qna, z, z, heads=NA_HEADSPEC, kcol=Z_NAK // 256, vcol=Z_NAV // 256,
                            kw=256, with_lat=False, name="na_ctx_attn", **common)
            o_mla_c = _flash(qm, km, vm, heads=MLA_HEADSPEC, kw=512, with_lat=False,
                             name="mla_ctx_attn", **common)
            o_diff_c = _flash(qd, kd, vd, heads=DIFF_HEADSPEC, kw=256, diff=True, lam=lam_vec,
                              subln=subln, lam_init=lam_init, with_lat=False, name="diff_ctx_attn", **common)
            o_s5_c = _s5_finish(yf, yb, w_glu, nb=nb, t0=0, t_len=l_len)
            outs_ctx = [o_na_c, o_mla_c, o_s5_c, o_diff_c]
            n_act = n_tot
        else:
            outs_ctx = None
            n_act = n_lat
        x1, h2 = _merge(xa, modl, [o_na, o_mla, o_s5_lat, o_diff], outs_ctx,
                        w_gate[layer].astype(BF16), b_gate[layer].reshape(1, -1),
                        w_branch[layer].astype(BF16), w_out[layer].astype(BF16),
                        ln1_g[layer].reshape(1, d), ln1_b[layer].reshape(1, d), seg, n_act, n_lat_tiles)

        jj = layer // 2
        g2, b2 = ln2_g[layer].reshape(1, d), ln2_b[layer].reshape(1, d)
        if layer % 2 == 0:
            be = jnp.zeros((n_act // MOE_BLK,), jnp.int32)
            f = _ffn(h2, be, ffn_w1[jj:jj + 1].astype(BF16), ffn_w3[jj:jj + 1].astype(BF16),
                     ffn_w2[jj:jj + 1].astype(BF16))
            xa = _combine(x1, modl, f, g2, b2, seg)
        else:
            wr = jnp.pad(moe_router[jj], ((0, 0), (0, LANES - N_EXPERTS)))
            top = _router(h2, wr)
            buf_tok, block_e, n_used, dest_tiles = _route(top, n_act)
            y_rows = _moe_ffn(h2, block_e, buf_tok, n_used, moe_w1[jj].astype(BF16), moe_w3[jj].astype(BF16),
                              moe_w2[jj].astype(BF16))
            xa = _moe_combine(x1, modl, y_rows, dest_tiles, top, g2, b2, seg)
    return xa[:n_lat].reshape(nb, s_len, d)
```

```python
import functools
import math

import jax
import jax.numpy as jnp
import numpy as np
from jax import lax
from jax.experimental import pallas as pl
from jax.experimental.pallas import tpu as pltpu

F32 = jnp.float32
BF16 = jnp.bfloat16

D_MODEL = 1024
DEPTH = 4
GRID_W = 64
BRANCH_W = 256
NA_HEADS = 4
NA_HD = 64
NA_KH = 8
NA_KW = 16
MLA_HEADS = 4
MLA_NOPE = 64
MLA_ROPE = 32
MLA_V = 64
S5_GROUP_CH = 16
S5_GROUPS = 16
S5_STATE = 64
DIFF_HEADS = 4
DIFF_HD = 32
D_FF = 3584
N_EXPERTS = 8
ROPE_BASE = 10000.0
LN_EPS = 1e-5
RMS_EPS = 1e-6
ALPHA = (2 * DEPTH) ** 0.25
NA_SCALE = NA_HD ** -0.5
MLA_SCALE = (MLA_NOPE + MLA_ROPE) ** -0.5
DIFF_SCALE = DIFF_HD ** -0.5
LOG2E = 1.4426950408889634

LANES = 128
ZW = 2304
Z_NAQ, Z_NAK, Z_NAV, Z_MQ, Z_MKV, Z_MKR, Z_S5, Z_DQ, Z_DK, Z_DV = (
    0, 256, 512, 768, 1024, 1152, 1280, 1536, 1792, 2048)
NEG = -1e30
TM = 512
TQ = 1024
TK = 256
NA_QROWS = 8
NA_KROWS = NA_QROWS + NA_KH
S5_TC = 128
FF_CHUNK = 512
MOE_BLK = 512


def _cparams(sem, vmem_mb=48):
    return pltpu.CompilerParams(dimension_semantics=sem, vmem_limit_bytes=vmem_mb << 20)


def _dot(a, b):
    return jnp.dot(a, b, preferred_element_type=F32)


def _dot_nt(a, b):
    return lax.dot_general(a, b, (((1,), (1,)), ((), ())), preferred_element_type=F32)


def _layer_norm(r, g, b):
    rc = r - jnp.mean(r, axis=-1, keepdims=True)
    var = jnp.mean(rc * rc, axis=-1, keepdims=True)
    return rc * lax.rsqrt(var + LN_EPS) * g + b


def _ada_kernel(c_ref, w_ref, b_ref, o_ref):
    c = c_ref[...]
    cond = c * jax.nn.sigmoid(c)
    o_ref[0] = _dot(cond.astype(BF16), w_ref[0].astype(BF16)) + b_ref[0]


def _ada_all(cvec, w_ada, b_ada):
    nrow = cvec.shape[0]
    d = D_MODEL
    return pl.pallas_call(
        _ada_kernel,
        grid=(DEPTH, 6),
        in_specs=[pl.BlockSpec((nrow, d), lambda l, n: (0, 0)),
                  pl.BlockSpec((1, d, d), lambda l, n: (l, 0, n)),
                  pl.BlockSpec((1, 1, d), lambda l, n: (l, 0, n))],
        out_specs=pl.BlockSpec((1, nrow, d), lambda l, n: (l, 0, n)),
        out_shape=jax.ShapeDtypeStruct((DEPTH, nrow, 6 * d), F32),
        compiler_params=_cparams(("parallel", "parallel")),
        name="ada_mod",
    )(cvec, w_ada, b_ada.reshape(DEPTH, 1, 6 * d))


def _inproj_kernel(x_ref, mod_ref, w_ref, o_ref):
    m = mod_ref[0]
    h = x_ref[...] * (1.0 + m[1:2]) + m[0:1]
    o_ref[...] = _dot(h.astype(BF16), w_ref[...])


def _inproj(xa, modl, w_in_pad, seg):
    n, d = xa.shape
    return pl.pallas_call(
        _inproj_kernel,
        grid=(n // TM,),
        in_specs=[pl.BlockSpec((TM, d), lambda i: (i, 0)),
                  pl.BlockSpec((1, 6, d), lambda i: (seg(i), 0, 0)),
                  pl.BlockSpec((d, ZW), lambda i: (0, 0))],
        out_specs=pl.BlockSpec((TM, ZW), lambda i: (i, 0)),
        out_shape=jax.ShapeDtypeStruct((n, ZW), F32),
        compiler_params=_cparams(("parallel",)),
        name="in_proj",
    )(xa, modl, w_in_pad)


def _swap8(z):
    w = z.shape[-1]
    lane = lax.broadcasted_iota(jnp.int32, z.shape, 1)
    up = pltpu.roll(z, w - 8, 1)
    dn = pltpu.roll(z, 8, 1)
    return jnp.where((lane & 8) == 0, up, dn)


def _rms(z, g):
    return z * lax.rsqrt(jnp.mean(z * z, axis=-1, keepdims=True) + RMS_EPS) * g


def _prep_kernel(naq_ref, mq_ref, mkv_ref, mkr_ref, dq_ref, dk_ref, dv_ref,
                 cq_ref, sq_ref, ck_ref, sk_ref, cd_ref, sd_ref,
                 qn_ref, kvn_ref, wuq_ref, wukv_ref, place_ref,
                 qm_ref, km_ref, vm_ref, qd_ref, kd_ref, vd_ref, qna_ref):
    aq = _rms(mq_ref[...], qn_ref[...])
    q = _dot(aq.astype(BF16), wuq_ref[...])
    q = q * cq_ref[...] + _swap8(q) * sq_ref[...]
    qm_ref[...] = (q * (MLA_SCALE * LOG2E)).astype(BF16)
    akv = _rms(mkv_ref[...], kvn_ref[...])
    kv = _dot(akv.astype(BF16), wukv_ref[...])
    kr = mkr_ref[...]
    kr = kr * ck_ref[...] + _swap8(kr) * sk_ref[...]
    km_ref[...] = (kv[:, :4 * LANES] + _dot(kr.astype(BF16), place_ref[...])).astype(BF16)
    vm_ref[...] = kv[:, 4 * LANES:].astype(BF16)
    dq = dq_ref[...]
    dq = (dq * cd_ref[...] + _swap8(dq) * sd_ref[...]) * (DIFF_SCALE * LOG2E)
    dk = dk_ref[...]
    dk = dk * cd_ref[...] + _swap8(dk) * sd_ref[...]
    kd_ref[...] = dk.astype(BF16)
    vd_ref[...] = dv_ref[...].astype(BF16)
    lane = lax.broadcasted_iota(jnp.int32, (dq.shape[0], LANES), 1)
    for g in range(2 * DIFF_HEADS):
        blk = dq[:, (g // 4) * LANES:(g // 4 + 1) * LANES]
        qd_ref[:, g * LANES:(g + 1) * LANES] = jnp.where((lane // DIFF_HD) == (g % 4), blk, 0.0).astype(BF16)
    naq = naq_ref[...] * (NA_SCALE * LOG2E)
    for h in range(NA_HEADS):
        blk = naq[:, (h // 2) * LANES:(h // 2 + 1) * LANES]
        qna_ref[:, h * LANES:(h + 1) * LANES] = jnp.where((lane // NA_HD) == (h % 2), blk, 0.0).astype(BF16)


def _prep(z, tabs, qn, kvn, wuq, wukv, place, *, nb, s_len):
    n = z.shape[0]
    cq, sq, ck, sk, cd, sd = tabs
    tps = s_len // TM
    n_lat_tiles = nb * tps

    def tile(i):
        return jnp.where(i < n_lat_tiles, (i % nb) * tps + i // nb, i)

    def tab_idx(i):
        return jnp.where(i < n_lat_tiles, i // nb, tps)

    def zspec(width, col):
        return pl.BlockSpec((TM, width), lambda i: (tile(i), col))

    def tspec(width):
        return pl.BlockSpec((TM, width), lambda i: (tab_idx(i), 0))

    def wspec(a):
        return pl.BlockSpec(a.shape, lambda i: (0,) * a.ndim)

    def ospec(width):
        return pl.BlockSpec((TM, width), lambda i: (tile(i), 0))

    outs = [(512, "qm"), (512, "km"), (256, "vm"), (1024, "qd"), (256, "kd"), (256, "vd"), (512, "qna")]
    return pl.pallas_call(
        _prep_kernel,
        grid=(n // TM,),
        in_specs=[zspec(256, Z_NAQ // 256), zspec(256, Z_MQ // 256), zspec(128, Z_MKV // 128),
                  zspec(128, Z_MKR // 128), zspec(256, Z_DQ // 256), zspec(256, Z_DK // 256),
                  zspec(256, Z_DV // 256),
                  tspec(512), tspec(512), tspec(128), tspec(128), tspec(256), tspec(256),
                  wspec(qn), wspec(kvn), wspec(wuq), wspec(wukv), wspec(place)],
        out_specs=[ospec(w) for w, _ in outs],
        out_shape=[jax.ShapeDtypeStruct((n, w), BF16) for w, _ in outs],
        compiler_params=_cparams(("parallel",)),
        name="attn_prep",
    )(z, z, z, z, z, z, z, cq, sq, ck, sk, cd, sd, qn, kvn, wuq, wukv, place)


def _flash_kernel(*refs, heads, n_acc, with_lat, diff, lam_init, tk):
    refs = list(refs)
    q_ref = refs.pop(0)
    if with_lat:
        kl_ref, vl_ref = refs.pop(0), refs.pop(0)
    kc_ref, vc_ref = refs.pop(0), refs.pop(0)
    if diff:
        lam_ref, sub_ref = refs.pop(0), refs.pop(0)
    o_ref, m_sc, l_sc, acc_sc = refs
    tq = q_ref.shape[0]
    lane = lax.broadcasted_iota(jnp.int32, (tq, LANES), 1)
    lo_half = lane < 64

    def tile(k_ref, v_ref, rows):
        nkb = (rows.stop - rows.start if isinstance(rows, slice) else rows.size) // LANES
        for g, (kb, vb, vh, ai) in enumerate(heads):
            q = q_ref[:, g * LANES:(g + 1) * LANES]
            k = k_ref[rows, kb * LANES:(kb + 1) * LANES].astype(BF16)
            s = _dot_nt(q, k)
            blocks = [s[:, c * LANES:(c + 1) * LANES] for c in range(nkb)]
            mx = blocks[0]
            for blk in blocks[1:]:
                mx = jnp.maximum(mx, blk)
            m_prev = m_sc[g]
            m_new = jnp.maximum(m_prev, jnp.max(mx, axis=-1, keepdims=True))
            alpha = jnp.exp2(m_prev - m_new)
            ps = [jnp.exp2(blk - m_new) for blk in blocks]
            lsum = ps[0]
            for pb in ps[1:]:
                lsum = lsum + pb
            l_sc[g] = alpha * l_sc[g] + lsum
            m_sc[g] = m_new
            p = jnp.concatenate([pb.astype(BF16) for pb in ps], axis=1)
            v = v_ref[rows, vb * LANES:(vb + 1) * LANES].astype(BF16)
            pv = _dot(p, v)
            old = acc_sc[ai, :, vb * LANES:(vb + 1) * LANES]
            mine = lo_half if vh == 0 else jnp.logical_not(lo_half)
            acc_sc[ai, :, vb * LANES:(vb + 1) * LANES] = jnp.where(mine, alpha * old + pv, old)

    m_sc[...] = jnp.full(m_sc.shape, NEG, F32)
    l_sc[...] = jnp.zeros(l_sc.shape, F32)
    acc_sc[...] = jnp.zeros(acc_sc.shape, F32)
    tile(kc_ref, vc_ref, slice(0, kc_ref.shape[0]))

    if with_lat:
        def kv_step(j, carry):
            tile(kl_ref, vl_ref, pl.ds(pl.multiple_of(j * tk, tk), tk))
            return carry

        lax.fori_loop(0, kl_ref.shape[0] // tk, kv_step, 0, unroll=4)

    def inv_l(g):
        return 1.0 / jnp.sum(l_sc[g], axis=-1, keepdims=True)

    for c in range(2):
        if not diff:
            o = jnp.where(lo_half, acc_sc[0, :, c * LANES:(c + 1) * LANES] * inv_l(2 * c),
                          acc_sc[0, :, c * LANES:(c + 1) * LANES] * inv_l(2 * c + 1))
        else:
            o1 = jnp.where(lo_half, acc_sc[0, :, c * LANES:(c + 1) * LANES] * inv_l(4 * c),
                           acc_sc[0, :, c * LANES:(c + 1) * LANES] * inv_l(4 * c + 2))
            o2 = jnp.where(lo_half, acc_sc[1, :, c * LANES:(c + 1) * LANES] * inv_l(4 * c + 1),
                           acc_sc[1, :, c * LANES:(c + 1) * LANES] * inv_l(4 * c + 3))
            o = o1 - lam_ref[...] * o2
            sq = o * o
            ms_lo = jnp.sum(jnp.where(lo_half, sq, 0.0), axis=-1, keepdims=True) * (1.0 / 64)
            ms_hi = jnp.sum(jnp.where(lo_half, 0.0, sq), axis=-1, keepdims=True) * (1.0 / 64)
            rs = jnp.where(lo_half, lax.rsqrt(ms_lo + RMS_EPS), lax.rsqrt(ms_hi + RMS_EPS))
            o = o * rs * sub_ref[:, c * LANES:(c + 1) * LANES] * (1.0 - lam_init)
        o_ref[:, c * LANES:(c + 1) * LANES] = o


def _flash(q, k, v, *, nb, s_len, l_len, heads, kcol=0, vcol=0, kw=None, vw=256,
           with_lat=True, diff=False, lam=None, subln=None, lam_init=0.0, name="flash"):
    nh = len(heads)
    n_acc = 2 if diff else 1
    ctx0 = nb * s_len // l_len
    tk = min(TK, s_len)
    if with_lat:
        tq = min(TQ, s_len)
        grid = (nb, s_len // tq)
        rows = nb * s_len
        qmap = lambda b, i: (b * (s_len // tq) + i, 0)
    else:
        tq = l_len
        grid = (nb, 1)
        rows = nb * l_len
        qmap = lambda b, i: (ctx0 + b, 0)
    in_specs = [pl.BlockSpec((tq, nh * LANES), qmap)]
    args = [q]
    if with_lat:
        in_specs += [pl.BlockSpec((s_len, kw), lambda b, i: (b, kcol)),
                     pl.BlockSpec((s_len, vw), lambda b, i: (b, vcol))]
        args += [k, v]
    in_specs += [pl.BlockSpec((l_len, kw), lambda b, i: (ctx0 + b, kcol)),
                 pl.BlockSpec((l_len, vw), lambda b, i: (ctx0 + b, vcol))]
    args += [k, v]
    if diff:
        in_specs += [pl.BlockSpec((1, LANES), lambda b, i: (0, 0)),
                     pl.BlockSpec((1, 256), lambda b, i: (0, 0))]
        args += [lam, subln]
    return pl.pallas_call(
        functools.partial(_flash_kernel, heads=heads, n_acc=n_acc, with_lat=with_lat,
                          diff=diff, lam_init=lam_init, tk=tk),
        grid=grid,
        in_specs=in_specs,
        out_specs=pl.BlockSpec((tq, 256), qmap if with_lat else (lambda b, i: (b, 0))),
        out_shape=jax.ShapeDtypeStruct((rows, 256), F32),
        scratch_shapes=[pltpu.VMEM((nh, tq, LANES), F32), pltpu.VMEM((nh, tq, LANES), F32),
                        pltpu.VMEM((n_acc, tq, 256), F32)],
        compiler_params=_cparams(("parallel", "parallel")),
        name=name,
    )(*args)


MLA_HEADSPEC = tuple((h, h // 2, h % 2, 0) for h in range(MLA_HEADS))
NA_HEADSPEC = tuple((h // 2, h // 2, h % 2, 0) for h in range(NA_HEADS))
DIFF_HEADSPEC = tuple((g // 4, g // 4, (g // 2) % 2, g % 2) for g in range(2 * DIFF_HEADS))


def _na_kernel(q_ref, k_ref, v_ref, kc_ref, vc_ref, bias_ref, o_ref, *, rows):
    rb = pl.program_id(1)
    k0 = jnp.clip(NA_QROWS * rb - NA_KH // 2, 0, rows - NA_KROWS)
    start = pl.multiple_of(k0 * GRID_W, GRID_W)
    nk = NA_KROWS * GRID_W
    kw = k_ref[pl.ds(start, nk), :].astype(BF16)
    vw = v_ref[pl.ds(start, nk), :].astype(BF16)
    kc = kc_ref[...].astype(BF16)
    vc = vc_ref[...].astype(BF16)
    tq = q_ref.shape[0]
    lane = lax.broadcasted_iota(jnp.int32, (tq, LANES), 1)
    lo_half = lane < 64
    for c in range(2):
        o_c = jnp.zeros((tq, LANES), F32)
        for half in range(2):
            h = 2 * c + half
            q = q_ref[:, h * LANES:(h + 1) * LANES]
            s_loc = _dot_nt(q, kw[:, c * LANES:(c + 1) * LANES])
            s_ctx = _dot_nt(q, kc[:, c * LANES:(c + 1) * LANES])
            blocks = [s_loc[:, i * LANES:(i + 1) * LANES] + bias_ref[0, h, :, i * LANES:(i + 1) * LANES]
                      for i in range(nk // LANES)]
            blocks += [s_ctx[:, i * LANES:(i + 1) * LANES] for i in range(kc.shape[0] // LANES)]
            mx = blocks[0]
            for blk in blocks[1:]:
                mx = jnp.maximum(mx, blk)
            m = jnp.max(mx, axis=-1, keepdims=True)
            ps = [jnp.exp2(blk - m) for blk in blocks]
            lsum = ps[0]
            for pb in ps[1:]:
                lsum = lsum + pb
            l = jnp.sum(lsum, axis=-1, keepdims=True)
            n_loc = nk // LANES
            p_loc = jnp.concatenate([pb.astype(BF16) for pb in ps[:n_loc]], axis=1)
            p_ctx = jnp.concatenate([pb.astype(BF16) for pb in ps[n_loc:]], axis=1)
            pv = (_dot(p_loc, vw[:, c * LANES:(c + 1) * LANES]) + _dot(p_ctx, vc[:, c * LANES:(c + 1) * LANES]))
            mine = lo_half if half == 0 else jnp.logical_not(lo_half)
            o_c = jnp.where(mine, pv * (1.0 / l), o_c)
        o_ref[:, c * LANES:(c + 1) * LANES] = o_c


def _na_attention(qna, z, bias, *, nb, s_len, l_len):
    rows = s_len // GRID_W
    tq = NA_QROWS * GRID_W
    nrb = rows // NA_QROWS
    ctx0 = nb * s_len // l_len

    def variant(rb):
        return jnp.where(rb == 0, 0, jnp.where(rb == nrb - 1, 2, 1))

    return pl.pallas_call(
        functools.partial(_na_kernel, rows=rows),
        grid=(nb, nrb),
        in_specs=[pl.BlockSpec((tq, NA_HEADS * LANES), lambda b, r: (b * nrb + r, 0)),
                  pl.BlockSpec((s_len, 256), lambda b, r: (b, Z_NAK // 256)),
                  pl.BlockSpec((s_len, 256), lambda b, r: (b, Z_NAV // 256)),
                  pl.BlockSpec((l_len, 256), lambda b, r: (ctx0 + b, Z_NAK // 256)),
                  pl.BlockSpec((l_len, 256), lambda b, r: (ctx0 + b, Z_NAV // 256)),
                  pl.BlockSpec((1, NA_HEADS, tq, NA_KROWS * GRID_W), lambda b, r: (variant(r), 0, 0, 0))],
        out_specs=pl.BlockSpec((tq, 256), lambda b, r: (b * nrb + r, 0)),
        out_shape=jax.ShapeDtypeStruct((nb * s_len, 256), F32),
        compiler_params=_cparams(("parallel", "arbitrary"), 56),
        name="na_attn",
    )(qna, z, z, z, z, bias)


def _na_bias_tables(rpb, rows):
    a = np.arange(NA_QROWS)
    qc = np.arange(GRID_W)
    kr_rel = np.arange(NA_KROWS)
    kc = np.arange(GRID_W)
    col0 = np.clip(qc - NA_KW // 2, 0, GRID_W - NA_KW)
    col_valid = (kc[None, :] >= col0[:, None]) & (kc[None, :] < col0[:, None] + NA_KW)
    col_off = np.clip(kc[None, :] - qc[:, None] + (NA_KW - 1), 0, 2 * NA_KW - 2)
    oh_col = (col_off[:, :, None] == np.arange(2 * NA_KW - 1)).astype(np.float32)
    big = 10 ** 6
    out = []
    for r_blk, k0, rows_eff in ((0, 0, big), (NA_QROWS, NA_KH // 2, big), (rows - NA_QROWS, rows - NA_KROWS, rows)):
        qr = r_blk + a
        r0 = np.clip(qr - NA_KH // 2, 0, rows_eff - NA_KH)
        kr = k0 + kr_rel
        row_valid = (kr[None, :] >= r0[:, None]) & (kr[None, :] < r0[:, None] + NA_KH)
        row_off = np.clip(kr[None, :] - qr[:, None] + (NA_KH - 1), 0, 2 * NA_KH - 2)
        oh_row = (row_off[:, :, None] == np.arange(2 * NA_KH - 1)).astype(np.float32)
        b = jnp.einsum('akr,hrc,qlc->haqkl', oh_row, rpb, oh_col, precision=lax.Precision.HIGHEST)
        valid = row_valid[:, None, :, None] & col_valid[None, :, None, :]
        b = jnp.where(valid[None], b * LOG2E, NEG)
        out.append(b.reshape(NA_HEADS, NA_QROWS * GRID_W, NA_KROWS * GRID_W))
    return jnp.stack(out)


def _s5_kernel(*refs, nb):
    uf_refs, ub_refs = refs[:nb], refs[nb:2 * nb]
    (bf_ref, bb_ref, a_ref, cf_ref, cb_ref, d_ref, yf_ref, yb_ref,
     uf_sc, ub_sc, buf_sc, bub_sc, hf_sc, hb_sc) = refs[2 * nb:]
    j = pl.program_id(0)
    tc = uf_refs[0].shape[0]
    half = S5_GROUPS * S5_STATE

    @pl.when(j == 0)
    def _():
        hf_sc[...] = jnp.zeros(hf_sc.shape, F32)
        hb_sc[...] = jnp.zeros(hb_sc.shape, F32)

    for b in range(nb):
        for hv in range(2):
            uf_sc[hv, pl.ds(b, tc, stride=nb), :] = uf_refs[b][:, hv * LANES:(hv + 1) * LANES]
            ub_sc[hv, pl.ds(b, tc, stride=nb), :] = ub_refs[b][:, hv * LANES:(hv + 1) * LANES]

    def scan(u_sc, b_ref, c_ref, y_ref, bu_sc, h_sc, d, reverse):
        u = jnp.concatenate([u_sc[0], u_sc[1]], axis=1)
        bu_sc[...] = _dot(u.astype(BF16), b_ref[...])
        ar = jnp.broadcast_to(a_ref[d, 0:1, :], (nb, half))
        ai = jnp.broadcast_to(a_ref[d, 1:2, :], (nb, half))

        def step(t, carry):
            hr, hi = carry
            tt = (tc - 1 - t) if reverse else t
            row = pl.multiple_of(tt * nb, nb)
            nr = ar * hr - ai * hi + bu_sc[pl.ds(row, nb), 0:half]
            ni = ar * hi + ai * hr + bu_sc[pl.ds(row, nb), half:2 * half]
            bu_sc[pl.ds(row, nb), 0:half] = nr
            bu_sc[pl.ds(row, nb), half:2 * half] = ni
            return nr, ni

        hr, hi = lax.fori_loop(0, tc, step, (h_sc[:, 0:half], h_sc[:, half:2 * half]), unroll=4)
        h_sc[:, 0:half] = hr
        h_sc[:, half:2 * half] = hi
        y = _dot(bu_sc[...].astype(BF16), c_ref[...])
        y_ref[...] = y if reverse else y + d_ref[...] * u

    scan(uf_sc, bf_ref, cf_ref, yf_ref, buf_sc, hf_sc, 0, False)
    scan(ub_sc, bb_ref, cb_ref, yb_ref, bub_sc, hb_sc, 1, True)


def _s5_scan(z, bmat, avec, cmat, d_skip, *, nb, s_len, l_len):
    tc = S5_TC
    cr = tc * nb
    nctx, nlat = l_len // tc, s_len // tc
    nchunk = nctx + nlat
    half = S5_GROUPS * S5_STATE
    n_lat = nb * s_len

    def bwd(j):
        return jnp.where(j < nctx, nctx - 1 - j, nchunk - 1 - (j - nctx))

    def u_spec(b, order):
        def row_block(j):
            c = order(j)
            return jnp.where(c < nctx, (n_lat + b * l_len) // tc + c, (b * s_len) // tc + c - nctx)
        return pl.BlockSpec((tc, 256), lambda j: (row_block(j), Z_S5 // 256))

    const = lambda shape: pl.BlockSpec(shape, lambda j: (0,) * len(shape))
    return pl.pallas_call(
        functools.partial(_s5_kernel, nb=nb),
        grid=(nchunk,),
        in_specs=[u_spec(b, lambda j: j) for b in range(nb)] + [u_spec(b, bwd) for b in range(nb)]
                 + [const((256, 2 * half)), const((256, 2 * half)), const((2, 2, half)),
                    const((2 * half, 256)), const((2 * half, 256)), const((1, 256))],
        out_specs=[pl.BlockSpec((cr, 256), lambda j: (j, 0)),
                   pl.BlockSpec((cr, 256), lambda j: (bwd(j), 0))],
        out_shape=[jax.ShapeDtypeStruct((nchunk * cr, 256), F32)] * 2,
        scratch_shapes=[pltpu.VMEM((2, cr, LANES), F32), pltpu.VMEM((2, cr, LANES), F32),
                        pltpu.VMEM((cr, 2 * half), F32), pltpu.VMEM((cr, 2 * half), F32),
                        pltpu.VMEM((nb, 2 * half), F32), pltpu.VMEM((nb, 2 * half), F32)],
        compiler_params=_cparams(("arbitrary",)),
        name="s5_scan",
    )(*([z] * (2 * nb)), bmat[0], bmat[1], avec, cmat[0], cmat[1], d_skip)


def _s5_finish_kernel(yf_ref, yb_ref, w_ref, o_ref, r_sc):
    nb, steps = o_ref.shape[0], o_ref.shape[1]
    r = _dot((yf_ref[...] + yb_ref[...]).astype(BF16), w_ref[...])
    o = r[:, :BRANCH_W] * jax.nn.sigmoid(r[:, BRANCH_W:])
    for hv in range(2):
        r_sc[hv] = o[:, hv * LANES:(hv + 1) * LANES]
    for b in range(nb):
        for hv in range(2):
            o_ref[b, :, hv * LANES:(hv + 1) * LANES] = r_sc[hv, pl.ds(b, steps, stride=nb), :]


def _s5_finish(yf, yb, w_glu, *, nb, t0, t_len):
    steps = TM // nb
    tile0 = t0 // steps
    spec = pl.BlockSpec((TM, 256), lambda i: (tile0 + i, 0))
    out = pl.pallas_call(
        _s5_finish_kernel,
        grid=(t_len // steps,),
        in_specs=[spec, spec, pl.BlockSpec((256, 512), lambda i: (0, 0))],
        out_specs=pl.BlockSpec((nb, steps, 256), lambda i: (0, i, 0)),
        out_shape=jax.ShapeDtypeStruct((nb, t_len, 256), F32),
        scratch_shapes=[pltpu.VMEM((2, TM, LANES), F32)],
        compiler_params=_cparams(("parallel",)),
        name="s5_glu",
    )(yf, yb, w_glu)
    return out.reshape(nb * t_len, 256)


def _s5_params(lam_re, lam_im, log_dt, b_re, b_im, c_re, c_im):
    lam = lax.complex(lam_re, lam_im)
    dt = jnp.exp(log_dt)[..., None]
    a_bar = jnp.exp(lam * dt)
    b_bar = ((a_bar - 1.0) / lam)[..., None] * lax.complex(b_re, b_im)
    eye = jnp.eye(S5_GROUPS, dtype=F32)
    g, p, ch = S5_GROUPS, S5_STATE, S5_GROUP_CH

    def bdiag_in(m):
        return jnp.einsum('gh,dgpc->dgchp', eye, m).reshape(2, g * ch, g * p)

    def bdiag_out(m):
        return jnp.einsum('gh,dgcp->dgphc', eye, m).reshape(2, g * p, g * ch)

    bmat = jnp.concatenate([bdiag_in(jnp.real(b_bar)), bdiag_in(jnp.imag(b_bar))], axis=2).astype(BF16)
    cmat = jnp.concatenate([bdiag_out(c_re), -bdiag_out(c_im)], axis=1).astype(BF16)
    avec = jnp.stack([jnp.real(a_bar).reshape(2, g * p), jnp.imag(a_bar).reshape(2, g * p)], axis=1)
    return bmat, avec, cmat


def _merge_kernel(*refs, n_lat_tiles, has_ctx):
    x_ref, mod_ref = refs[0], refs[1]
    n_br = 8 if has_ctx else 4
    br_refs = refs[2:2 + n_br]
    wg_ref, bg_ref, wb_ref, wo_ref, g_ref, b_ref, x1_ref, h2_ref = refs[2 + n_br:]
    d = D_MODEL
    m = mod_ref[0]
    x = x_ref[...]
    hb = (x * (1.0 + m[1:2]) + m[0:1]).astype(BF16)
    is_ctx = pl.program_id(0) >= n_lat_tiles
    acc = None
    for i in range(4):
        if has_ctx:
            o = jnp.where(is_ctx, br_refs[2 * i + 1][...], br_refs[2 * i][...])
        else:
            o = br_refs[i][...]
        gate = jax.nn.sigmoid(_dot(hb, wg_ref[:, i * d:(i + 1) * d]) + bg_ref[:, i * d:(i + 1) * d])
        term = gate * _dot(o.astype(BF16), wb_ref[i])
        acc = term if acc is None else acc + term
    y = _dot(acc.astype(BF16), wo_ref[...])
    x1 = _layer_norm(ALPHA * x + m[2:3] * y, g_ref[...], b_ref[...])
    x1_ref[...] = x1
    h2_ref[...] = x1 * (1.0 + m[4:5]) + m[3:4]


def _merge(xa, modl, outs_lat, outs_ctx, wg, bg, wb, wo, g, b, seg, n, n_lat_tiles):
    d = D_MODEL
    tok = pl.BlockSpec((TM, d), lambda i: (i, 0))
    lat = pl.BlockSpec((TM, 256), lambda i: (jnp.minimum(i, n_lat_tiles - 1), 0))
    ctx = pl.BlockSpec((TM, 256), lambda i: (jnp.maximum(i - n_lat_tiles, 0), 0))
    const = lambda a: pl.BlockSpec(a.shape, lambda i: (0,) * a.ndim)
    has_ctx = outs_ctx is not None
    if has_ctx:
        branches = [a for pair in zip(outs_lat, outs_ctx) for a in pair]
        br_specs = [lat, ctx] * 4
    else:
        branches, br_specs = list(outs_lat), [lat] * 4
    return pl.pallas_call(
        functools.partial(_merge_kernel, n_lat_tiles=n_lat_tiles, has_ctx=has_ctx),
        grid=(n // TM,),
        in_specs=[tok, pl.BlockSpec((1, 6, d), lambda i: (seg(i), 0, 0))] + br_specs
                 + [const(wg), const(bg), const(wb), const(wo), const(g), const(b)],
        out_specs=[tok, tok],
        out_shape=[jax.ShapeDtypeStruct((n, d), F32)] * 2,
        compiler_params=_cparams(("parallel",), 56),
        name="merge_ln1",
    )(xa, modl, *branches, wg, bg, wb, wo, g, b)


def _start_row_gather(src_hbm, dst, idx_ref, base, n_rows, sem):
    def body(r, c):
        pltpu.make_async_copy(src_hbm.at[pl.ds(idx_ref[base + r], 1)], dst.at[pl.ds(r, 1)], sem).start()
        return c

    lax.fori_loop(0, n_rows, body, 0, unroll=8)


def _wait_row_gather(src_hbm, dst, sem):
    pltpu.make_async_copy(src_hbm.at[pl.ds(0, dst.shape[0])], dst, sem).wait()


def _swiglu_block(xb, w1_ref, w3_ref, w2_ref, o_ref, between=None):
    n_chunks = D_FF // FF_CHUNK
    for c in range(n_chunks):
        cs = slice(c * FF_CHUNK, (c + 1) * FF_CHUNK)
        a = _dot(xb, w1_ref[0, :, cs])
        b = _dot(xb, w3_ref[0, :, cs])
        g = (a * jax.nn.sigmoid(a) * b).astype(BF16)
        y = _dot(g, w2_ref[0, cs, :])
        if c == 0:
            o_ref[...] = y
        else:
            o_ref[...] += y
        if between is not None:
            between(c, n_chunks)


def _ffn_kernel(be_ref, x_ref, w1_ref, w3_ref, w2_ref, o_ref):
    _swiglu_block(x_ref[...].astype(BF16), w1_ref, w3_ref, w2_ref, o_ref)


def _moe_ffn_kernel(be_ref, tok_ref, nused_ref, h_hbm, w1_ref, w3_ref, w2_ref, o_ref, xbuf, sem):
    i = pl.program_id(0)
    last = pl.num_programs(0) - 1
    slot = i % 2
    nxt_base = jnp.minimum(i + 1, last) * MOE_BLK
    nxt_buf, nxt_sem = xbuf.at[1 - slot], sem.at[1 - slot]

    @pl.when(i == 0)
    def _():
        _start_row_gather(h_hbm, xbuf.at[0], tok_ref, 0, MOE_BLK, sem.at[0])

    _wait_row_gather(h_hbm, xbuf.at[slot], sem.at[slot])

    def start_piece(c, n_chunks):
        per = -(-MOE_BLK // n_chunks)
        for r in range(c * per, min((c + 1) * per, MOE_BLK)):
            pltpu.make_async_copy(h_hbm.at[pl.ds(tok_ref[nxt_base + r], 1)], nxt_buf.at[pl.ds(r, 1)],
                                  nxt_sem).start()

    @pl.when(i < nused_ref[0])
    def _():
        _swiglu_block(xbuf[slot].astype(BF16), w1_ref, w3_ref, w2_ref, o_ref, between=start_piece)

    @pl.when(i >= nused_ref[0])
    def _():
        o_ref[...] = jnp.zeros(o_ref.shape, F32)
        _start_row_gather(h_hbm, nxt_buf, tok_ref, nxt_base, MOE_BLK, nxt_sem)

    @pl.when(i == last)
    def _():
        _wait_row_gather(h_hbm, nxt_buf, nxt_sem)


def _moe_ffn(h, block_e, buf_tok, n_used, w1, w3, w2):
    d = h.shape[1]
    n_rows = buf_tok.shape[0]
    wspec = lambda shape: pl.BlockSpec(shape, lambda i, be, tok, nu: (be[i], 0, 0), pipeline_mode=pl.Buffered(1))
    grid_spec = pltpu.PrefetchScalarGridSpec(
        num_scalar_prefetch=3,
        grid=(n_rows // MOE_BLK,),
        in_specs=[pl.BlockSpec(memory_space=pl.ANY), wspec((1, d, D_FF)), wspec((1, d, D_FF)), wspec((1, D_FF, d))],
        out_specs=pl.BlockSpec((MOE_BLK, d), lambda i, be, tok, nu: (i, 0)),
        scratch_shapes=[pltpu.VMEM((2, MOE_BLK, d), F32), pltpu.SemaphoreType.DMA((2,))],
    )
    return pl.pallas_call(
        _moe_ffn_kernel,
        grid_spec=grid_spec,
        out_shape=jax.ShapeDtypeStruct((n_rows, d), F32),
        compiler_params=_cparams(("arbitrary",), 56),
        name="moe_swiglu",
    )(block_e, buf_tok, n_used, h, w1, w3, w2)


def _ffn(xb, block_e, w1, w3, w2):
    n, d = xb.shape
    blk = MOE_BLK
    grid_spec = pltpu.PrefetchScalarGridSpec(
        num_scalar_prefetch=1,
        grid=(n // blk,),
        in_specs=[pl.BlockSpec((blk, d), lambda i, be: (i, 0)),
                  pl.BlockSpec((1, d, D_FF), lambda i, be: (be[i], 0, 0), pipeline_mode=pl.Buffered(1)),
                  pl.BlockSpec((1, d, D_FF), lambda i, be: (be[i], 0, 0), pipeline_mode=pl.Buffered(1)),
                  pl.BlockSpec((1, D_FF, d), lambda i, be: (be[i], 0, 0), pipeline_mode=pl.Buffered(1))],
        out_specs=pl.BlockSpec((blk, d), lambda i, be: (i, 0)),
    )
    return pl.pallas_call(
        _ffn_kernel,
        grid_spec=grid_spec,
        out_shape=jax.ShapeDtypeStruct((n, d), F32),
        compiler_params=_cparams(("arbitrary",), 56),
        name="swiglu",
    )(block_e, xb, w1, w3, w2)


def _router_kernel(h_ref, w_ref, o_ref):
    logits = jnp.dot(h_ref[...], w_ref[...], preferred_element_type=F32, precision=lax.Precision.HIGHEST)
    lane = lax.broadcasted_iota(jnp.int32, logits.shape, 1)
    lg = jnp.where(lane < N_EXPERTS, logits, NEG)
    v1 = jnp.max(lg, axis=-1, keepdims=True)
    i1 = jnp.min(jnp.where(lg == v1, lane, LANES), axis=-1, keepdims=True)
    lg2 = jnp.where(lane == i1, NEG, lg)
    v2 = jnp.max(lg2, axis=-1, keepdims=True)
    i2 = jnp.min(jnp.where(lg2 == v2, lane, LANES), axis=-1, keepdims=True)
    e = jnp.exp(v2 - v1)
    g1 = 1.0 / (1.0 + e)
    g2 = e / (1.0 + e)
    out = jnp.where(lane == 0, i1.astype(F32), jnp.where(lane == 1, i2.astype(F32),
                    jnp.where(lane == 2, g1, jnp.where(lane == 3, g2, 0.0))))
    o_ref[...] = out


def _router(h2, w_router_pad):
    n, d = h2.shape
    return pl.pallas_call(
        _router_kernel,
        grid=(n // TM,),
        in_specs=[pl.BlockSpec((TM, d), lambda i: (i, 0)), pl.BlockSpec((d, LANES), lambda i: (0, 0))],
        out_specs=pl.BlockSpec((TM, LANES), lambda i: (i, 0)),
        out_shape=jax.ShapeDtypeStruct((n, LANES), F32),
        compiler_params=_cparams(("parallel",)),
        name="router",
    )(h2, w_router_pad)


def _combine_kernel(x_ref, mod_ref, f_ref, g_ref, b_ref, o_ref):
    m = mod_ref[0]
    o_ref[...] = _layer_norm(ALPHA * x_ref[...] + m[5:6] * f_ref[...], g_ref[...], b_ref[...])


def _combine(x1, modl, f, g, b, seg):
    n, d = x1.shape
    tok = pl.BlockSpec((TM, d), lambda i: (i, 0))
    vec = pl.BlockSpec((1, d), lambda i: (0, 0))
    return pl.pallas_call(
        _combine_kernel,
        grid=(n // TM,),
        in_specs=[tok, pl.BlockSpec((1, 6, d), lambda i: (seg(i), 0, 0)), tok, vec, vec],
        out_specs=tok,
        out_shape=jax.ShapeDtypeStruct((n, d), F32),
        compiler_params=_cparams(("parallel",)),
        name="combine_ln2",
    )(x1, modl, f, g, b)


def _moe_combine_kernel(dest_ref, x_ref, mod_ref, gt_ref, y_hbm, g_ref, b_ref, o_ref, ybuf, sem):
    i = pl.program_id(0)
    last = pl.num_programs(0) - 1
    slot = i % 2
    nxt_base = jnp.minimum(i + 1, last) * 2 * TM
    nxt_buf, nxt_sem = ybuf.at[1 - slot], sem.at[1 - slot]

    @pl.when(i == 0)
    def _():
        _start_row_gather(y_hbm, ybuf.at[0], dest_ref, 0, 2 * TM, sem.at[0])

    for r in range(2 * TM):
        pltpu.make_async_copy(y_hbm.at[pl.ds(dest_ref[nxt_base + r], 1)], nxt_buf.at[pl.ds(r, 1)], nxt_sem).start()

    _wait_row_gather(y_hbm, ybuf.at[slot], sem.at[slot])
    gt = gt_ref[...]
    f = gt[:, 2:3] * ybuf[slot, 0:TM, :] + gt[:, 3:4] * ybuf[slot, TM:2 * TM, :]
    m = mod_ref[0]
    o_ref[...] = _layer_norm(ALPHA * x_ref[...] + m[5:6] * f, g_ref[...], b_ref[...])

    @pl.when(i == last)
    def _():
        _wait_row_gather(y_hbm, nxt_buf, nxt_sem)


def _moe_combine(x1, modl, y_rows, dest_tiles, gates, g, b, seg):
    n, d = x1.shape
    tok = pl.BlockSpec((TM, d), lambda i, dst: (i, 0))
    vec = pl.BlockSpec((1, d), lambda i, dst: (0, 0))
    grid_spec = pltpu.PrefetchScalarGridSpec(
        num_scalar_prefetch=1,
        grid=(n // TM,),
        in_specs=[tok, pl.BlockSpec((1, 6, d), lambda i, dst: (seg(i), 0, 0)),
                  pl.BlockSpec((TM, LANES), lambda i, dst: (i, 0)), pl.BlockSpec(memory_space=pl.ANY), vec, vec],
        out_specs=tok,
        scratch_shapes=[pltpu.VMEM((2, 2 * TM, d), F32), pltpu.SemaphoreType.DMA((2,))],
    )
    return pl.pallas_call(
        _moe_combine_kernel,
        grid_spec=grid_spec,
        out_shape=jax.ShapeDtypeStruct((n, d), F32),
        compiler_params=_cparams(("arbitrary",)),
        name="moe_combine_ln2",
    )(dest_tiles, x1, modl, gates, y_rows, g, b)


def _rope_tables(s_len, width, rope_lane, extra_rows):
    t = jnp.arange(s_len)
    lane = np.arange(width)
    inv = ROPE_BASE ** (-(jnp.asarray(lane % 8, F32)) / 8.0)
    pos = jnp.where((lane & 16) == 0, (t // GRID_W)[:, None], (t % GRID_W)[:, None]).astype(F32)
    ang = pos * inv[None, :]
    cos = jnp.where(rope_lane[None, :], jnp.cos(ang), 1.0)
    sin = jnp.where(rope_lane[None, :], jnp.where((lane & 8) == 0, -jnp.sin(ang), jnp.sin(ang)), 0.0)
    cos = jnp.concatenate([cos, jnp.ones((extra_rows, width), F32)])
    sin = jnp.concatenate([sin, jnp.zeros((extra_rows, width), F32)])
    return cos, sin


def _pad_heads(w, n_heads, width):
    k = w.shape[0]
    w = w.reshape(k, n_heads, -1)
    return jnp.pad(w, ((0, 0), (0, 0), (0, width - w.shape[-1]))).reshape(k, n_heads * width)


def _route(top, n_tok):
    e_flat = top[:, 0:2].astype(jnp.int32).reshape(-1)
    n_assign = 2 * n_tok
    onehot = (e_flat[:, None] == jnp.arange(N_EXPERTS)[None, :]).astype(jnp.int32)
    csum = jnp.cumsum(onehot, axis=0)
    rank = jnp.take_along_axis(csum, e_flat[:, None], axis=1)[:, 0] - 1
    counts = csum[-1]
    padded = (counts + MOE_BLK - 1) // MOE_BLK * MOE_BLK
    pad_end = jnp.cumsum(padded)
    pad_start = pad_end - padded
    dest = pad_start[e_flat] + rank
    n_rows = -(-n_assign // MOE_BLK) * MOE_BLK + N_EXPERTS * MOE_BLK
    buf_tok = jnp.zeros((n_rows,), jnp.int32).at[dest].set(jnp.arange(n_assign, dtype=jnp.int32) // 2)
    block_e = jnp.minimum(jnp.searchsorted(pad_end, jnp.arange(n_rows // MOE_BLK) * MOE_BLK, side='right'),
                          N_EXPERTS - 1).astype(jnp.int32)
    n_used = (pad_end[-1:] // MOE_BLK).astype(jnp.int32)
    dest_tiles = dest.astype(jnp.int32).reshape(n_tok // TM, TM, 2).transpose(0, 2, 1).reshape(-1)
    return buf_tok, block_e, n_used, dest_tiles


def kernel(x, c, ctx, c_ctx, w_ada, b_ada, w_in, na_rpb, mla_q_norm, mla_kv_norm, mla_w_uq, mla_w_ukv,
           s5_lam_re, s5_lam_im, s5_log_dt, s5_b_re, s5_b_im, s5_c_re, s5_c_im, s5_d, s5_w_glu,
           diff_lam_q1, diff_lam_k1, diff_lam_q2, diff_lam_k2, diff_subln,
           w_branch, w_gate, b_gate, w_out, ln1_g, ln1_b, ln2_g, ln2_b,
           ffn_w1, ffn_w3, ffn_w2, moe_router, moe_w1, moe_w3, moe_w2):
    nb, s_len, d = x.shape
    l_len = ctx.shape[1]
    n_lat, n_ctx = nb * s_len, nb * l_len
    n_tot = n_lat + n_ctx
    rows = s_len // GRID_W
    assert d == D_MODEL and s_len % TM == 0 and n_ctx % TM == 0 and rows % NA_QROWS == 0 and rows >= NA_KROWS
    assert l_len % S5_TC == 0 and s_len % S5_TC == 0

    def seg(i):
        return jnp.minimum((i * TM) // s_len, nb)

    n_lat_tiles = n_lat // TM

    xa = jnp.concatenate([x.reshape(n_lat, d), ctx.reshape(n_ctx, d)], axis=0)
    nrow_mod = -(-(nb + 1) // 16) * 16
    cvec = jnp.zeros((nrow_mod, d), F32).at[:nb].set(c).at[nb].set(c_ctx)
    mod_all = _ada_all(cvec, w_ada, b_ada).reshape(DEPTH, nrow_mod, 6, d)

    lane512 = np.arange(512)
    q_rope = (lane512 % LANES >= MLA_NOPE) & (lane512 % LANES < MLA_NOPE + MLA_ROPE)
    cq, sq = _rope_tables(s_len, 512, q_rope, TM)
    ck, sk = _rope_tables(s_len, LANES, np.arange(LANES) < MLA_ROPE, TM)
    cd, sd = _rope_tables(s_len, 256, np.ones((256,), bool), TM)
    tabs = (cq, sq, ck, sk, cd, sd)
    place = np.zeros((LANES, MLA_HEADS * LANES), np.float32)
    for h in range(MLA_HEADS):
        place[np.arange(MLA_ROPE), h * LANES + MLA_NOPE + np.arange(MLA_ROPE)] = 1.0
    place = jnp.asarray(place, BF16)

    for layer in range(DEPTH):
        ctx_out = layer < DEPTH - 1
        lam_init = 0.8 - 0.6 * math.exp(-0.3 * layer)
        modl = mod_all[layer]

        wi = w_in[layer]
        w_in_pad = jnp.concatenate([wi[:, :1184], jnp.zeros((d, 96), F32), wi[:, 1184:]], axis=1).astype(BF16)
        wuq = _pad_heads(mla_w_uq[layer], MLA_HEADS, LANES).astype(BF16)
        wukv4 = mla_w_ukv[layer].reshape(-1, MLA_HEADS, MLA_NOPE + MLA_V)
        wuk = jnp.pad(wukv4[:, :, :MLA_NOPE], ((0, 0), (0, 0), (0, LANES - MLA_NOPE))).reshape(-1, MLA_HEADS * LANES)
        wuv = wukv4[:, :, MLA_NOPE:].reshape(-1, MLA_HEADS * MLA_V)
        wukv = jnp.concatenate([wuk, wuv], axis=1).astype(BF16)
        bias = _na_bias_tables(na_rpb[layer], rows)
        bmat, avec, cmat = _s5_params(s5_lam_re[layer], s5_lam_im[layer], s5_log_dt[layer], s5_b_re[layer],
                                      s5_b_im[layer], s5_c_re[layer], s5_c_im[layer])
        lam = (jnp.exp(jnp.sum(diff_lam_q1[layer] * diff_lam_k1[layer]))
               - jnp.exp(jnp.sum(diff_lam_q2[layer] * diff_lam_k2[layer])) + lam_init)
        lam_vec = jnp.full((1, LANES), lam, F32)
        subln = jnp.tile(diff_subln[layer], DIFF_HEADS).reshape(1, 256)

        z = _inproj(xa, modl, w_in_pad, seg)
        qm, km, vm, qd, kd, vd, qna = _prep(z, tabs, mla_q_norm[layer].reshape(1, -1),
                                            mla_kv_norm[layer].reshape(1, -1), wuq, wukv, place,
                                            nb=nb, s_len=s_len)
        common = dict(nb=nb, s_len=s_len, l_len=l_len)
        o_na = _na_attention(qna, z, bias, **common)
        o_mla = _flash(qm, km, vm, heads=MLA_HEADSPEC, kw=512, name="mla_attn", **common)
        o_diff = _flash(qd, kd, vd, heads=DIFF_HEADSPEC, kw=256, diff=True, lam=lam_vec,
                        subln=subln, lam_init=lam_init, name="diff_attn", **common)
        yf, yb = _s5_scan(z, bmat, avec, cmat, s5_d[layer].reshape(1, 256), **common)
        w_glu = s5_w_glu[layer].astype(BF16)
        o_s5_lat = _s5_finish(yf, yb, w_glu, nb=nb, t0=l_len, t_len=s_len)
        if ctx_out:
            o_na_c = _flash(qna, z, z, heads=NA_HEADSPEC, kcol=Z_NAK // 256, vcol=Z_NAV // 256,
                            kw=256, with_lat=False, name="na_ctx_attn", **common)
            o_mla_c = _flash(qm, km, vm, heads=MLA_HEADSPEC, kw=512, with_lat=False,
                             name="mla_ctx_attn", **common)
            o_diff_c = _flash(qd, kd, vd, heads=DIFF_HEADSPEC, kw=256, diff=True, lam=lam_vec,
                              subln=subln, lam_init=lam_init, with_lat=False, name="diff_ctx_attn", **common)
            o_s5_c = _s5_finish(yf, yb, w_glu, nb=nb, t0=0, t_len=l_len)
            outs_ctx = [o_na_c, o_mla_c, o_s5_c, o_diff_c]
            n_act = n_tot
        else:
            outs_ctx = None
            n_act = n_lat
        x1, h2 = _merge(xa, modl, [o_na, o_mla, o_s5_lat, o_diff], outs_ctx,
                        w_gate[layer].astype(BF16), b_gate[layer].reshape(1, -1),
                        w_branch[layer].astype(BF16), w_out[layer].astype(BF16),
                        ln1_g[layer].reshape(1, d), ln1_b[layer].reshape(1, d), seg, n_act, n_lat_tiles)

        jj = layer // 2
        g2, b2 = ln2_g[layer].reshape(1, d), ln2_b[layer].reshape(1, d)
        if layer % 2 == 0:
            be = jnp.zeros((n_act // MOE_BLK,), jnp.int32)
            f = _ffn(h2, be, ffn_w1[jj:jj + 1].astype(BF16), ffn_w3[jj:jj + 1].astype(BF16),
                     ffn_w2[jj:jj + 1].astype(BF16))
            xa = _combine(x1, modl, f, g2, b2, seg)
        else:
            wr = jnp.pad(moe_router[jj], ((0, 0), (0, LANES - N_EXPERTS)))
            top = _router(h2, wr)
            buf_tok, block_e, n_used, dest_tiles = _route(top, n_act)
            y_rows = _moe_ffn(h2, block_e, buf_tok, n_used, moe_w1[jj].astype(BF16), moe_w3[jj].astype(BF16),
                              moe_w2[jj].astype(BF16))
            xa = _moe_combine(x1, modl, y_rows, dest_tiles, top, g2, b2, seg)
    return xa[:n_lat].reshape(nb, s_len, d)
```

```python
import functools
import math

import jax
import jax.numpy as jnp
import numpy as np
from jax import lax
from jax.experimental import pallas as pl
from jax.experimental.pallas import tpu as pltpu

F32 = jnp.float32
BF16 = jnp.bfloat16

D_MODEL = 1024
DEPTH = 4
GRID_W = 64
BRANCH_W = 256
NA_HEADS = 4
NA_HD = 64
NA_KH = 8
NA_KW = 16
MLA_HEADS = 4
MLA_NOPE = 64
MLA_ROPE = 32
MLA_V = 64
S5_GROUP_CH = 16
S5_GROUPS = 16
S5_STATE = 64
DIFF_HEADS = 4
DIFF_HD = 32
D_FF = 3584
N_EXPERTS = 8
ROPE_BASE = 10000.0
LN_EPS = 1e-5
RMS_EPS = 1e-6
ALPHA = (2 * DEPTH) ** 0.25
NA_SCALE = NA_HD ** -0.5
MLA_SCALE = (MLA_NOPE + MLA_ROPE) ** -0.5
DIFF_SCALE = DIFF_HD ** -0.5
LOG2E = 1.4426950408889634

LANES = 128
ZW = 2304
Z_NAQ, Z_NAK, Z_NAV, Z_MQ, Z_MKV, Z_MKR, Z_S5, Z_DQ, Z_DK, Z_DV = (
    0, 256, 512, 768, 1024, 1152, 1280, 1536, 1792, 2048)
NEG = -1e30
TM = 512
TQ = 1024
TK = 256
NA_QROWS = 8
NA_KROWS = NA_QROWS + NA_KH
S5_TC = 128
FF_CHUNK = 512
MOE_BLK = 512


def _cparams(sem, vmem_mb=48):
    return pltpu.CompilerParams(dimension_semantics=sem, vmem_limit_bytes=vmem_mb << 20)


def _dot(a, b):
    return jnp.dot(a, b, preferred_element_type=F32)


def _dot_nt(a, b):
    return lax.dot_general(a, b, (((1,), (1,)), ((), ())), preferred_element_type=F32)


def _layer_norm(r, g, b):
    rc = r - jnp.mean(r, axis=-1, keepdims=True)
    var = jnp.mean(rc * rc, axis=-1, keepdims=True)
    return rc * lax.rsqrt(var + LN_EPS) * g + b


def _ada_kernel(c_ref, w_ref, b_ref, o_ref):
    c = c_ref[...]
    cond = c * jax.nn.sigmoid(c)
    o_ref[0] = _dot(cond.astype(BF16), w_ref[0].astype(BF16)) + b_ref[0]


def _ada_all(cvec, w_ada, b_ada):
    nrow = cvec.shape[0]
    d = D_MODEL
    return pl.pallas_call(
        _ada_kernel,
        grid=(DEPTH, 6),
        in_specs=[pl.BlockSpec((nrow, d), lambda l, n: (0, 0)),
                  pl.BlockSpec((1, d, d), lambda l, n: (l, 0, n)),
                  pl.BlockSpec((1, 1, d), lambda l, n: (l, 0, n))],
        out_specs=pl.BlockSpec((1, nrow, d), lambda l, n: (l, 0, n)),
        out_shape=jax.ShapeDtypeStruct((DEPTH, nrow, 6 * d), F32),
        compiler_params=_cparams(("parallel", "parallel")),
        name="ada_mod",
    )(cvec, w_ada, b_ada.reshape(DEPTH, 1, 6 * d))


def _swap8(z):
    w = z.shape[-1]
    lane = lax.broadcasted_iota(jnp.int32, z.shape, 1)
    up = pltpu.roll(z, w - 8, 1)
    dn = pltpu.roll(z, 8, 1)
    return jnp.where((lane & 8) == 0, up, dn)


def _rms(z, g):
    return z * lax.rsqrt(jnp.mean(z * z, axis=-1, keepdims=True) + RMS_EPS) * g


def _proj_kernel(x_ref, mod_ref, w_ref, cq_ref, sq_ref, ck_ref, sk_ref, cd_ref, sd_ref,
                 qn_ref, kvn_ref, wuq_ref, wukv_ref, place_ref,
                 qm_ref, km_ref, vm_ref, qd_ref, kd_ref, vd_ref, qna_ref, kna_ref, vna_ref, u_ref):
    m = mod_ref[0]
    h = x_ref[...] * (1.0 + m[1:2]) + m[0:1]
    z = _dot(h.astype(BF16), w_ref[...])
    kna_ref[...] = z[:, Z_NAK:Z_NAK + 256].astype(BF16)
    vna_ref[...] = z[:, Z_NAV:Z_NAV + 256].astype(BF16)
    u_ref[...] = z[:, Z_S5:Z_S5 + 256]
    aq = _rms(z[:, Z_MQ:Z_MQ + 256], qn_ref[...])
    q = _dot(aq.astype(BF16), wuq_ref[...])
    q = q * cq_ref[...] + _swap8(q) * sq_ref[...]
    qm_ref[...] = (q * (MLA_SCALE * LOG2E)).astype(BF16)
    akv = _rms(z[:, Z_MKV:Z_MKV + 128], kvn_ref[...])
    kv = _dot(akv.astype(BF16), wukv_ref[...])
    kr = z[:, Z_MKR:Z_MKR + 128]
    kr = kr * ck_ref[...] + _swap8(kr) * sk_ref[...]
    km_ref[...] = (kv[:, :4 * LANES] + _dot(kr.astype(BF16), place_ref[...])).astype(BF16)
    vm_ref[...] = kv[:, 4 * LANES:].astype(BF16)
    dq = z[:, Z_DQ:Z_DQ + 256]
    dq = (dq * cd_ref[...] + _swap8(dq) * sd_ref[...]) * (DIFF_SCALE * LOG2E)
    dk = z[:, Z_DK:Z_DK + 256]
    dk = dk * cd_ref[...] + _swap8(dk) * sd_ref[...]
    kd_ref[...] = dk.astype(BF16)
    vd_ref[...] = z[:, Z_DV:Z_DV + 256].astype(BF16)
    lane = lax.broadcasted_iota(jnp.int32, (dq.shape[0], LANES), 1)
    for g in range(2 * DIFF_HEADS):
        blk = dq[:, (g // 4) * LANES:(g // 4 + 1) * LANES]
        qd_ref[:, g * LANES:(g + 1) * LANES] = jnp.where((lane // DIFF_HD) == (g % 4), blk, 0.0).astype(BF16)
    naq = z[:, Z_NAQ:Z_NAQ + 256] * (NA_SCALE * LOG2E)
    for hd in range(NA_HEADS):
        blk = naq[:, (hd // 2) * LANES:(hd // 2 + 1) * LANES]
        qna_ref[:, hd * LANES:(hd + 1) * LANES] = jnp.where((lane // NA_HD) == (hd % 2), blk, 0.0).astype(BF16)


def _proj(xa, modl, w_in_pad, tabs, qn, kvn, wuq, wukv, place, *, nb, s_len):
    n, d = xa.shape
    cq, sq, ck, sk, cd, sd = tabs
    tps = s_len // TM
    n_lat_tiles = nb * tps

    def tile(i):
        return jnp.where(i < n_lat_tiles, (i % nb) * tps + i // nb, i)

    def seg(i):
        return jnp.minimum((tile(i) * TM) // s_len, nb)

    def tab_idx(i):
        return jnp.where(i < n_lat_tiles, i // nb, tps)

    def tspec(width):
        return pl.BlockSpec((TM, width), lambda i: (tab_idx(i), 0))

    def wspec(a):
        return pl.BlockSpec(a.shape, lambda i: (0,) * a.ndim)

    def ospec(width):
        return pl.BlockSpec((TM, width), lambda i: (tile(i), 0))

    widths = [512, 512, 256, 1024, 256, 256, 512, 256, 256]
    return pl.pallas_call(
        _proj_kernel,
        grid=(n // TM,),
        in_specs=[pl.BlockSpec((TM, d), lambda i: (tile(i), 0)),
                  pl.BlockSpec((1, 6, d), lambda i: (seg(i), 0, 0)),
                  wspec(w_in_pad),
                  tspec(512), tspec(512), tspec(128), tspec(128), tspec(256), tspec(256),
                  wspec(qn), wspec(kvn), wspec(wuq), wspec(wukv), wspec(place)],
        out_specs=[ospec(w) for w in widths] + [ospec(256)],
        out_shape=[jax.ShapeDtypeStruct((n, w), BF16) for w in widths] + [jax.ShapeDtypeStruct((n, 256), F32)],
        compiler_params=_cparams(("parallel",), 56),
        name="in_proj_prep",
    )(xa, modl, w_in_pad, cq, sq, ck, sk, cd, sd, qn, kvn, wuq, wukv, place)


def _flash_kernel(*refs, heads, n_acc, with_lat, diff, lam_init, tk):
    refs = list(refs)
    q_ref = refs.pop(0)
    if with_lat:
        kl_ref, vl_ref = refs.pop(0), refs.pop(0)
    kc_ref, vc_ref = refs.pop(0), refs.pop(0)
    if diff:
        lam_ref, sub_ref = refs.pop(0), refs.pop(0)
    o_ref, m_sc, l_sc, acc_sc = refs
    tq = q_ref.shape[0]
    lane = lax.broadcasted_iota(jnp.int32, (tq, LANES), 1)
    lo_half = lane < 64

    def tile(k_ref, v_ref, rows):
        nkb = (rows.stop - rows.start if isinstance(rows, slice) else rows.size) // LANES
        for g, (kb, vb, vh, ai) in enumerate(heads):
            q = q_ref[:, g * LANES:(g + 1) * LANES]
            k = k_ref[rows, kb * LANES:(kb + 1) * LANES].astype(BF16)
            s = _dot_nt(q, k)
            blocks = [s[:, c * LANES:(c + 1) * LANES] for c in range(nkb)]
            mx = blocks[0]
            for blk in blocks[1:]:
                mx = jnp.maximum(mx, blk)
            m_prev = m_sc[g]
            m_new = jnp.maximum(m_prev, jnp.max(mx, axis=-1, keepdims=True))
            alpha = jnp.exp2(m_prev - m_new)
            ps = [jnp.exp2(blk - m_new) for blk in blocks]
            lsum = ps[0]
            for pb in ps[1:]:
                lsum = lsum + pb
            l_sc[g] = alpha * l_sc[g] + lsum
            m_sc[g] = m_new
            p = jnp.concatenate([pb.astype(BF16) for pb in ps], axis=1)
            v = v_ref[rows, vb * LANES:(vb + 1) * LANES].astype(BF16)
            pv = _dot(p, v)
            old = acc_sc[ai, :, vb * LANES:(vb + 1) * LANES]
            mine = lo_half if vh == 0 else jnp.logical_not(lo_half)
            acc_sc[ai, :, vb * LANES:(vb + 1) * LANES] = jnp.where(mine, alpha * old + pv, old)

    m_sc[...] = jnp.full(m_sc.shape, NEG, F32)
    l_sc[...] = jnp.zeros(l_sc.shape, F32)
    acc_sc[...] = jnp.zeros(acc_sc.shape, F32)
    tile(kc_ref, vc_ref, slice(0, kc_ref.shape[0]))

    if with_lat:
        def kv_step(j, carry):
            tile(kl_ref, vl_ref, pl.ds(pl.multiple_of(j * tk, tk), tk))
            return carry

        lax.fori_loop(0, kl_ref.shape[0] // tk, kv_step, 0, unroll=4)

    def inv_l(g):
        return 1.0 / jnp.sum(l_sc[g], axis=-1, keepdims=True)

    for c in range(2):
        if not diff:
            o = jnp.where(lo_half, acc_sc[0, :, c * LANES:(c + 1) * LANES] * inv_l(2 * c),
                          acc_sc[0, :, c * LANES:(c + 1) * LANES] * inv_l(2 * c + 1))
        else:
            o1 = jnp.where(lo_half, acc_sc[0, :, c * LANES:(c + 1) * LANES] * inv_l(4 * c),
                           acc_sc[0, :, c * LANES:(c + 1) * LANES] * inv_l(4 * c + 2))
            o2 = jnp.where(lo_half, acc_sc[1, :, c * LANES:(c + 1) * LANES] * inv_l(4 * c + 1),
                           acc_sc[1, :, c * LANES:(c + 1) * LANES] * inv_l(4 * c + 3))
            o = o1 - lam_ref[...] * o2
            sq = o * o
            ms_lo = jnp.sum(jnp.where(lo_half, sq, 0.0), axis=-1, keepdims=True) * (1.0 / 64)
            ms_hi = jnp.sum(jnp.where(lo_half, 0.0, sq), axis=-1, keepdims=True) * (1.0 / 64)
            rs = jnp.where(lo_half, lax.rsqrt(ms_lo + RMS_EPS), lax.rsqrt(ms_hi + RMS_EPS))
            o = o * rs * sub_ref[:, c * LANES:(c + 1) * LANES] * (1.0 - lam_init)
        o_ref[:, c * LANES:(c + 1) * LANES] = o


def _flash(q, k, v, *, nb, s_len, l_len, heads, kcol=0, vcol=0, kw=None, vw=256,
           with_lat=True, diff=False, lam=None, subln=None, lam_init=0.0, name="flash"):
    nh = len(heads)
    n_acc = 2 if diff else 1
    ctx0 = nb * s_len // l_len
    tk = min(TK, s_len)
    if with_lat:
        tq = min(TQ, s_len)
        grid = (nb, s_len // tq)
        rows = nb * s_len
        qmap = lambda b, i: (b * (s_len // tq) + i, 0)
    else:
        tq = l_len
        grid = (nb, 1)
        rows = nb * l_len
        qmap = lambda b, i: (ctx0 + b, 0)
    in_specs = [pl.BlockSpec((tq, nh * LANES), qmap)]
    args = [q]
    if with_lat:
        in_specs += [pl.BlockSpec((s_len, kw), lambda b, i: (b, kcol)),
                     pl.BlockSpec((s_len, vw), lambda b, i: (b, vcol))]
        args += [k, v]
    in_specs += [pl.BlockSpec((l_len, kw), lambda b, i: (ctx0 + b, kcol)),
                 pl.BlockSpec((l_len, vw), lambda b, i: (ctx0 + b, vcol))]
    args += [k, v]
    if diff:
        in_specs += [pl.BlockSpec((1, LANES), lambda b, i: (0, 0)),
                     pl.BlockSpec((1, 256), lambda b, i: (0, 0))]
        args += [lam, subln]
    return pl.pallas_call(
        functools.partial(_flash_kernel, heads=heads, n_acc=n_acc, with_lat=with_lat,
                          diff=diff, lam_init=lam_init, tk=tk),
        grid=grid,
        in_specs=in_specs,
        out_specs=pl.BlockSpec((tq, 256), qmap if with_lat else (lambda b, i: (b, 0))),
        out_shape=jax.ShapeDtypeStruct((rows, 256), F32),
        scratch_shapes=[pltpu.VMEM((nh, tq, LANES), F32), pltpu.VMEM((nh, tq, LANES), F32),
                        pltpu.VMEM((n_acc, tq, 256), F32)],
        compiler_params=_cparams(("parallel", "parallel")),
        name=name,
    )(*args)


MLA_HEADSPEC = tuple((h, h // 2, h % 2, 0) for h in range(MLA_HEADS))
NA_HEADSPEC = tuple((h // 2, h // 2, h % 2, 0) for h in range(NA_HEADS))
DIFF_HEADSPEC = tuple((g // 4, g // 4, (g // 2) % 2, g % 2) for g in range(2 * DIFF_HEADS))


def _na_kernel(q_ref, k_ref, v_ref, kc_ref, vc_ref, bias_ref, o_ref, *, rows):
    rb = pl.program_id(1)
    k0 = jnp.clip(NA_QROWS * rb - NA_KH // 2, 0, rows - NA_KROWS)
    start = pl.multiple_of(k0 * GRID_W, GRID_W)
    nk = NA_KROWS * GRID_W
    kw = k_ref[pl.ds(start, nk), :].astype(BF16)
    vw = v_ref[pl.ds(start, nk), :].astype(BF16)
    kc = kc_ref[...].astype(BF16)
    vc = vc_ref[...].astype(BF16)
    tq = q_ref.shape[0]
    lane = lax.broadcasted_iota(jnp.int32, (tq, LANES), 1)
    lo_half = lane < 64
    for c in range(2):
        o_c = jnp.zeros((tq, LANES), F32)
        for half in range(2):
            h = 2 * c + half
            q = q_ref[:, h * LANES:(h + 1) * LANES]
            s_loc = _dot_nt(q, kw[:, c * LANES:(c + 1) * LANES])
            s_ctx = _dot_nt(q, kc[:, c * LANES:(c + 1) * LANES])
            blocks = [s_loc[:, i * LANES:(i + 1) * LANES] + bias_ref[0, h, :, i * LANES:(i + 1) * LANES]
                      for i in range(nk // LANES)]
            blocks += [s_ctx[:, i * LANES:(i + 1) * LANES] for i in range(kc.shape[0] // LANES)]
            mx = blocks[0]
            for blk in blocks[1:]:
                mx = jnp.maximum(mx, blk)
            m = jnp.max(mx, axis=-1, keepdims=True)
            ps = [jnp.exp2(blk - m) for blk in blocks]
            lsum = ps[0]
            for pb in ps[1:]:
                lsum = lsum + pb
            l = jnp.sum(lsum, axis=-1, keepdims=True)
            n_loc = nk // LANES
            p_loc = jnp.concatenate([pb.astype(BF16) for pb in ps[:n_loc]], axis=1)
            p_ctx = jnp.concatenate([pb.astype(BF16) for pb in ps[n_loc:]], axis=1)
            pv = (_dot(p_loc, vw[:, c * LANES:(c + 1) * LANES]) + _dot(p_ctx, vc[:, c * LANES:(c + 1) * LANES]))
            mine = lo_half if half == 0 else jnp.logical_not(lo_half)
            o_c = jnp.where(mine, pv * (1.0 / l), o_c)
        o_ref[:, c * LANES:(c + 1) * LANES] = o_c


def _na_attention(qna, kna, vna, bias, *, nb, s_len, l_len):
    rows = s_len // GRID_W
    tq = NA_QROWS * GRID_W
    nrb = rows // NA_QROWS
    ctx0 = nb * s_len // l_len

    def variant(rb):
        return jnp.where(rb == 0, 0, jnp.where(rb == nrb - 1, 2, 1))

    return pl.pallas_call(
        functools.partial(_na_kernel, rows=rows),
        grid=(nb, nrb),
        in_specs=[pl.BlockSpec((tq, NA_HEADS * LANES), lambda b, r: (b * nrb + r, 0)),
                  pl.BlockSpec((s_len, 256), lambda b, r: (b, 0)),
                  pl.BlockSpec((s_len, 256), lambda b, r: (b, 0)),
                  pl.BlockSpec((l_len, 256), lambda b, r: (ctx0 + b, 0)),
                  pl.BlockSpec((l_len, 256), lambda b, r: (ctx0 + b, 0)),
                  pl.BlockSpec((1, NA_HEADS, tq, NA_KROWS * GRID_W), lambda b, r: (variant(r), 0, 0, 0))],
        out_specs=pl.BlockSpec((tq, 256), lambda b, r: (b * nrb + r, 0)),
        out_shape=jax.ShapeDtypeStruct((nb * s_len, 256), F32),
        compiler_params=_cparams(("parallel", "arbitrary"), 56),
        name="na_attn",
    )(qna, kna, vna, kna, vna, bias)


def _na_bias_tables(rpb, rows):
    a = np.arange(NA_QROWS)
    qc = np.arange(GRID_W)
    kr_rel = np.arange(NA_KROWS)
    kc = np.arange(GRID_W)
    col0 = np.clip(qc - NA_KW // 2, 0, GRID_W - NA_KW)
    col_valid = (kc[None, :] >= col0[:, None]) & (kc[None, :] < col0[:, None] + NA_KW)
    col_off = np.clip(kc[None, :] - qc[:, None] + (NA_KW - 1), 0, 2 * NA_KW - 2)
    oh_col = (col_off[:, :, None] == np.arange(2 * NA_KW - 1)).astype(np.float32)
    big = 10 ** 6
    out = []
    for r_blk, k0, rows_eff in ((0, 0, big), (NA_QROWS, NA_KH // 2, big), (rows - NA_QROWS, rows - NA_KROWS, rows)):
        qr = r_blk + a
        r0 = np.clip(qr - NA_KH // 2, 0, rows_eff - NA_KH)
        kr = k0 + kr_rel
        row_valid = (kr[None, :] >= r0[:, None]) & (kr[None, :] < r0[:, None] + NA_KH)
        row_off = np.clip(kr[None, :] - qr[:, None] + (NA_KH - 1), 0, 2 * NA_KH - 2)
        oh_row = (row_off[:, :, None] == np.arange(2 * NA_KH - 1)).astype(np.float32)
        b = jnp.einsum('akr,hrc,qlc->haqkl', oh_row, rpb, oh_col, precision=lax.Precision.HIGHEST)
        valid = row_valid[:, None, :, None] & col_valid[None, :, None, :]
        b = jnp.where(valid[None], b * LOG2E, NEG)
        out.append(b.reshape(NA_HEADS, NA_QROWS * GRID_W, NA_KROWS * GRID_W))
    return jnp.stack(out)


def _s5_kernel(*refs, nb):
    uf_refs, ub_refs = refs[:nb], refs[nb:2 * nb]
    (bf_ref, bb_ref, a_ref, cf_ref, cb_ref, d_ref, yf_ref, yb_ref,
     uf_sc, ub_sc, buf_sc, bub_sc, hf_sc, hb_sc) = refs[2 * nb:]
    j = pl.program_id(0)
    tc = uf_refs[0].shape[0]
    half = S5_GROUPS * S5_STATE

    @pl.when(j == 0)
    def _():
        hf_sc[...] = jnp.zeros(hf_sc.shape, F32)
        hb_sc[...] = jnp.zeros(hb_sc.shape, F32)

    for b in range(nb):
        for hv in range(2):
            uf_sc[hv, pl.ds(b, tc, stride=nb), :] = uf_refs[b][:, hv * LANES:(hv + 1) * LANES]
            ub_sc[hv, pl.ds(b, tc, stride=nb), :] = ub_refs[b][:, hv * LANES:(hv + 1) * LANES]

    def scan(u_sc, b_ref, c_ref, y_ref, bu_sc, h_sc, d, reverse):
        u = jnp.concatenate([u_sc[0], u_sc[1]], axis=1)
        bu_sc[...] = _dot(u.astype(BF16), b_ref[...])
        ar = jnp.broadcast_to(a_ref[d, 0:1, :], (nb, half))
        ai = jnp.broadcast_to(a_ref[d, 1:2, :], (nb, half))

        def step(t, carry):
            hr, hi = carry
            tt = (tc - 1 - t) if reverse else t
            row = pl.multiple_of(tt * nb, nb)
            nr = ar * hr - ai * hi + bu_sc[pl.ds(row, nb), 0:half]
            ni = ar * hi + ai * hr + bu_sc[pl.ds(row, nb), half:2 * half]
            bu_sc[pl.ds(row, nb), 0:half] = nr
            bu_sc[pl.ds(row, nb), half:2 * half] = ni
            return nr, ni

        hr, hi = lax.fori_loop(0, tc, step, (h_sc[:, 0:half], h_sc[:, half:2 * half]), unroll=4)
        h_sc[:, 0:half] = hr
        h_sc[:, half:2 * half] = hi
        y = _dot(bu_sc[...].astype(BF16), c_ref[...])
        y_ref[...] = y if reverse else y + d_ref[...] * u

    scan(uf_sc, bf_ref, cf_ref, yf_ref, buf_sc, hf_sc, 0, False)
    scan(ub_sc, bb_ref, cb_ref, yb_ref, bub_sc, hb_sc, 1, True)


def _s5_scan(z, bmat, avec, cmat, d_skip, *, nb, s_len, l_len):
    tc = S5_TC
    cr = tc * nb
    nctx, nlat = l_len // tc, s_len // tc
    nchunk = nctx + nlat
    half = S5_GROUPS * S5_STATE
    n_lat = nb * s_len

    def bwd(j):
        return jnp.where(j < nctx, nctx - 1 - j, nchunk - 1 - (j - nctx))

    def u_spec(b, order):
        def row_block(j):
            c = order(j)
            return jnp.where(c < nctx, (n_lat + b * l_len) // tc + c, (b * s_len) // tc + c - nctx)
        return pl.BlockSpec((tc, 256), lambda j: (row_block(j), 0))

    const = lambda shape: pl.BlockSpec(shape, lambda j: (0,) * len(shape))
    return pl.pallas_call(
        functools.partial(_s5_kernel, nb=nb),
        grid=(nchunk,),
        in_specs=[u_spec(b, lambda j: j) for b in range(nb)] + [u_spec(b, bwd) for b in range(nb)]
                 + [const((256, 2 * half)), const((256, 2 * half)), const((2, 2, half)),
                    const((2 * half, 256)), const((2 * half, 256)), const((1, 256))],
        out_specs=[pl.BlockSpec((cr, 256), lambda j: (j, 0)),
                   pl.BlockSpec((cr, 256), lambda j: (bwd(j), 0))],
        out_shape=[jax.ShapeDtypeStruct((nchunk * cr, 256), F32)] * 2,
        scratch_shapes=[pltpu.VMEM((2, cr, LANES), F32), pltpu.VMEM((2, cr, LANES), F32),
                        pltpu.VMEM((cr, 2 * half), F32), pltpu.VMEM((cr, 2 * half), F32),
                        pltpu.VMEM((nb, 2 * half), F32), pltpu.VMEM((nb, 2 * half), F32)],
        compiler_params=_cparams(("arbitrary",)),
        name="s5_scan",
    )(*([z] * (2 * nb)), bmat[0], bmat[1], avec, cmat[0], cmat[1], d_skip)


def _s5_finish_kernel(yf_ref, yb_ref, w_ref, o_ref, r_sc):
    nb, steps = o_ref.shape[0], o_ref.shape[1]
    r = _dot((yf_ref[...] + yb_ref[...]).astype(BF16), w_ref[...])
    o = r[:, :BRANCH_W] * jax.nn.sigmoid(r[:, BRANCH_W:])
    for hv in range(2):
        r_sc[hv] = o[:, hv * LANES:(hv + 1) * LANES]
    for b in range(nb):
        for hv in range(2):
            o_ref[b, :, hv * LANES:(hv + 1) * LANES] = r_sc[hv, pl.ds(b, steps, stride=nb), :]


def _s5_finish(yf, yb, w_glu, *, nb, t0, t_len):
    steps = TM // nb
    tile0 = t0 // steps
    spec = pl.BlockSpec((TM, 256), lambda i: (tile0 + i, 0))
    out = pl.pallas_call(
        _s5_finish_kernel,
        grid=(t_len // steps,),
        in_specs=[spec, spec, pl.BlockSpec((256, 512), lambda i: (0, 0))],
        out_specs=pl.BlockSpec((nb, steps, 256), lambda i: (0, i, 0)),
        out_shape=jax.ShapeDtypeStruct((nb, t_len, 256), F32),
        scratch_shapes=[pltpu.VMEM((2, TM, LANES), F32)],
        compiler_params=_cparams(("parallel",)),
        name="s5_glu",
    )(yf, yb, w_glu)
    return out.reshape(nb * t_len, 256)


def _s5_params(lam_re, lam_im, log_dt, b_re, b_im, c_re, c_im):
    lam = lax.complex(lam_re, lam_im)
    dt = jnp.exp(log_dt)[..., None]
    a_bar = jnp.exp(lam * dt)
    b_bar = ((a_bar - 1.0) / lam)[..., None] * lax.complex(b_re, b_im)
    eye = jnp.eye(S5_GROUPS, dtype=F32)
    g, p, ch = S5_GROUPS, S5_STATE, S5_GROUP_CH

    def bdiag_in(m):
        return jnp.einsum('gh,dgpc->dgchp', eye, m).reshape(2, g * ch, g * p)

    def bdiag_out(m):
        return jnp.einsum('gh,dgcp->dgphc', eye, m).reshape(2, g * p, g * ch)

    bmat = jnp.concatenate([bdiag_in(jnp.real(b_bar)), bdiag_in(jnp.imag(b_bar))], axis=2).astype(BF16)
    cmat = jnp.concatenate([bdiag_out(c_re), -bdiag_out(c_im)], axis=1).astype(BF16)
    avec = jnp.stack([jnp.real(a_bar).reshape(2, g * p), jnp.imag(a_bar).reshape(2, g * p)], axis=1)
    return bmat, avec, cmat


def _merge_kernel(*refs, n_lat_tiles, has_ctx):
    x_ref, mod_ref = refs[0], refs[1]
    n_br = 8 if has_ctx else 4
    br_refs = refs[2:2 + n_br]
    wg_ref, bg_ref, wb_ref, wo_ref, g_ref, b_ref, x1_ref, h2_ref = refs[2 + n_br:]
    d = D_MODEL
    m = mod_ref[0]
    x = x_ref[...]
    hb = (x * (1.0 + m[1:2]) + m[0:1]).astype(BF16)
    is_ctx = pl.program_id(0) >= n_lat_tiles
    acc = None
    for i in range(4):
        if has_ctx:
            o = jnp.where(is_ctx, br_refs[2 * i + 1][...], br_refs[2 * i][...])
        else:
            o = br_refs[i][...]
        gate = jax.nn.sigmoid(_dot(hb, wg_ref[:, i * d:(i + 1) * d]) + bg_ref[:, i * d:(i + 1) * d])
        term = gate * _dot(o.astype(BF16), wb_ref[i])
        acc = term if acc is None else acc + term
    y = _dot(acc.astype(BF16), wo_ref[...])
    x1 = _layer_norm(ALPHA * x + m[2:3] * y, g_ref[...], b_ref[...])
    x1_ref[...] = x1
    h2_ref[...] = x1 * (1.0 + m[4:5]) + m[3:4]


def _merge(xa, modl, outs_lat, outs_ctx, wg, bg, wb, wo, g, b, seg, n, n_lat_tiles):
    d = D_MODEL
    tok = pl.BlockSpec((TM, d), lambda i: (i, 0))
    lat = pl.BlockSpec((TM, 256), lambda i: (jnp.minimum(i, n_lat_tiles - 1), 0))
    ctx = pl.BlockSpec((TM, 256), lambda i: (jnp.maximum(i - n_lat_tiles, 0), 0))
    const = lambda a: pl.BlockSpec(a.shape, lambda i: (0,) * a.ndim)
    has_ctx = outs_ctx is not None
    if has_ctx:
        branches = [a for pair in zip(outs_lat, outs_ctx) for a in pair]
        br_specs = [lat, ctx] * 4
    else:
        branches, br_specs = list(outs_lat), [lat] * 4
    return pl.pallas_call(
        functools.partial(_merge_kernel, n_lat_tiles=n_lat_tiles, has_ctx=has_ctx),
        grid=(n // TM,),
        in_specs=[tok, pl.BlockSpec((1, 6, d), lambda i: (seg(i), 0, 0))] + br_specs
                 + [const(wg), const(bg), const(wb), const(wo), const(g), const(b)],
        out_specs=[tok, tok],
        out_shape=[jax.ShapeDtypeStruct((n, d), F32)] * 2,
        compiler_params=_cparams(("parallel",), 56),
        name="merge_ln1",
    )(xa, modl, *branches, wg, bg, wb, wo, g, b)


def _start_row_gather(src_hbm, dst, idx_ref, base, n_rows, sem):
    def body(r, c):
        pltpu.make_async_copy(src_hbm.at[pl.ds(idx_ref[base + r], 1)], dst.at[pl.ds(r, 1)], sem).start()
        return c

    lax.fori_loop(0, n_rows, body, 0, unroll=8)


def _wait_row_gather(src_hbm, dst, sem):
    pltpu.make_async_copy(src_hbm.at[pl.ds(0, dst.shape[0])], dst, sem).wait()


def _swiglu_block(xb, w1_ref, w3_ref, w2_ref, o_ref, between=None):
    n_chunks = D_FF // FF_CHUNK
    for c in range(n_chunks):
        cs = slice(c * FF_CHUNK, (c + 1) * FF_CHUNK)
        a = _dot(xb, w1_ref[0, :, cs])
        b = _dot(xb, w3_ref[0, :, cs])
        g = (a * jax.nn.sigmoid(a) * b).astype(BF16)
        y = _dot(g, w2_ref[0, cs, :])
        if c == 0:
            o_ref[...] = y
        else:
            o_ref[...] += y
        if between is not None:
            between(c, n_chunks)


def _ffn_kernel(h_ref, x_ref, mod_ref, w1_ref, w3_ref, w2_ref, g_ref, b_ref, o_ref, acc_sc):
    _swiglu_block(h_ref[...].astype(BF16), w1_ref, w3_ref, w2_ref, acc_sc)
    m = mod_ref[0]
    o_ref[...] = _layer_norm(ALPHA * x_ref[...] + m[5:6] * acc_sc[...], g_ref[...], b_ref[...])


def _moe_ffn_kernel(be_ref, tok_ref, nused_ref, h_hbm, w1_ref, w3_ref, w2_ref, o_ref, xbuf, sem):
    i = pl.program_id(0)
    last = pl.num_programs(0) - 1
    slot = i % 2
    nxt_base = jnp.minimum(i + 1, last) * MOE_BLK
    nxt_buf, nxt_sem = xbuf.at[1 - slot], sem.at[1 - slot]

    @pl.when(i == 0)
    def _():
        _start_row_gather(h_hbm, xbuf.at[0], tok_ref, 0, MOE_BLK, sem.at[0])

    _wait_row_gather(h_hbm, xbuf.at[slot], sem.at[slot])

    def start_piece(c, n_chunks):
        per = -(-MOE_BLK // n_chunks)
        for r in range(c * per, min((c + 1) * per, MOE_BLK)):
            pltpu.make_async_copy(h_hbm.at[pl.ds(tok_ref[nxt_base + r], 1)], nxt_buf.at[pl.ds(r, 1)],
                                  nxt_sem).start()

    @pl.when(i < nused_ref[0])
    def _():
        _swiglu_block(xbuf[slot].astype(BF16), w1_ref, w3_ref, w2_ref, o_ref, between=start_piece)

    @pl.when(i >= nused_ref[0])
    def _():
        o_ref[...] = jnp.zeros(o_ref.shape, F32)
        _start_row_gather(h_hbm, nxt_buf, tok_ref, nxt_base, MOE_BLK, nxt_sem)

    @pl.when(i == last)
    def _():
        _wait_row_gather(h_hbm, nxt_buf, nxt_sem)


def _moe_ffn(h, block_e, buf_tok, n_used, w1, w3, w2):
    d = h.shape[1]
    n_rows = buf_tok.shape[0]
    wspec = lambda shape: pl.BlockSpec(shape, lambda i, be, tok, nu: (be[i], 0, 0), pipeline_mode=pl.Buffered(1))
    grid_spec = pltpu.PrefetchScalarGridSpec(
        num_scalar_prefetch=3,
        grid=(n_rows // MOE_BLK,),
        in_specs=[pl.BlockSpec(memory_space=pl.ANY), wspec((1, d, D_FF)), wspec((1, d, D_FF)), wspec((1, D_FF, d))],
        out_specs=pl.BlockSpec((MOE_BLK, d), lambda i, be, tok, nu: (i, 0)),
        scratch_shapes=[pltpu.VMEM((2, MOE_BLK, d), F32), pltpu.SemaphoreType.DMA((2,))],
    )
    return pl.pallas_call(
        _moe_ffn_kernel,
        grid_spec=grid_spec,
        out_shape=jax.ShapeDtypeStruct((n_rows, d), F32),
        compiler_params=_cparams(("arbitrary",), 56),
        name="moe_swiglu",
    )(block_e, buf_tok, n_used, h, w1, w3, w2)


def _ffn(h2, x1, modl, w1, w3, w2, g, b, seg):
    n, d = h2.shape
    tok = pl.BlockSpec((TM, d), lambda i: (i, 0))
    vec = pl.BlockSpec((1, d), lambda i: (0, 0))
    wspec = lambda shape: pl.BlockSpec(shape, lambda i: (0, 0, 0), pipeline_mode=pl.Buffered(1))
    return pl.pallas_call(
        _ffn_kernel,
        grid=(n // TM,),
        in_specs=[tok, tok, pl.BlockSpec((1, 6, d), lambda i: (seg(i), 0, 0)),
                  wspec((1, d, D_FF)), wspec((1, d, D_FF)), wspec((1, D_FF, d)), vec, vec],
        out_specs=tok,
        out_shape=jax.ShapeDtypeStruct((n, d), F32),
        scratch_shapes=[pltpu.VMEM((TM, d), F32)],
        compiler_params=_cparams(("parallel",), 56),
        name="swiglu_ln2",
    )(h2, x1, modl, w1, w3, w2, g, b)


def _router_kernel(h_ref, w_ref, o_ref):
    logits = jnp.dot(h_ref[...], w_ref[...], preferred_element_type=F32, precision=lax.Precision.HIGHEST)
    lane = lax.broadcasted_iota(jnp.int32, logits.shape, 1)
    lg = jnp.where(lane < N_EXPERTS, logits, NEG)
    v1 = jnp.max(lg, axis=-1, keepdims=True)
    i1 = jnp.min(jnp.where(lg == v1, lane, LANES), axis=-1, keepdims=True)
    lg2 = jnp.where(lane == i1, NEG, lg)
    v2 = jnp.max(lg2, axis=-1, keepdims=True)
    i2 = jnp.min(jnp.where(lg2 == v2, lane, LANES), axis=-1, keepdims=True)
    e = jnp.exp(v2 - v1)
    g1 = 1.0 / (1.0 + e)
    g2 = e / (1.0 + e)
    out = jnp.where(lane == 0, i1.astype(F32), jnp.where(lane == 1, i2.astype(F32),
                    jnp.where(lane == 2, g1, jnp.where(lane == 3, g2, 0.0))))
    o_ref[...] = out


def _router(h2, w_router_pad):
    n, d = h2.shape
    return pl.pallas_call(
        _router_kernel,
        grid=(n // TM,),
        in_specs=[pl.BlockSpec((TM, d), lambda i: (i, 0)), pl.BlockSpec((d, LANES), lambda i: (0, 0))],
        out_specs=pl.BlockSpec((TM, LANES), lambda i: (i, 0)),
        out_shape=jax.ShapeDtypeStruct((n, LANES), F32),
        compiler_params=_cparams(("parallel",)),
        name="router",
    )(h2, w_router_pad)


def _moe_combine_kernel(dest_ref, x_ref, mod_ref, gt_ref, y_hbm, g_ref, b_ref, o_ref, ybuf, sem):
    i = pl.program_id(0)
    last = pl.num_programs(0) - 1
    slot = i % 2
    nxt_base = jnp.minimum(i + 1, last) * 2 * TM
    nxt_buf, nxt_sem = ybuf.at[1 - slot], sem.at[1 - slot]

    @pl.when(i == 0)
    def _():
        _start_row_gather(y_hbm, ybuf.at[0], dest_ref, 0, 2 * TM, sem.at[0])

    for r in range(2 * TM):
        pltpu.make_async_copy(y_hbm.at[pl.ds(dest_ref[nxt_base + r], 1)], nxt_buf.at[pl.ds(r, 1)], nxt_sem).start()

    _wait_row_gather(y_hbm, ybuf.at[slot], sem.at[slot])
    gt = gt_ref[...]
    f = gt[:, 2:3] * ybuf[slot, 0:TM, :] + gt[:, 3:4] * ybuf[slot, TM:2 * TM, :]
    m = mod_ref[0]
    o_ref[...] = _layer_norm(ALPHA * x_ref[...] + m[5:6] * f, g_ref[...], b_ref[...])

    @pl.when(i == last)
    def _():
        _wait_row_gather(y_hbm, nxt_buf, nxt_sem)


def _moe_combine(x1, modl, y_rows, dest_tiles, gates, g, b, seg):
    n, d = x1.shape
    tok = pl.BlockSpec((TM, d), lambda i, dst: (i, 0))
    vec = pl.BlockSpec((1, d), lambda i, dst: (0, 0))
    grid_spec = pltpu.PrefetchScalarGridSpec(
        num_scalar_prefetch=1,
        grid=(n // TM,),
        in_specs=[tok, pl.BlockSpec((1, 6, d), lambda i, dst: (seg(i), 0, 0)),
                  pl.BlockSpec((TM, LANES), lambda i, dst: (i, 0)), pl.BlockSpec(memory_space=pl.ANY), vec, vec],
        out_specs=tok,
        scratch_shapes=[pltpu.VMEM((2, 2 * TM, d), F32), pltpu.SemaphoreType.DMA((2,))],
    )
    return pl.pallas_call(
        _moe_combine_kernel,
        grid_spec=grid_spec,
        out_shape=jax.ShapeDtypeStruct((n, d), F32),
        compiler_params=_cparams(("arbitrary",)),
        name="moe_combine_ln2",
    )(dest_tiles, x1, modl, gates, y_rows, g, b)


def _rope_tables(s_len, width, rope_lane, extra_rows):
    t = jnp.arange(s_len)
    lane = np.arange(width)
    inv = ROPE_BASE ** (-(jnp.asarray(lane % 8, F32)) / 8.0)
    pos = jnp.where((lane & 16) == 0, (t // GRID_W)[:, None], (t % GRID_W)[:, None]).astype(F32)
    ang = pos * inv[None, :]
    cos = jnp.where(rope_lane[None, :], jnp.cos(ang), 1.0)
    sin = jnp.where(rope_lane[None, :], jnp.where((lane & 8) == 0, -jnp.sin(ang), jnp.sin(ang)), 0.0)
    cos = jnp.concatenate([cos, jnp.ones((extra_rows, width), F32)])
    sin = jnp.concatenate([sin, jnp.zeros((extra_rows, width), F32)])
    return cos, sin


def _pad_heads(w, n_heads, width):
    k = w.shape[0]
    w = w.reshape(k, n_heads, -1)
    return jnp.pad(w, ((0, 0), (0, 0), (0, width - w.shape[-1]))).reshape(k, n_heads * width)


def _route(top, n_tok):
    e_flat = top[:, 0:2].astype(jnp.int32).reshape(-1)
    n_assign = 2 * n_tok
    onehot = (e_flat[:, None] == jnp.arange(N_EXPERTS)[None, :]).astype(jnp.int32)
    csum = jnp.cumsum(onehot, axis=0)
    rank = jnp.take_along_axis(csum, e_flat[:, None], axis=1)[:, 0] - 1
    counts = csum[-1]
    padded = (counts + MOE_BLK - 1) // MOE_BLK * MOE_BLK
    pad_end = jnp.cumsum(padded)
    pad_start = pad_end - padded
    dest = pad_start[e_flat] + rank
    n_rows = -(-n_assign // MOE_BLK) * MOE_BLK + N_EXPERTS * MOE_BLK
    buf_tok = jnp.zeros((n_rows,), jnp.int32).at[dest].set(jnp.arange(n_assign, dtype=jnp.int32) // 2,
                                                           unique_indices=True)
    block_e = jnp.minimum(jnp.searchsorted(pad_end, jnp.arange(n_rows // MOE_BLK) * MOE_BLK, side='right'),
                          N_EXPERTS - 1).astype(jnp.int32)
    n_used = (pad_end[-1:] // MOE_BLK).astype(jnp.int32)
    dest_tiles = dest.astype(jnp.int32).reshape(n_tok // TM, TM, 2).transpose(0, 2, 1).reshape(-1)
    return buf_tok, block_e, n_used, dest_tiles


def kernel(x, c, ctx, c_ctx, w_ada, b_ada, w_in, na_rpb, mla_q_norm, mla_kv_norm, mla_w_uq, mla_w_ukv,
           s5_lam_re, s5_lam_im, s5_log_dt, s5_b_re, s5_b_im, s5_c_re, s5_c_im, s5_d, s5_w_glu,
           diff_lam_q1, diff_lam_k1, diff_lam_q2, diff_lam_k2, diff_subln,
           w_branch, w_gate, b_gate, w_out, ln1_g, ln1_b, ln2_g, ln2_b,
           ffn_w1, ffn_w3, ffn_w2, moe_router, moe_w1, moe_w3, moe_w2):
    nb, s_len, d = x.shape
    l_len = ctx.shape[1]
    n_lat, n_ctx = nb * s_len, nb * l_len
    n_tot = n_lat + n_ctx
    rows = s_len // GRID_W
    assert d == D_MODEL and s_len % TM == 0 and n_ctx % TM == 0 and rows % NA_QROWS == 0 and rows >= NA_KROWS
    assert l_len % S5_TC == 0 and s_len % S5_TC == 0

    def seg(i):
        return jnp.minimum((i * TM) // s_len, nb)

    n_lat_tiles = n_lat // TM

    xa = jnp.concatenate([x.reshape(n_lat, d), ctx.reshape(n_ctx, d)], axis=0)
    nrow_mod = -(-(nb + 1) // 16) * 16
    cvec = jnp.zeros((nrow_mod, d), F32).at[:nb].set(c).at[nb].set(c_ctx)
    mod_all = _ada_all(cvec, w_ada, b_ada).reshape(DEPTH, nrow_mod, 6, d)

    lane512 = np.arange(512)
    q_rope = (lane512 % LANES >= MLA_NOPE) & (lane512 % LANES < MLA_NOPE + MLA_ROPE)
    cq, sq = _rope_tables(s_len, 512, q_rope, TM)
    ck, sk = _rope_tables(s_len, LANES, np.arange(LANES) < MLA_ROPE, TM)
    cd, sd = _rope_tables(s_len, 256, np.ones((256,), bool), TM)
    tabs = (cq, sq, ck, sk, cd, sd)
    place = np.zeros((LANES, MLA_HEADS * LANES), np.float32)
    for h in range(MLA_HEADS):
        place[np.arange(MLA_ROPE), h * LANES + MLA_NOPE + np.arange(MLA_ROPE)] = 1.0
    place = jnp.asarray(place, BF16)

    for layer in range(DEPTH):
        ctx_out = layer < DEPTH - 1
        lam_init = 0.8 - 0.6 * math.exp(-0.3 * layer)
        modl = mod_all[layer]

        wi = w_in[layer]
        w_in_pad = jnp.concatenate([wi[:, :1184], jnp.zeros((d, 96), F32), wi[:, 1184:]], axis=1).astype(BF16)
        wuq = _pad_heads(mla_w_uq[layer], MLA_HEADS, LANES).astype(BF16)
        wukv4 = mla_w_ukv[layer].reshape(-1, MLA_HEADS, MLA_NOPE + MLA_V)
        wuk = jnp.pad(wukv4[:, :, :MLA_NOPE], ((0, 0), (0, 0), (0, LANES - MLA_NOPE))).reshape(-1, MLA_HEADS * LANES)
        wuv = wukv4[:, :, MLA_NOPE:].reshape(-1, MLA_HEADS * MLA_V)
        wukv = jnp.concatenate([wuk, wuv], axis=1).astype(BF16)
        bias = _na_bias_tables(na_rpb[layer], rows)
        bmat, avec, cmat = _s5_params(s5_lam_re[layer], s5_lam_im[layer], s5_log_dt[layer], s5_b_re[layer],
                                      s5_b_im[layer], s5_c_re[layer], s5_c_im[layer])
        lam = (jnp.exp(jnp.sum(diff_lam_q1[layer] * diff_lam_k1[layer]))
               - jnp.exp(jnp.sum(diff_lam_q2[layer] * diff_lam_k2[layer])) + lam_init)
        lam_vec = jnp.full((1, LANES), lam, F32)
        subln = jnp.tile(diff_subln[layer], DIFF_HEADS).reshape(1, 256)

        qm, km, vm, qd, kd, vd, qna, kna, vna, u5 = _proj(
            xa, modl, w_in_pad, tabs, mla_q_norm[layer].reshape(1, -1), mla_kv_norm[layer].reshape(1, -1),
            wuq, wukv, place, nb=nb, s_len=s_len)
        common = dict(nb=nb, s_len=s_len, l_len=l_len)
        o_na = _na_attention(qna, kna, vna, bias, **common)
        o_mla = _flash(qm, km, vm, heads=MLA_HEADSPEC, kw=512, name="mla_attn", **common)
        o_diff = _flash(qd, kd, vd, heads=DIFF_HEADSPEC, kw=256, diff=True, lam=lam_vec,
                        subln=subln, lam_init=lam_init, name="diff_attn", **common)
        yf, yb = _s5_scan(u5, bmat, avec, cmat, s5_d[layer].reshape(1, 256), **common)
        w_glu = s5_w_glu[layer].astype(BF16)
        o_s5_lat = _s5_finish(yf, yb, w_glu, nb=nb, t0=l_len, t_len=s_len)
        if ctx_out:
            o_na_c = _flash(qna, kna, vna, heads=NA_HEADSPEC, kw=256, with_lat=False, name="na_ctx_attn", **common)
            o_mla_c = _flash(qm, km, vm, heads=MLA_HEADSPEC, kw=512, with_lat=False,
                             name="mla_ctx_attn", **common)
            o_diff_c = _flash(qd, kd, vd, heads=DIFF_HEADSPEC, kw=256, diff=True, lam=lam_vec,
                              subln=subln, lam_init=lam_init, with_lat=False, name="diff_ctx_attn", **common)
            o_s5_c = _s5_finish(yf, yb, w_glu, nb=nb, t0=0, t_len=l_len)
            outs_ctx = [o_na_c, o_mla_c, o_s5_c, o_diff_c]
            n_act = n_tot
        else:
            outs_ctx = None
            n_act = n_lat
        x1, h2 = _merge(xa, modl, [o_na, o_mla, o_s5_lat, o_diff], outs_ctx,
                        w_gate[layer].astype(BF16), b_gate[layer].reshape(1, -1),
                        w_branch[layer].astype(BF16), w_out[layer].astype(BF16),
                        ln1_g[layer].reshape(1, d), ln1_b[layer].reshape(1, d), seg, n_act, n_lat_tiles)

        jj = layer // 2
        g2, b2 = ln2_g[layer].reshape(1, d), ln2_b[layer].reshape(1, d)
        if layer % 2 == 0:
            xa = _ffn(h2, x1, modl, ffn_w1[jj:jj + 1].astype(BF16), ffn_w3[jj:jj + 1].astype(BF16),
                      ffn_w2[jj:jj + 1].astype(BF16), g2, b2, seg)
        else:
            wr = jnp.pad(moe_router[jj], ((0, 0), (0, LANES - N_EXPERTS)))
            top = _router(h2, wr)
            buf_tok, block_e, n_used, dest_tiles = _route(top, n_act)
            y_rows = _moe_ffn(h2, block_e, buf_tok, n_used, moe_w1[jj].astype(BF16), moe_w3[jj].astype(BF16),
                              moe_w2[jj].astype(BF16))
            xa = _moe_combine(x1, modl, y_rows, dest_tiles, top, g2, b2, seg)
    return xa[:n_lat].reshape(nb, s_len, d)
```

```python
import functools
import math

import jax
import jax.numpy as jnp
import numpy as np
from jax import lax
from jax.experimental import pallas as pl
from jax.experimental.pallas import tpu as pltpu

F32 = jnp.float32
BF16 = jnp.bfloat16

D_MODEL = 1024
DEPTH = 4
GRID_W = 64
BRANCH_W = 256
NA_HEADS = 4
NA_HD = 64
NA_KH = 8
NA_KW = 16
MLA_HEADS = 4
MLA_NOPE = 64
MLA_ROPE = 32
MLA_V = 64
S5_GROUP_CH = 16
S5_GROUPS = 16
S5_STATE = 64
DIFF_HEADS = 4
DIFF_HD = 32
D_FF = 3584
N_EXPERTS = 8
ROPE_BASE = 10000.0
LN_EPS = 1e-5
RMS_EPS = 1e-6
ALPHA = (2 * DEPTH) ** 0.25
NA_SCALE = NA_HD ** -0.5
MLA_SCALE = (MLA_NOPE + MLA_ROPE) ** -0.5
DIFF_SCALE = DIFF_HD ** -0.5
LOG2E = 1.4426950408889634

LANES = 128
ZW = 2304
Z_NAQ, Z_NAK, Z_NAV, Z_MQ, Z_MKV, Z_MKR, Z_S5, Z_DQ, Z_DK, Z_DV = (
    0, 256, 512, 768, 1024, 1152, 1280, 1536, 1792, 2048)
NEG = -1e30
TM = 512
TQ = 1024
TK = 256
NA_QROWS = 8
NA_KROWS = NA_QROWS + NA_KH
S5_TC = 128
FF_CHUNK = 512
MOE_BLK = 512


def _cparams(sem, vmem_mb=48):
    return pltpu.CompilerParams(dimension_semantics=sem, vmem_limit_bytes=vmem_mb << 20)


def _dot(a, b):
    return jnp.dot(a, b, preferred_element_type=F32)


def _dot_nt(a, b):
    return lax.dot_general(a, b, (((1,), (1,)), ((), ())), preferred_element_type=F32)


def _layer_norm(r, g, b):
    rc = r - jnp.mean(r, axis=-1, keepdims=True)
    var = jnp.mean(rc * rc, axis=-1, keepdims=True)
    return rc * lax.rsqrt(var + LN_EPS) * g + b


def _ada_kernel(c_ref, w_ref, b_ref, o_ref):
    c = c_ref[...]
    cond = c * jax.nn.sigmoid(c)
    o_ref[0] = _dot(cond.astype(BF16), w_ref[0].astype(BF16)) + b_ref[0]


def _ada_all(cvec, w_ada, b_ada):
    nrow = cvec.shape[0]
    d = D_MODEL
    return pl.pallas_call(
        _ada_kernel,
        grid=(DEPTH, 6),
        in_specs=[pl.BlockSpec((nrow, d), lambda l, n: (0, 0)),
                  pl.BlockSpec((1, d, d), lambda l, n: (l, 0, n)),
                  pl.BlockSpec((1, 1, d), lambda l, n: (l, 0, n))],
        out_specs=pl.BlockSpec((1, nrow, d), lambda l, n: (l, 0, n)),
        out_shape=jax.ShapeDtypeStruct((DEPTH, nrow, 6 * d), F32),
        compiler_params=_cparams(("parallel", "parallel")),
        name="ada_mod",
    )(cvec, w_ada, b_ada.reshape(DEPTH, 1, 6 * d))


def _swap8(z):
    w = z.shape[-1]
    lane = lax.broadcasted_iota(jnp.int32, z.shape, 1)
    up = pltpu.roll(z, w - 8, 1)
    dn = pltpu.roll(z, 8, 1)
    return jnp.where((lane & 8) == 0, up, dn)


def _rms(z, g):
    return z * lax.rsqrt(jnp.mean(z * z, axis=-1, keepdims=True) + RMS_EPS) * g


def _proj_kernel(x_ref, mod_ref, w_ref, cq_ref, sq_ref, ck_ref, sk_ref, cd_ref, sd_ref,
                 qn_ref, kvn_ref, wuq_ref, wukv_ref, place_ref,
                 qm_ref, km_ref, vm_ref, qd_ref, kd_ref, vd_ref, qna_ref, kna_ref, vna_ref, u_ref):
    m = mod_ref[0]
    h = x_ref[...] * (1.0 + m[1:2]) + m[0:1]
    z = _dot(h.astype(BF16), w_ref[...])
    kna_ref[...] = z[:, Z_NAK:Z_NAK + 256].astype(BF16)
    vna_ref[...] = z[:, Z_NAV:Z_NAV + 256].astype(BF16)
    u_ref[...] = z[:, Z_S5:Z_S5 + 256]
    aq = _rms(z[:, Z_MQ:Z_MQ + 256], qn_ref[...])
    q = _dot(aq.astype(BF16), wuq_ref[...])
    q = q * cq_ref[...] + _swap8(q) * sq_ref[...]
    qm_ref[...] = (q * (MLA_SCALE * LOG2E)).astype(BF16)
    akv = _rms(z[:, Z_MKV:Z_MKV + 128], kvn_ref[...])
    kv = _dot(akv.astype(BF16), wukv_ref[...])
    kr = z[:, Z_MKR:Z_MKR + 128]
    kr = kr * ck_ref[...] + _swap8(kr) * sk_ref[...]
    km_ref[...] = (kv[:, :4 * LANES] + _dot(kr.astype(BF16), place_ref[...])).astype(BF16)
    vm_ref[...] = kv[:, 4 * LANES:].astype(BF16)
    dq = z[:, Z_DQ:Z_DQ + 256]
    dq = (dq * cd_ref[...] + _swap8(dq) * sd_ref[...]) * (DIFF_SCALE * LOG2E)
    dk = z[:, Z_DK:Z_DK + 256]
    dk = dk * cd_ref[...] + _swap8(dk) * sd_ref[...]
    kd_ref[...] = dk.astype(BF16)
    vd_ref[...] = z[:, Z_DV:Z_DV + 256].astype(BF16)
    lane = lax.broadcasted_iota(jnp.int32, (dq.shape[0], LANES), 1)
    for g in range(2 * DIFF_HEADS):
        blk = dq[:, (g // 4) * LANES:(g // 4 + 1) * LANES]
        qd_ref[:, g * LANES:(g + 1) * LANES] = jnp.where((lane // DIFF_HD) == (g % 4), blk, 0.0).astype(BF16)
    naq = z[:, Z_NAQ:Z_NAQ + 256] * (NA_SCALE * LOG2E)
    for hd in range(NA_HEADS):
        blk = naq[:, (hd // 2) * LANES:(hd // 2 + 1) * LANES]
        qna_ref[:, hd * LANES:(hd + 1) * LANES] = jnp.where((lane // NA_HD) == (hd % 2), blk, 0.0).astype(BF16)


def _proj(xa, modl, w_in_pad, tabs, qn, kvn, wuq, wukv, place, *, nb, s_len):
    n, d = xa.shape
    cq, sq, ck, sk, cd, sd = tabs
    tps = s_len // TM
    n_lat_tiles = nb * tps

    def tile(i):
        return jnp.where(i < n_lat_tiles, (i % nb) * tps + i // nb, i)

    def seg(i):
        return jnp.minimum((tile(i) * TM) // s_len, nb)

    def tab_idx(i):
        return jnp.where(i < n_lat_tiles, i // nb, tps)

    def tspec(width):
        return pl.BlockSpec((TM, width), lambda i: (tab_idx(i), 0))

    def wspec(a):
        return pl.BlockSpec(a.shape, lambda i: (0,) * a.ndim)

    def ospec(width):
        return pl.BlockSpec((TM, width), lambda i: (tile(i), 0))

    widths = [512, 512, 256, 1024, 256, 256, 512, 256, 256]
    return pl.pallas_call(
        _proj_kernel,
        grid=(n // TM,),
        in_specs=[pl.BlockSpec((TM, d), lambda i: (tile(i), 0)),
                  pl.BlockSpec((1, 6, d), lambda i: (seg(i), 0, 0)),
                  wspec(w_in_pad),
                  tspec(512), tspec(512), tspec(128), tspec(128), tspec(256), tspec(256),
                  wspec(qn), wspec(kvn), wspec(wuq), wspec(wukv), wspec(place)],
        out_specs=[ospec(w) for w in widths] + [ospec(256)],
        out_shape=[jax.ShapeDtypeStruct((n, w), BF16) for w in widths] + [jax.ShapeDtypeStruct((n, 256), F32)],
        compiler_params=_cparams(("parallel",), 56),
        name="in_proj_prep",
    )(xa, modl, w_in_pad, cq, sq, ck, sk, cd, sd, qn, kvn, wuq, wukv, place)


def _flash_kernel(*refs, heads, n_acc, with_lat, diff, lam_init, tk):
    refs = list(refs)
    q_ref = refs.pop(0)
    if with_lat:
        kl_ref, vl_ref = refs.pop(0), refs.pop(0)
    kc_ref, vc_ref = refs.pop(0), refs.pop(0)
    if diff:
        lam_ref, sub_ref = refs.pop(0), refs.pop(0)
    o_ref, m_sc, l_sc, acc_sc = refs
    tq = q_ref.shape[0]
    lane = lax.broadcasted_iota(jnp.int32, (tq, LANES), 1)
    lo_half = lane < 64

    def tile(k_ref, v_ref, rows):
        nkb = (rows.stop - rows.start if isinstance(rows, slice) else rows.size) // LANES
        for g, (kb, vb, vh, ai) in enumerate(heads):
            q = q_ref[:, g * LANES:(g + 1) * LANES]
            k = k_ref[rows, kb * LANES:(kb + 1) * LANES].astype(BF16)
            s = _dot_nt(q, k)
            blocks = [s[:, c * LANES:(c + 1) * LANES] for c in range(nkb)]
            mx = blocks[0]
            for blk in blocks[1:]:
                mx = jnp.maximum(mx, blk)
            m_prev = m_sc[g]
            m_new = jnp.maximum(m_prev, jnp.max(mx, axis=-1, keepdims=True))
            alpha = jnp.exp2(m_prev - m_new)
            ps = [jnp.exp2(blk - m_new) for blk in blocks]
            lsum = ps[0]
            for pb in ps[1:]:
                lsum = lsum + pb
            l_sc[g] = alpha * l_sc[g] + lsum
            m_sc[g] = m_new
            p = jnp.concatenate([pb.astype(BF16) for pb in ps], axis=1)
            v = v_ref[rows, vb * LANES:(vb + 1) * LANES].astype(BF16)
            pv = _dot(p, v)
            old = acc_sc[ai, :, vb * LANES:(vb + 1) * LANES]
            mine = lo_half if vh == 0 else jnp.logical_not(lo_half)
            acc_sc[ai, :, vb * LANES:(vb + 1) * LANES] = jnp.where(mine, alpha * old + pv, old)

    m_sc[...] = jnp.full(m_sc.shape, NEG, F32)
    l_sc[...] = jnp.zeros(l_sc.shape, F32)
    acc_sc[...] = jnp.zeros(acc_sc.shape, F32)
    tile(kc_ref, vc_ref, slice(0, kc_ref.shape[0]))

    if with_lat:
        def kv_step(j, carry):
            tile(kl_ref, vl_ref, pl.ds(pl.multiple_of(j * tk, tk), tk))
            return carry

        lax.fori_loop(0, kl_ref.shape[0] // tk, kv_step, 0, unroll=4)

    def inv_l(g):
        return 1.0 / jnp.sum(l_sc[g], axis=-1, keepdims=True)

    for c in range(2):
        if not diff:
            o = jnp.where(lo_half, acc_sc[0, :, c * LANES:(c + 1) * LANES] * inv_l(2 * c),
                          acc_sc[0, :, c * LANES:(c + 1) * LANES] * inv_l(2 * c + 1))
        else:
            o1 = jnp.where(lo_half, acc_sc[0, :, c * LANES:(c + 1) * LANES] * inv_l(4 * c),
                           acc_sc[0, :, c * LANES:(c + 1) * LANES] * inv_l(4 * c + 2))
            o2 = jnp.where(lo_half, acc_sc[1, :, c * LANES:(c + 1) * LANES] * inv_l(4 * c + 1),
                           acc_sc[1, :, c * LANES:(c + 1) * LANES] * inv_l(4 * c + 3))
            o = o1 - lam_ref[...] * o2
            sq = o * o
            ms_lo = jnp.sum(jnp.where(lo_half, sq, 0.0), axis=-1, keepdims=True) * (1.0 / 64)
            ms_hi = jnp.sum(jnp.where(lo_half, 0.0, sq), axis=-1, keepdims=True) * (1.0 / 64)
            rs = jnp.where(lo_half, lax.rsqrt(ms_lo + RMS_EPS), lax.rsqrt(ms_hi + RMS_EPS))
            o = o * rs * sub_ref[:, c * LANES:(c + 1) * LANES] * (1.0 - lam_init)
        o_ref[:, c * LANES:(c + 1) * LANES] = o


def _flash(q, k, v, *, nb, s_len, l_len, heads, kcol=0, vcol=0, kw=None, vw=256,
           with_lat=True, diff=False, lam=None, subln=None, lam_init=0.0, name="flash"):
    nh = len(heads)
    n_acc = 2 if diff else 1
    ctx0 = nb * s_len // l_len
    tk = min(TK, s_len)
    if with_lat:
        tq = min(TQ, s_len)
        grid = (nb, s_len // tq)
        rows = nb * s_len
        qmap = lambda b, i: (b * (s_len // tq) + i, 0)
    else:
        tq = l_len
        grid = (nb, 1)
        rows = nb * l_len
        qmap = lambda b, i: (ctx0 + b, 0)
    in_specs = [pl.BlockSpec((tq, nh * LANES), qmap)]
    args = [q]
    if with_lat:
        in_specs += [pl.BlockSpec((s_len, kw), lambda b, i: (b, kcol)),
                     pl.BlockSpec((s_len, vw), lambda b, i: (b, vcol))]
        args += [k, v]
    in_specs += [pl.BlockSpec((l_len, kw), lambda b, i: (ctx0 + b, kcol)),
                 pl.BlockSpec((l_len, vw), lambda b, i: (ctx0 + b, vcol))]
    args += [k, v]
    if diff:
        in_specs += [pl.BlockSpec((1, LANES), lambda b, i: (0, 0)),
                     pl.BlockSpec((1, 256), lambda b, i: (0, 0))]
        args += [lam, subln]
    return pl.pallas_call(
        functools.partial(_flash_kernel, heads=heads, n_acc=n_acc, with_lat=with_lat,
                          diff=diff, lam_init=lam_init, tk=tk),
        grid=grid,
        in_specs=in_specs,
        out_specs=pl.BlockSpec((tq, 256), qmap if with_lat else (lambda b, i: (b, 0))),
        out_shape=jax.ShapeDtypeStruct((rows, 256), F32),
        scratch_shapes=[pltpu.VMEM((nh, tq, LANES), F32), pltpu.VMEM((nh, tq, LANES), F32),
                        pltpu.VMEM((n_acc, tq, 256), F32)],
        compiler_params=_cparams(("parallel", "parallel")),
        name=name,
    )(*args)


MLA_HEADSPEC = tuple((h, h // 2, h % 2, 0) for h in range(MLA_HEADS))
NA_HEADSPEC = tuple((h // 2, h // 2, h % 2, 0) for h in range(NA_HEADS))
DIFF_HEADSPEC = tuple((g // 4, g // 4, (g // 2) % 2, g % 2) for g in range(2 * DIFF_HEADS))


def _na_kernel(q_ref, k_ref, v_ref, kc_ref, vc_ref, bias_ref, o_ref, *, rows):
    rb = pl.program_id(1)
    k0 = jnp.clip(NA_QROWS * rb - NA_KH // 2, 0, rows - NA_KROWS)
    start = pl.multiple_of(k0 * GRID_W, GRID_W)
    nk = NA_KROWS * GRID_W
    kw = k_ref[pl.ds(start, nk), :].astype(BF16)
    vw = v_ref[pl.ds(start, nk), :].astype(BF16)
    kc = kc_ref[...].astype(BF16)
    vc = vc_ref[...].astype(BF16)
    tq = q_ref.shape[0]
    lane = lax.broadcasted_iota(jnp.int32, (tq, LANES), 1)
    lo_half = lane < 64
    for c in range(2):
        o_c = jnp.zeros((tq, LANES), F32)
        for half in range(2):
            h = 2 * c + half
            q = q_ref[:, h * LANES:(h + 1) * LANES]
            s_loc = _dot_nt(q, kw[:, c * LANES:(c + 1) * LANES])
            s_ctx = _dot_nt(q, kc[:, c * LANES:(c + 1) * LANES])
            blocks = [s_loc[:, i * LANES:(i + 1) * LANES] + bias_ref[0, h, :, i * LANES:(i + 1) * LANES]
                      for i in range(nk // LANES)]
            blocks += [s_ctx[:, i * LANES:(i + 1) * LANES] for i in range(kc.shape[0] // LANES)]
            mx = blocks[0]
            for blk in blocks[1:]:
                mx = jnp.maximum(mx, blk)
            m = jnp.max(mx, axis=-1, keepdims=True)
            ps = [jnp.exp2(blk - m) for blk in blocks]
            lsum = ps[0]
            for pb in ps[1:]:
                lsum = lsum + pb
            l = jnp.sum(lsum, axis=-1, keepdims=True)
            n_loc = nk // LANES
            p_loc = jnp.concatenate([pb.astype(BF16) for pb in ps[:n_loc]], axis=1)
            p_ctx = jnp.concatenate([pb.astype(BF16) for pb in ps[n_loc:]], axis=1)
            pv = (_dot(p_loc, vw[:, c * LANES:(c + 1) * LANES]) + _dot(p_ctx, vc[:, c * LANES:(c + 1) * LANES]))
            mine = lo_half if half == 0 else jnp.logical_not(lo_half)
            o_c = jnp.where(mine, pv * (1.0 / l), o_c)
        o_ref[:, c * LANES:(c + 1) * LANES] = o_c


def _na_attention(qna, kna, vna, bias, *, layer, nb, s_len, l_len):
    rows = s_len // GRID_W
    tq = NA_QROWS * GRID_W
    nrb = rows // NA_QROWS
    ctx0 = nb * s_len // l_len

    def variant(rb):
        return jnp.where(rb == 0, 0, jnp.where(rb == nrb - 1, 2, 1))

    return pl.pallas_call(
        functools.partial(_na_kernel, rows=rows),
        grid=(nb, nrb),
        in_specs=[pl.BlockSpec((tq, NA_HEADS * LANES), lambda b, r: (b * nrb + r, 0)),
                  pl.BlockSpec((s_len, 256), lambda b, r: (b, 0)),
                  pl.BlockSpec((s_len, 256), lambda b, r: (b, 0)),
                  pl.BlockSpec((l_len, 256), lambda b, r: (ctx0 + b, 0)),
                  pl.BlockSpec((l_len, 256), lambda b, r: (ctx0 + b, 0)),
                  pl.BlockSpec((1, NA_HEADS, tq, NA_KROWS * GRID_W), lambda b, r: (variant(r), layer, 0, 0))],
        out_specs=pl.BlockSpec((tq, 256), lambda b, r: (b * nrb + r, 0)),
        out_shape=jax.ShapeDtypeStruct((nb * s_len, 256), F32),
        compiler_params=_cparams(("parallel", "arbitrary"), 56),
        name="na_attn",
    )(qna, kna, vna, kna, vna, bias)


def _na_bias_tables(rpb, rows):
    a = np.arange(NA_QROWS)
    qc = np.arange(GRID_W)
    kr_rel = np.arange(NA_KROWS)
    kc = np.arange(GRID_W)
    col0 = np.clip(qc - NA_KW // 2, 0, GRID_W - NA_KW)
    col_valid = (kc[None, :] >= col0[:, None]) & (kc[None, :] < col0[:, None] + NA_KW)
    col_off = np.clip(kc[None, :] - qc[:, None] + (NA_KW - 1), 0, 2 * NA_KW - 2)
    oh_col = (col_off[:, :, None] == np.arange(2 * NA_KW - 1)).astype(np.float32)
    big = 10 ** 6
    out = []
    for r_blk, k0, rows_eff in ((0, 0, big), (NA_QROWS, NA_KH // 2, big), (rows - NA_QROWS, rows - NA_KROWS, rows)):
        qr = r_blk + a
        r0 = np.clip(qr - NA_KH // 2, 0, rows_eff - NA_KH)
        kr = k0 + kr_rel
        row_valid = (kr[None, :] >= r0[:, None]) & (kr[None, :] < r0[:, None] + NA_KH)
        row_off = np.clip(kr[None, :] - qr[:, None] + (NA_KH - 1), 0, 2 * NA_KH - 2)
        oh_row = (row_off[:, :, None] == np.arange(2 * NA_KH - 1)).astype(np.float32)
        b = jnp.einsum('akr,hrc,qlc->haqkl', oh_row, rpb, oh_col, precision=lax.Precision.HIGHEST)
        valid = row_valid[:, None, :, None] & col_valid[None, :, None, :]
        b = jnp.where(valid[None], b * LOG2E, NEG)
        out.append(b.reshape(rpb.shape[0], NA_QROWS * GRID_W, NA_KROWS * GRID_W))
    return jnp.stack(out)


def _s5_kernel(*refs, nb):
    uf_refs, ub_refs = refs[:nb], refs[nb:2 * nb]
    (bf_ref, bb_ref, a_ref, cf_ref, cb_ref, d_ref, yf_ref, yb_ref,
     uf_sc, ub_sc, buf_sc, bub_sc, hf_sc, hb_sc) = refs[2 * nb:]
    j = pl.program_id(0)
    tc = uf_refs[0].shape[0]
    half = S5_GROUPS * S5_STATE

    @pl.when(j == 0)
    def _():
        hf_sc[...] = jnp.zeros(hf_sc.shape, F32)
        hb_sc[...] = jnp.zeros(hb_sc.shape, F32)

    for b in range(nb):
        for hv in range(2):
            uf_sc[hv, pl.ds(b, tc, stride=nb), :] = uf_refs[b][:, hv * LANES:(hv + 1) * LANES]
            ub_sc[hv, pl.ds(b, tc, stride=nb), :] = ub_refs[b][:, hv * LANES:(hv + 1) * LANES]

    def scan(u_sc, b_ref, c_ref, y_ref, bu_sc, h_sc, d, reverse):
        u = jnp.concatenate([u_sc[0], u_sc[1]], axis=1)
        bu_sc[...] = _dot(u.astype(BF16), b_ref[...])
        ar = jnp.broadcast_to(a_ref[d, 0:1, :], (nb, half))
        ai = jnp.broadcast_to(a_ref[d, 1:2, :], (nb, half))

        def step(t, carry):
            hr, hi = carry
            tt = (tc - 1 - t) if reverse else t
            row = pl.multiple_of(tt * nb, nb)
            nr = ar * hr - ai * hi + bu_sc[pl.ds(row, nb), 0:half]
            ni = ar * hi + ai * hr + bu_sc[pl.ds(row, nb), half:2 * half]
            bu_sc[pl.ds(row, nb), 0:half] = nr
            bu_sc[pl.ds(row, nb), half:2 * half] = ni
            return nr, ni

        hr, hi = lax.fori_loop(0, tc, step, (h_sc[:, 0:half], h_sc[:, half:2 * half]), unroll=True)
        h_sc[:, 0:half] = hr
        h_sc[:, half:2 * half] = hi
        y = _dot(bu_sc[...].astype(BF16), c_ref[...])
        y_ref[...] = y if reverse else y + d_ref[...] * u

    scan(uf_sc, bf_ref, cf_ref, yf_ref, buf_sc, hf_sc, 0, False)
    scan(ub_sc, bb_ref, cb_ref, yb_ref, bub_sc, hb_sc, 1, True)


def _s5_scan(z, bmat, avec, cmat, d_skip, *, nb, s_len, l_len):
    tc = S5_TC
    cr = tc * nb
    nctx, nlat = l_len // tc, s_len // tc
    nchunk = nctx + nlat
    half = S5_GROUPS * S5_STATE
    n_lat = nb * s_len

    def bwd(j):
        return jnp.where(j < nctx, nctx - 1 - j, nchunk - 1 - (j - nctx))

    def u_spec(b, order):
        def row_block(j):
            c = order(j)
            return jnp.where(c < nctx, (n_lat + b * l_len) // tc + c, (b * s_len) // tc + c - nctx)
        return pl.BlockSpec((tc, 256), lambda j: (row_block(j), 0))

    const = lambda shape: pl.BlockSpec(shape, lambda j: (0,) * len(shape))
    return pl.pallas_call(
        functools.partial(_s5_kernel, nb=nb),
        grid=(nchunk,),
        in_specs=[u_spec(b, lambda j: j) for b in range(nb)] + [u_spec(b, bwd) for b in range(nb)]
                 + [const((256, 2 * half)), const((256, 2 * half)), const((2, 2, half)),
                    const((2 * half, 256)), const((2 * half, 256)), const((1, 256))],
        out_specs=[pl.BlockSpec((cr, 256), lambda j: (j, 0)),
                   pl.BlockSpec((cr, 256), lambda j: (bwd(j), 0))],
        out_shape=[jax.ShapeDtypeStruct((nchunk * cr, 256), F32)] * 2,
        scratch_shapes=[pltpu.VMEM((2, cr, LANES), F32), pltpu.VMEM((2, cr, LANES), F32),
                        pltpu.VMEM((cr, 2 * half), F32), pltpu.VMEM((cr, 2 * half), F32),
                        pltpu.VMEM((nb, 2 * half), F32), pltpu.VMEM((nb, 2 * half), F32)],
        compiler_params=_cparams(("arbitrary",)),
        name="s5_scan",
    )(*([z] * (2 * nb)), bmat[0], bmat[1], avec, cmat[0], cmat[1], d_skip)


def _s5_finish_kernel(yf_ref, yb_ref, w_ref, o_ref, r_sc):
    nb, steps = o_ref.shape[0], o_ref.shape[1]
    r = _dot((yf_ref[...] + yb_ref[...]).astype(BF16), w_ref[...])
    o = r[:, :BRANCH_W] * jax.nn.sigmoid(r[:, BRANCH_W:])
    for hv in range(2):
        r_sc[hv] = o[:, hv * LANES:(hv + 1) * LANES]
    for b in range(nb):
        for hv in range(2):
            o_ref[b, :, hv * LANES:(hv + 1) * LANES] = r_sc[hv, pl.ds(b, steps, stride=nb), :]


def _s5_finish(yf, yb, w_glu, *, nb, t0, t_len):
    steps = TM // nb
    tile0 = t0 // steps
    spec = pl.BlockSpec((TM, 256), lambda i: (tile0 + i, 0))
    out = pl.pallas_call(
        _s5_finish_kernel,
        grid=(t_len // steps,),
        in_specs=[spec, spec, pl.BlockSpec((256, 512), lambda i: (0, 0))],
        out_specs=pl.BlockSpec((nb, steps, 256), lambda i: (0, i, 0)),
        out_shape=jax.ShapeDtypeStruct((nb, t_len, 256), F32),
        scratch_shapes=[pltpu.VMEM((2, TM, LANES), F32)],
        compiler_params=_cparams(("parallel",)),
        name="s5_glu",
    )(yf, yb, w_glu)
    return out.reshape(nb * t_len, 256)


def _s5_params(lam_re, lam_im, log_dt, b_re, b_im, c_re, c_im):
    lam = lax.complex(lam_re, lam_im)
    dt = jnp.exp(log_dt)[..., None]
    a_bar = jnp.exp(lam * dt)
    b_bar = ((a_bar - 1.0) / lam)[..., None] * lax.complex(b_re, b_im)
    eye = jnp.eye(S5_GROUPS, dtype=F32)
    g, p, ch = S5_GROUPS, S5_STATE, S5_GROUP_CH

    def bdiag_in(m):
        return jnp.einsum('gh,dgpc->dgchp', eye, m).reshape(2, g * ch, g * p)

    def bdiag_out(m):
        return jnp.einsum('gh,dgcp->dgphc', eye, m).reshape(2, g * p, g * ch)

    bmat = jnp.concatenate([bdiag_in(jnp.real(b_bar)), bdiag_in(jnp.imag(b_bar))], axis=2).astype(BF16)
    cmat = jnp.concatenate([bdiag_out(c_re), -bdiag_out(c_im)], axis=1).astype(BF16)
    avec = jnp.stack([jnp.real(a_bar).reshape(2, g * p), jnp.imag(a_bar).reshape(2, g * p)], axis=1)
    return bmat, avec, cmat


def _merge_kernel(*refs, n_lat_tiles, has_ctx):
    x_ref, mod_ref = refs[0], refs[1]
    n_br = 8 if has_ctx else 4
    br_refs = refs[2:2 + n_br]
    wg_ref, bg_ref, wb_ref, wo_ref, g_ref, b_ref, x1_ref, h2_ref = refs[2 + n_br:]
    d = D_MODEL
    m = mod_ref[0]
    x = x_ref[...]
    hb = (x * (1.0 + m[1:2]) + m[0:1]).astype(BF16)
    is_ctx = pl.program_id(0) >= n_lat_tiles
    acc = None
    for i in range(4):
        if has_ctx:
            o = jnp.where(is_ctx, br_refs[2 * i + 1][...], br_refs[2 * i][...])
        else:
            o = br_refs[i][...]
        gate = jax.nn.sigmoid(_dot(hb, wg_ref[0, :, i * d:(i + 1) * d]) + bg_ref[:, i * d:(i + 1) * d])
        term = gate * _dot(o.astype(BF16), wb_ref[0, i])
        acc = term if acc is None else acc + term
    y = _dot(acc.astype(BF16), wo_ref[0])
    x1 = _layer_norm(ALPHA * x + m[2:3] * y, g_ref[...], b_ref[...])
    x1_ref[...] = x1
    h2_ref[...] = x1 * (1.0 + m[4:5]) + m[3:4]


def _merge(xa, modl, outs_lat, outs_ctx, wg, bg, wb, wo, g, b, seg, n, n_lat_tiles, layer):
    d = D_MODEL
    tok = pl.BlockSpec((TM, d), lambda i: (i, 0))
    lat = pl.BlockSpec((TM, 256), lambda i: (jnp.minimum(i, n_lat_tiles - 1), 0))
    ctx = pl.BlockSpec((TM, 256), lambda i: (jnp.maximum(i - n_lat_tiles, 0), 0))

    def const(a):
        return pl.BlockSpec(a.shape, lambda i: (0,) * a.ndim)

    def stacked(a):
        return pl.BlockSpec((1,) + a.shape[1:], lambda i: (layer,) + (0,) * (a.ndim - 1))
    has_ctx = outs_ctx is not None
    if has_ctx:
        branches = [a for pair in zip(outs_lat, outs_ctx) for a in pair]
        br_specs = [lat, ctx] * 4
    else:
        branches, br_specs = list(outs_lat), [lat] * 4
    return pl.pallas_call(
        functools.partial(_merge_kernel, n_lat_tiles=n_lat_tiles, has_ctx=has_ctx),
        grid=(n // TM,),
        in_specs=[tok, pl.BlockSpec((1, 6, d), lambda i: (seg(i), 0, 0))] + br_specs
                 + [stacked(wg), const(bg), stacked(wb), stacked(wo), const(g), const(b)],
        out_specs=[tok, tok],
        out_shape=[jax.ShapeDtypeStruct((n, d), F32)] * 2,
        compiler_params=_cparams(("parallel",), 56),
        name="merge_ln1",
    )(xa, modl, *branches, wg, bg, wb, wo, g, b)


def _start_row_gather(src_hbm, dst, idx_ref, base, n_rows, sem):
    def body(r, c):
        pltpu.make_async_copy(src_hbm.at[pl.ds(idx_ref[base + r], 1)], dst.at[pl.ds(r, 1)], sem).start()
        return c

    lax.fori_loop(0, n_rows, body, 0, unroll=8)


def _wait_row_gather(src_hbm, dst, sem):
    pltpu.make_async_copy(src_hbm.at[pl.ds(0, dst.shape[0])], dst, sem).wait()


def _swiglu_block(xb, w1_ref, w3_ref, w2_ref, o_ref, between=None):
    lead = (0,) * (len(w1_ref.shape) - 2)
    n_chunks = D_FF // FF_CHUNK
    for c in range(n_chunks):
        cs = slice(c * FF_CHUNK, (c + 1) * FF_CHUNK)
        a = _dot(xb, w1_ref[lead + (slice(None), cs)])
        b = _dot(xb, w3_ref[lead + (slice(None), cs)])
        g = (a * jax.nn.sigmoid(a) * b).astype(BF16)
        y = _dot(g, w2_ref[lead + (cs, slice(None))])
        if c == 0:
            o_ref[...] = y
        else:
            o_ref[...] += y
        if between is not None:
            between(c, n_chunks)


def _ffn_kernel(h_ref, x_ref, mod_ref, w1_ref, w3_ref, w2_ref, g_ref, b_ref, o_ref, acc_sc):
    _swiglu_block(h_ref[...].astype(BF16), w1_ref, w3_ref, w2_ref, acc_sc)
    m = mod_ref[0]
    o_ref[...] = _layer_norm(ALPHA * x_ref[...] + m[5:6] * acc_sc[...], g_ref[...], b_ref[...])


def _moe_ffn_kernel(be_ref, tok_ref, nused_ref, h_hbm, w1_ref, w3_ref, w2_ref, o_ref, xbuf, sem):
    i = pl.program_id(0)
    last = pl.num_programs(0) - 1
    slot = i % 2
    nxt_base = jnp.minimum(i + 1, last) * MOE_BLK
    nxt_buf, nxt_sem = xbuf.at[1 - slot], sem.at[1 - slot]

    @pl.when(i == 0)
    def _():
        _start_row_gather(h_hbm, xbuf.at[0], tok_ref, 0, MOE_BLK, sem.at[0])

    _wait_row_gather(h_hbm, xbuf.at[slot], sem.at[slot])

    def start_piece(c, n_chunks):
        per = -(-MOE_BLK // n_chunks)
        for r in range(c * per, min((c + 1) * per, MOE_BLK)):
            pltpu.make_async_copy(h_hbm.at[pl.ds(tok_ref[nxt_base + r], 1)], nxt_buf.at[pl.ds(r, 1)],
                                  nxt_sem).start()

    @pl.when(i < nused_ref[0])
    def _():
        _swiglu_block(xbuf[slot].astype(BF16), w1_ref, w3_ref, w2_ref, o_ref, between=start_piece)

    @pl.when(i >= nused_ref[0])
    def _():
        o_ref[...] = jnp.zeros(o_ref.shape, F32)
        _start_row_gather(h_hbm, nxt_buf, tok_ref, nxt_base, MOE_BLK, nxt_sem)

    @pl.when(i == last)
    def _():
        _wait_row_gather(h_hbm, nxt_buf, nxt_sem)


def _moe_ffn(h, block_e, buf_tok, n_used, w1, w3, w2, layer):
    d = h.shape[1]
    n_rows = buf_tok.shape[0]
    wspec = lambda shape: pl.BlockSpec(shape, lambda i, be, tok, nu: (layer, be[i], 0, 0),
                                       pipeline_mode=pl.Buffered(1))
    grid_spec = pltpu.PrefetchScalarGridSpec(
        num_scalar_prefetch=3,
        grid=(n_rows // MOE_BLK,),
        in_specs=[pl.BlockSpec(memory_space=pl.ANY), wspec((1, 1, d, D_FF)), wspec((1, 1, d, D_FF)),
                  wspec((1, 1, D_FF, d))],
        out_specs=pl.BlockSpec((MOE_BLK, d), lambda i, be, tok, nu: (i, 0)),
        scratch_shapes=[pltpu.VMEM((2, MOE_BLK, d), F32), pltpu.SemaphoreType.DMA((2,))],
    )
    return pl.pallas_call(
        _moe_ffn_kernel,
        grid_spec=grid_spec,
        out_shape=jax.ShapeDtypeStruct((n_rows, d), F32),
        compiler_params=_cparams(("arbitrary",), 56),
        name="moe_swiglu",
    )(block_e, buf_tok, n_used, h, w1, w3, w2)


def _ffn(h2, x1, modl, w1, w3, w2, g, b, seg, layer):
    n, d = h2.shape
    tok = pl.BlockSpec((TM, d), lambda i: (i, 0))
    vec = pl.BlockSpec((1, d), lambda i: (0, 0))
    wspec = lambda shape: pl.BlockSpec(shape, lambda i: (layer, 0, 0), pipeline_mode=pl.Buffered(1))
    return pl.pallas_call(
        _ffn_kernel,
        grid=(n // TM,),
        in_specs=[tok, tok, pl.BlockSpec((1, 6, d), lambda i: (seg(i), 0, 0)),
                  wspec((1, d, D_FF)), wspec((1, d, D_FF)), wspec((1, D_FF, d)), vec, vec],
        out_specs=tok,
        out_shape=jax.ShapeDtypeStruct((n, d), F32),
        scratch_shapes=[pltpu.VMEM((TM, d), F32)],
        compiler_params=_cparams(("parallel",), 56),
        name="swiglu_ln2",
    )(h2, x1, modl, w1, w3, w2, g, b)


def _router_kernel(h_ref, w_ref, o_ref):
    logits = jnp.dot(h_ref[...], w_ref[...], preferred_element_type=F32, precision=lax.Precision.HIGHEST)
    lane = lax.broadcasted_iota(jnp.int32, logits.shape, 1)
    lg = jnp.where(lane < N_EXPERTS, logits, NEG)
    v1 = jnp.max(lg, axis=-1, keepdims=True)
    i1 = jnp.min(jnp.where(lg == v1, lane, LANES), axis=-1, keepdims=True)
    lg2 = jnp.where(lane == i1, NEG, lg)
    v2 = jnp.max(lg2, axis=-1, keepdims=True)
    i2 = jnp.min(jnp.where(lg2 == v2, lane, LANES), axis=-1, keepdims=True)
    e = jnp.exp(v2 - v1)
    g1 = 1.0 / (1.0 + e)
    g2 = e / (1.0 + e)
    out = jnp.where(lane == 0, i1.astype(F32), jnp.where(lane == 1, i2.astype(F32),
                    jnp.where(lane == 2, g1, jnp.where(lane == 3, g2, 0.0))))
    o_ref[...] = out


def _router(h2, w_router_pad):
    n, d = h2.shape
    return pl.pallas_call(
        _router_kernel,
        grid=(n // TM,),
        in_specs=[pl.BlockSpec((TM, d), lambda i: (i, 0)), pl.BlockSpec((d, LANES), lambda i: (0, 0))],
        out_specs=pl.BlockSpec((TM, LANES), lambda i: (i, 0)),
        out_shape=jax.ShapeDtypeStruct((n, LANES), F32),
        compiler_params=_cparams(("parallel",)),
        name="router",
    )(h2, w_router_pad)


def _moe_combine_kernel(dest_ref, x_ref, mod_ref, gt_ref, y_hbm, g_ref, b_ref, o_ref, ybuf, sem):
    i = pl.program_id(0)
    last = pl.num_programs(0) - 1
    slot = i % 2
    nxt_base = jnp.minimum(i + 1, last) * 2 * TM
    nxt_buf, nxt_sem = ybuf.at[1 - slot], sem.at[1 - slot]

    @pl.when(i == 0)
    def _():
        _start_row_gather(y_hbm, ybuf.at[0], dest_ref, 0, 2 * TM, sem.at[0])

    for r in range(2 * TM):
        pltpu.make_async_copy(y_hbm.at[pl.ds(dest_ref[nxt_base + r], 1)], nxt_buf.at[pl.ds(r, 1)], nxt_sem).start()

    _wait_row_gather(y_hbm, ybuf.at[slot], sem.at[slot])
    gt = gt_ref[...]
    f = gt[:, 2:3] * ybuf[slot, 0:TM, :] + gt[:, 3:4] * ybuf[slot, TM:2 * TM, :]
    m = mod_ref[0]
    o_ref[...] = _layer_norm(ALPHA * x_ref[...] + m[5:6] * f, g_ref[...], b_ref[...])

    @pl.when(i == last)
    def _():
        _wait_row_gather(y_hbm, nxt_buf, nxt_sem)


def _moe_combine(x1, modl, y_rows, dest_tiles, gates, g, b, seg):
    n, d = x1.shape
    tok = pl.BlockSpec((TM, d), lambda i, dst: (i, 0))
    vec = pl.BlockSpec((1, d), lambda i, dst: (0, 0))
    grid_spec = pltpu.PrefetchScalarGridSpec(
        num_scalar_prefetch=1,
        grid=(n // TM,),
        in_specs=[tok, pl.BlockSpec((1, 6, d), lambda i, dst: (seg(i), 0, 0)),
                  pl.BlockSpec((TM, LANES), lambda i, dst: (i, 0)), pl.BlockSpec(memory_space=pl.ANY), vec, vec],
        out_specs=tok,
        scratch_shapes=[pltpu.VMEM((2, 2 * TM, d), F32), pltpu.SemaphoreType.DMA((2,))],
    )
    return pl.pallas_call(
        _moe_combine_kernel,
        grid_spec=grid_spec,
        out_shape=jax.ShapeDtypeStruct((n, d), F32),
        compiler_params=_cparams(("arbitrary",)),
        name="moe_combine_ln2",
    )(dest_tiles, x1, modl, gates, y_rows, g, b)


def _rope_tables(s_len, width, rope_lane, extra_rows):
    t = jnp.arange(s_len)
    lane = np.arange(width)
    inv = ROPE_BASE ** (-(jnp.asarray(lane % 8, F32)) / 8.0)
    pos = jnp.where((lane & 16) == 0, (t // GRID_W)[:, None], (t % GRID_W)[:, None]).astype(F32)
    ang = pos * inv[None, :]
    cos = jnp.where(rope_lane[None, :], jnp.cos(ang), 1.0)
    sin = jnp.where(rope_lane[None, :], jnp.where((lane & 8) == 0, -jnp.sin(ang), jnp.sin(ang)), 0.0)
    cos = jnp.concatenate([cos, jnp.ones((extra_rows, width), F32)])
    sin = jnp.concatenate([sin, jnp.zeros((extra_rows, width), F32)])
    return cos, sin


def _pad_heads(w, n_heads, width):
    k = w.shape[0]
    w = w.reshape(k, n_heads, -1)
    return jnp.pad(w, ((0, 0), (0, 0), (0, width - w.shape[-1]))).reshape(k, n_heads * width)


def _route(top, n_tok):
    e_flat = top[:, 0:2].astype(jnp.int32).reshape(-1)
    n_assign = 2 * n_tok
    onehot = (e_flat[:, None] == jnp.arange(N_EXPERTS)[None, :]).astype(jnp.int32)
    csum = jnp.cumsum(onehot, axis=0)
    counts = csum[-1]
    padded = (counts + MOE_BLK - 1) // MOE_BLK * MOE_BLK
    pad_end = jnp.cumsum(padded)
    pad_start = pad_end - padded
    dest = jnp.sum(onehot * (csum - 1 + pad_start[None, :]), axis=1)
    n_rows = -(-n_assign // MOE_BLK) * MOE_BLK + N_EXPERTS * MOE_BLK
    buf_tok = jnp.zeros((n_rows,), jnp.int32).at[dest].set(jnp.arange(n_assign, dtype=jnp.int32) // 2,
                                                           unique_indices=True)
    block_e = jnp.minimum(jnp.searchsorted(pad_end, jnp.arange(n_rows // MOE_BLK) * MOE_BLK, side='right'),
                          N_EXPERTS - 1).astype(jnp.int32)
    n_used = (pad_end[-1:] // MOE_BLK).astype(jnp.int32)
    dest_tiles = dest.astype(jnp.int32).reshape(n_tok // TM, TM, 2).transpose(0, 2, 1).reshape(-1)
    return buf_tok, block_e, n_used, dest_tiles


def kernel(x, c, ctx, c_ctx, w_ada, b_ada, w_in, na_rpb, mla_q_norm, mla_kv_norm, mla_w_uq, mla_w_ukv,
           s5_lam_re, s5_lam_im, s5_log_dt, s5_b_re, s5_b_im, s5_c_re, s5_c_im, s5_d, s5_w_glu,
           diff_lam_q1, diff_lam_k1, diff_lam_q2, diff_lam_k2, diff_subln,
           w_branch, w_gate, b_gate, w_out, ln1_g, ln1_b, ln2_g, ln2_b,
           ffn_w1, ffn_w3, ffn_w2, moe_router, moe_w1, moe_w3, moe_w2):
    nb, s_len, d = x.shape
    l_len = ctx.shape[1]
    n_lat, n_ctx = nb * s_len, nb * l_len
    n_tot = n_lat + n_ctx
    rows = s_len // GRID_W
    assert d == D_MODEL and s_len % TM == 0 and n_ctx % TM == 0 and rows % NA_QROWS == 0 and rows >= NA_KROWS
    assert l_len % S5_TC == 0 and s_len % S5_TC == 0

    def seg(i):
        return jnp.minimum((i * TM) // s_len, nb)

    n_lat_tiles = n_lat // TM

    xa = jnp.concatenate([x.reshape(n_lat, d), ctx.reshape(n_ctx, d)], axis=0)
    nrow_mod = -(-(nb + 1) // 16) * 16
    cvec = jnp.zeros((nrow_mod, d), F32).at[:nb].set(c).at[nb].set(c_ctx)
    mod_all = _ada_all(cvec, w_ada, b_ada).reshape(DEPTH, nrow_mod, 6, d)

    lane512 = np.arange(512)
    q_rope = (lane512 % LANES >= MLA_NOPE) & (lane512 % LANES < MLA_NOPE + MLA_ROPE)
    cq, sq = _rope_tables(s_len, 512, q_rope, TM)
    ck, sk = _rope_tables(s_len, LANES, np.arange(LANES) < MLA_ROPE, TM)
    cd, sd = _rope_tables(s_len, 256, np.ones((256,), bool), TM)
    tabs = (cq, sq, ck, sk, cd, sd)
    place = np.zeros((LANES, MLA_HEADS * LANES), np.float32)
    for h in range(MLA_HEADS):
        place[np.arange(MLA_ROPE), h * LANES + MLA_NOPE + np.arange(MLA_ROPE)] = 1.0
    place = jnp.asarray(place, BF16)

    w_gate_b, w_branch_b, w_out_b = w_gate.astype(BF16), w_branch.astype(BF16), w_out.astype(BF16)
    ffn_b = (ffn_w1.astype(BF16), ffn_w3.astype(BF16), ffn_w2.astype(BF16))
    moe_b = (moe_w1.astype(BF16), moe_w3.astype(BF16), moe_w2.astype(BF16))
    bias_all = _na_bias_tables(na_rpb.reshape(DEPTH * NA_HEADS, 2 * NA_KH - 1, 2 * NA_KW - 1), rows)

    for layer in range(DEPTH):
        ctx_out = layer < DEPTH - 1
        lam_init = 0.8 - 0.6 * math.exp(-0.3 * layer)
        modl = mod_all[layer]

        wi = w_in[layer]
        w_in_pad = jnp.concatenate([wi[:, :1184], jnp.zeros((d, 96), F32), wi[:, 1184:]], axis=1).astype(BF16)
        wuq = _pad_heads(mla_w_uq[layer], MLA_HEADS, LANES).astype(BF16)
        wukv4 = mla_w_ukv[layer].reshape(-1, MLA_HEADS, MLA_NOPE + MLA_V)
        wuk = jnp.pad(wukv4[:, :, :MLA_NOPE], ((0, 0), (0, 0), (0, LANES - MLA_NOPE))).reshape(-1, MLA_HEADS * LANES)
        wuv = wukv4[:, :, MLA_NOPE:].reshape(-1, MLA_HEADS * MLA_V)
        wukv = jnp.concatenate([wuk, wuv], axis=1).astype(BF16)
        bmat, avec, cmat = _s5_params(s5_lam_re[layer], s5_lam_im[layer], s5_log_dt[layer], s5_b_re[layer],
                                      s5_b_im[layer], s5_c_re[layer], s5_c_im[layer])
        lam = (jnp.exp(jnp.sum(diff_lam_q1[layer] * diff_lam_k1[layer]))
               - jnp.exp(jnp.sum(diff_lam_q2[layer] * diff_lam_k2[layer])) + lam_init)
        lam_vec = jnp.full((1, LANES), lam, F32)
        subln = jnp.tile(diff_subln[layer], DIFF_HEADS).reshape(1, 256)

        qm, km, vm, qd, kd, vd, qna, kna, vna, u5 = _proj(
            xa, modl, w_in_pad, tabs, mla_q_norm[layer].reshape(1, -1), mla_kv_norm[layer].reshape(1, -1),
            wuq, wukv, place, nb=nb, s_len=s_len)
        common = dict(nb=nb, s_len=s_len, l_len=l_len)
        o_na = _na_attention(qna, kna, vna, bias_all, layer=layer, **common)
        o_mla = _flash(qm, km, vm, heads=MLA_HEADSPEC, kw=512, name="mla_attn", **common)
        o_diff = _flash(qd, kd, vd, heads=DIFF_HEADSPEC, kw=256, diff=True, lam=lam_vec,
                        subln=subln, lam_init=lam_init, name="diff_attn", **common)
        yf, yb = _s5_scan(u5, bmat, avec, cmat, s5_d[layer].reshape(1, 256), **common)
        w_glu = s5_w_glu[layer].astype(BF16)
        o_s5_lat = _s5_finish(yf, yb, w_glu, nb=nb, t0=l_len, t_len=s_len)
        if ctx_out:
            o_na_c = _flash(qna, kna, vna, heads=NA_HEADSPEC, kw=256, with_lat=False, name="na_ctx_attn", **common)
            o_mla_c = _flash(qm, km, vm, heads=MLA_HEADSPEC, kw=512, with_lat=False,
                             name="mla_ctx_attn", **common)
            o_diff_c = _flash(qd, kd, vd, heads=DIFF_HEADSPEC, kw=256, diff=True, lam=lam_vec,
                              subln=subln, lam_init=lam_init, with_lat=False, name="diff_ctx_attn", **common)
            o_s5_c = _s5_finish(yf, yb, w_glu, nb=nb, t0=0, t_len=l_len)
            outs_ctx = [o_na_c, o_mla_c, o_s5_c, o_diff_c]
            n_act = n_tot
        else:
            outs_ctx = None
            n_act = n_lat
        x1, h2 = _merge(xa, modl, [o_na, o_mla, o_s5_lat, o_diff], outs_ctx,
                        w_gate_b, b_gate[layer].reshape(1, -1), w_branch_b, w_out_b,
                        ln1_g[layer].reshape(1, d), ln1_b[layer].reshape(1, d), seg, n_act, n_lat_tiles, layer)

        jj = layer // 2
        g2, b2 = ln2_g[layer].reshape(1, d), ln2_b[layer].reshape(1, d)
        if layer % 2 == 0:
            xa = _ffn(h2, x1, modl, *ffn_b, g2, b2, seg, jj)
        else:
            wr = jnp.pad(moe_router[jj], ((0, 0), (0, LANES - N_EXPERTS)))
            top = _router(h2, wr)
            buf_tok, block_e, n_used, dest_tiles = _route(top, n_act)
            y_rows = _moe_ffn(h2, block_e, buf_tok, n_used, *moe_b, jj)
            xa = _moe_combine(x1, modl, y_rows, dest_tiles, top, g2, b2, seg)
    return xa[:n_lat].reshape(nb, s_len, d)
```

```python
import functools
import math

import jax
import jax.numpy as jnp
import numpy as np
from jax import lax
from jax.experimental import pallas as pl
from jax.experimental.pallas import tpu as pltpu

F32 = jnp.float32
BF16 = jnp.bfloat16

D_MODEL = 1024
DEPTH = 4
GRID_W = 64
BRANCH_W = 256
NA_HEADS = 4
NA_HD = 64
NA_KH = 8
NA_KW = 16
MLA_HEADS = 4
MLA_NOPE = 64
MLA_ROPE = 32
MLA_V = 64
S5_GROUP_CH = 16
S5_GROUPS = 16
S5_STATE = 64
DIFF_HEADS = 4
DIFF_HD = 32
D_FF = 3584
N_EXPERTS = 8
ROPE_BASE = 10000.0
LN_EPS = 1e-5
RMS_EPS = 1e-6
ALPHA = (2 * DEPTH) ** 0.25
NA_SCALE = NA_HD ** -0.5
MLA_SCALE = (MLA_NOPE + MLA_ROPE) ** -0.5
DIFF_SCALE = DIFF_HD ** -0.5
LOG2E = 1.4426950408889634

LANES = 128
ZW = 2304
Z_NAQ, Z_NAK, Z_NAV, Z_MQ, Z_MKV, Z_MKR, Z_S5, Z_DQ, Z_DK, Z_DV = (
    0, 256, 512, 768, 1024, 1152, 1280, 1536, 1792, 2048)
NEG = -1e30
TM = 512
TQ = 1024
TK = 256
NA_QROWS = 8
NA_KROWS = NA_QROWS + NA_KH
S5_TC = 128
FF_CHUNK = 512
MOE_BLK = 512


def _cparams(sem, vmem_mb=48):
    return pltpu.CompilerParams(dimension_semantics=sem, vmem_limit_bytes=vmem_mb << 20)


def _dot(a, b):
    return jnp.dot(a, b, preferred_element_type=F32)


def _dot_nt(a, b):
    return lax.dot_general(a, b, (((1,), (1,)), ((), ())), preferred_element_type=F32)


def _layer_norm(r, g, b):
    rc = r - jnp.mean(r, axis=-1, keepdims=True)
    var = jnp.mean(rc * rc, axis=-1, keepdims=True)
    return rc * lax.rsqrt(var + LN_EPS) * g + b


def _ada_kernel(c_ref, w_ref, b_ref, o_ref):
    c = c_ref[...]
    cond = c * jax.nn.sigmoid(c)
    o_ref[0] = _dot(cond.astype(BF16), w_ref[0].astype(BF16)) + b_ref[0]


def _ada_all(cvec, w_ada, b_ada):
    nrow = cvec.shape[0]
    d = D_MODEL
    return pl.pallas_call(
        _ada_kernel,
        grid=(DEPTH, 6),
        in_specs=[pl.BlockSpec((nrow, d), lambda l, n: (0, 0)),
                  pl.BlockSpec((1, d, d), lambda l, n: (l, 0, n)),
                  pl.BlockSpec((1, 1, d), lambda l, n: (l, 0, n))],
        out_specs=pl.BlockSpec((1, nrow, d), lambda l, n: (l, 0, n)),
        out_shape=jax.ShapeDtypeStruct((DEPTH, nrow, 6 * d), F32),
        compiler_params=_cparams(("parallel", "parallel")),
        name="ada_mod",
    )(cvec, w_ada, b_ada.reshape(DEPTH, 1, 6 * d))


def _swap8(z):
    w = z.shape[-1]
    lane = lax.broadcasted_iota(jnp.int32, z.shape, 1)
    up = pltpu.roll(z, w - 8, 1)
    dn = pltpu.roll(z, 8, 1)
    return jnp.where((lane & 8) == 0, up, dn)


def _rms(z, g):
    return z * lax.rsqrt(jnp.mean(z * z, axis=-1, keepdims=True) + RMS_EPS) * g


def _proj_kernel(x_ref, mod_ref, w_ref, cq_ref, sq_ref, ck_ref, sk_ref, cd_ref, sd_ref,
                 qn_ref, kvn_ref, wuq_ref, wukv_ref, place_ref,
                 qm_ref, km_ref, vm_ref, qd_ref, kd_ref, vd_ref, qna_ref, kna_ref, vna_ref, u_ref):
    m = mod_ref[0]
    h = x_ref[...] * (1.0 + m[1:2]) + m[0:1]
    z = _dot(h.astype(BF16), w_ref[...])
    kna_ref[...] = z[:, Z_NAK:Z_NAK + 256].astype(BF16)
    vna_ref[...] = z[:, Z_NAV:Z_NAV + 256].astype(BF16)
    u_ref[...] = z[:, Z_S5:Z_S5 + 256]
    aq = _rms(z[:, Z_MQ:Z_MQ + 256], qn_ref[...])
    q = _dot(aq.astype(BF16), wuq_ref[...])
    q = q * cq_ref[...] + _swap8(q) * sq_ref[...]
    qm_ref[...] = (q * (MLA_SCALE * LOG2E)).astype(BF16)
    akv = _rms(z[:, Z_MKV:Z_MKV + 128], kvn_ref[...])
    kv = _dot(akv.astype(BF16), wukv_ref[...])
    kr = z[:, Z_MKR:Z_MKR + 128]
    kr = kr * ck_ref[...] + _swap8(kr) * sk_ref[...]
    km_ref[...] = (kv[:, :4 * LANES] + _dot(kr.astype(BF16), place_ref[...])).astype(BF16)
    vm_ref[...] = kv[:, 4 * LANES:].astype(BF16)
    dq = z[:, Z_DQ:Z_DQ + 256]
    dq = (dq * cd_ref[...] + _swap8(dq) * sd_ref[...]) * (DIFF_SCALE * LOG2E)
    dk = z[:, Z_DK:Z_DK + 256]
    dk = dk * cd_ref[...] + _swap8(dk) * sd_ref[...]
    kd_ref[...] = dk.astype(BF16)
    vd_ref[...] = z[:, Z_DV:Z_DV + 256].astype(BF16)
    lane = lax.broadcasted_iota(jnp.int32, (dq.shape[0], LANES), 1)
    for g in range(2 * DIFF_HEADS):
        blk = dq[:, (g // 4) * LANES:(g // 4 + 1) * LANES]
        qd_ref[:, g * LANES:(g + 1) * LANES] = jnp.where((lane // DIFF_HD) == (g % 4), blk, 0.0).astype(BF16)
    naq = z[:, Z_NAQ:Z_NAQ + 256] * (NA_SCALE * LOG2E)
    for hd in range(NA_HEADS):
        blk = naq[:, (hd // 2) * LANES:(hd // 2 + 1) * LANES]
        qna_ref[:, hd * LANES:(hd + 1) * LANES] = jnp.where((lane // NA_HD) == (hd % 2), blk, 0.0).astype(BF16)


def _proj(xa, modl, w_in_pad, tabs, qn, kvn, wuq, wukv, place, *, nb, s_len):
    n, d = xa.shape
    cq, sq, ck, sk, cd, sd = tabs
    tps = s_len // TM
    n_lat_tiles = nb * tps

    def tile(i):
        return jnp.where(i < n_lat_tiles, (i % nb) * tps + i // nb, i)

    def seg(i):
        return jnp.minimum((tile(i) * TM) // s_len, nb)

    def tab_idx(i):
        return jnp.where(i < n_lat_tiles, i // nb, tps)

    def tspec(width):
        return pl.BlockSpec((TM, width), lambda i: (tab_idx(i), 0))

    def wspec(a):
        return pl.BlockSpec(a.shape, lambda i: (0,) * a.ndim)

    def ospec(width):
        return pl.BlockSpec((TM, width), lambda i: (tile(i), 0))

    widths = [512, 512, 256, 1024, 256, 256, 512, 256, 256]
    return pl.pallas_call(
        _proj_kernel,
        grid=(n // TM,),
        in_specs=[pl.BlockSpec((TM, d), lambda i: (tile(i), 0)),
                  pl.BlockSpec((1, 6, d), lambda i: (seg(i), 0, 0)),
                  wspec(w_in_pad),
                  tspec(512), tspec(512), tspec(128), tspec(128), tspec(256), tspec(256),
                  wspec(qn), wspec(kvn), wspec(wuq), wspec(wukv), wspec(place)],
        out_specs=[ospec(w) for w in widths] + [ospec(256)],
        out_shape=[jax.ShapeDtypeStruct((n, w), BF16) for w in widths] + [jax.ShapeDtypeStruct((n, 256), F32)],
        compiler_params=_cparams(("parallel",), 56),
        name="in_proj_prep",
    )(xa, modl, w_in_pad, cq, sq, ck, sk, cd, sd, qn, kvn, wuq, wukv, place)


def _flash_kernel(*refs, heads, n_acc, with_lat, diff, lam_init, tk):
    refs = list(refs)
    q_ref = refs.pop(0)
    if with_lat:
        kl_ref, vl_ref = refs.pop(0), refs.pop(0)
    kc_ref, vc_ref = refs.pop(0), refs.pop(0)
    if diff:
        lam_ref, sub_ref = refs.pop(0), refs.pop(0)
    o_ref, m_sc, l_sc, acc_sc = refs
    tq = q_ref.shape[0]
    lane = lax.broadcasted_iota(jnp.int32, (tq, LANES), 1)
    lo_half = lane < 64

    def tile(k_ref, v_ref, rows):
        nkb = (rows.stop - rows.start if isinstance(rows, slice) else rows.size) // LANES
        for g, (kb, vb, vh, ai) in enumerate(heads):
            q = q_ref[:, g * LANES:(g + 1) * LANES]
            k = k_ref[rows, kb * LANES:(kb + 1) * LANES].astype(BF16)
            s = _dot_nt(q, k)
            blocks = [s[:, c * LANES:(c + 1) * LANES] for c in range(nkb)]
            mx = blocks[0]
            for blk in blocks[1:]:
                mx = jnp.maximum(mx, blk)
            m_prev = m_sc[g]
            m_new = jnp.maximum(m_prev, jnp.max(mx, axis=-1, keepdims=True))
            alpha = jnp.exp2(m_prev - m_new)
            ps = [jnp.exp2(blk - m_new) for blk in blocks]
            lsum = ps[0]
            for pb in ps[1:]:
                lsum = lsum + pb
            l_sc[g] = alpha * l_sc[g] + lsum
            m_sc[g] = m_new
            p = jnp.concatenate([pb.astype(BF16) for pb in ps], axis=1)
            v = v_ref[rows, vb * LANES:(vb + 1) * LANES].astype(BF16)
            pv = _dot(p, v)
            old = acc_sc[ai, :, vb * LANES:(vb + 1) * LANES]
            mine = lo_half if vh == 0 else jnp.logical_not(lo_half)
            acc_sc[ai, :, vb * LANES:(vb + 1) * LANES] = jnp.where(mine, alpha * old + pv, old)

    m_sc[...] = jnp.full(m_sc.shape, NEG, F32)
    l_sc[...] = jnp.zeros(l_sc.shape, F32)
    acc_sc[...] = jnp.zeros(acc_sc.shape, F32)
    tile(kc_ref, vc_ref, slice(0, kc_ref.shape[0]))

    if with_lat:
        def kv_step(j, carry):
            tile(kl_ref, vl_ref, pl.ds(pl.multiple_of(j * tk, tk), tk))
            return carry

        lax.fori_loop(0, kl_ref.shape[0] // tk, kv_step, 0, unroll=8)

    def inv_l(g):
        return 1.0 / jnp.sum(l_sc[g], axis=-1, keepdims=True)

    for c in range(2):
        if not diff:
            o = jnp.where(lo_half, acc_sc[0, :, c * LANES:(c + 1) * LANES] * inv_l(2 * c),
                          acc_sc[0, :, c * LANES:(c + 1) * LANES] * inv_l(2 * c + 1))
        else:
            o1 = jnp.where(lo_half, acc_sc[0, :, c * LANES:(c + 1) * LANES] * inv_l(4 * c),
                           acc_sc[0, :, c * LANES:(c + 1) * LANES] * inv_l(4 * c + 2))
            o2 = jnp.where(lo_half, acc_sc[1, :, c * LANES:(c + 1) * LANES] * inv_l(4 * c + 1),
                           acc_sc[1, :, c * LANES:(c + 1) * LANES] * inv_l(4 * c + 3))
            o = o1 - lam_ref[...] * o2
            sq = o * o
            ms_lo = jnp.sum(jnp.where(lo_half, sq, 0.0), axis=-1, keepdims=True) * (1.0 / 64)
            ms_hi = jnp.sum(jnp.where(lo_half, 0.0, sq), axis=-1, keepdims=True) * (1.0 / 64)
            rs = jnp.where(lo_half, lax.rsqrt(ms_lo + RMS_EPS), lax.rsqrt(ms_hi + RMS_EPS))
            o = o * rs * sub_ref[:, c * LANES:(c + 1) * LANES] * (1.0 - lam_init)
        o_ref[:, c * LANES:(c + 1) * LANES] = o


def _flash(q, k, v, *, nb, s_len, l_len, heads, kcol=0, vcol=0, kw=None, vw=256,
           with_lat=True, diff=False, lam=None, subln=None, lam_init=0.0, name="flash"):
    nh = len(heads)
    n_acc = 2 if diff else 1
    ctx0 = nb * s_len // l_len
    tk = min(TK, s_len)
    if with_lat:
        tq = min(TQ, s_len)
        grid = (nb, s_len // tq)
        rows = nb * s_len
        qmap = lambda b, i: (b * (s_len // tq) + i, 0)
    else:
        tq = l_len
        grid = (nb, 1)
        rows = nb * l_len
        qmap = lambda b, i: (ctx0 + b, 0)
    in_specs = [pl.BlockSpec((tq, nh * LANES), qmap)]
    args = [q]
    if with_lat:
        in_specs += [pl.BlockSpec((s_len, kw), lambda b, i: (b, kcol)),
                     pl.BlockSpec((s_len, vw), lambda b, i: (b, vcol))]
        args += [k, v]
    in_specs += [pl.BlockSpec((l_len, kw), lambda b, i: (ctx0 + b, kcol)),
                 pl.BlockSpec((l_len, vw), lambda b, i: (ctx0 + b, vcol))]
    args += [k, v]
    if diff:
        in_specs += [pl.BlockSpec((1, LANES), lambda b, i: (0, 0)),
                     pl.BlockSpec((1, 256), lambda b, i: (0, 0))]
        args += [lam, subln]
    return pl.pallas_call(
        functools.partial(_flash_kernel, heads=heads, n_acc=n_acc, with_lat=with_lat,
                          diff=diff, lam_init=lam_init, tk=tk),
        grid=grid,
        in_specs=in_specs,
        out_specs=pl.BlockSpec((tq, 256), qmap if with_lat else (lambda b, i: (b, 0))),
        out_shape=jax.ShapeDtypeStruct((rows, 256), F32),
        scratch_shapes=[pltpu.VMEM((nh, tq, LANES), F32), pltpu.VMEM((nh, tq, LANES), F32),
                        pltpu.VMEM((n_acc, tq, 256), F32)],
        compiler_params=_cparams(("parallel", "parallel")),
        name=name,
    )(*args)


MLA_HEADSPEC = tuple((h, h // 2, h % 2, 0) for h in range(MLA_HEADS))
NA_HEADSPEC = tuple((h // 2, h // 2, h % 2, 0) for h in range(NA_HEADS))
DIFF_HEADSPEC = tuple((g // 4, g // 4, (g // 2) % 2, g % 2) for g in range(2 * DIFF_HEADS))


def _na_kernel(q_ref, k_ref, v_ref, kc_ref, vc_ref, bias_ref, o_ref, *, rows):
    rb = pl.program_id(1)
    k0 = jnp.clip(NA_QROWS * rb - NA_KH // 2, 0, rows - NA_KROWS)
    start = pl.multiple_of(k0 * GRID_W, GRID_W)
    nk = NA_KROWS * GRID_W
    kw = k_ref[pl.ds(start, nk), :].astype(BF16)
    vw = v_ref[pl.ds(start, nk), :].astype(BF16)
    kc = kc_ref[...].astype(BF16)
    vc = vc_ref[...].astype(BF16)
    tq = q_ref.shape[0]
    lane = lax.broadcasted_iota(jnp.int32, (tq, LANES), 1)
    lo_half = lane < 64
    for c in range(2):
        o_c = jnp.zeros((tq, LANES), F32)
        for half in range(2):
            h = 2 * c + half
            q = q_ref[:, h * LANES:(h + 1) * LANES]
            s_loc = _dot_nt(q, kw[:, c * LANES:(c + 1) * LANES])
            s_ctx = _dot_nt(q, kc[:, c * LANES:(c + 1) * LANES])
            blocks = [s_loc[:, i * LANES:(i + 1) * LANES] + bias_ref[0, h, :, i * LANES:(i + 1) * LANES]
                      for i in range(nk // LANES)]
            blocks += [s_ctx[:, i * LANES:(i + 1) * LANES] for i in range(kc.shape[0] // LANES)]
            mx = blocks[0]
            for blk in blocks[1:]:
                mx = jnp.maximum(mx, blk)
            m = jnp.max(mx, axis=-1, keepdims=True)
            ps = [jnp.exp2(blk - m) for blk in blocks]
            lsum = ps[0]
            for pb in ps[1:]:
                lsum = lsum + pb
            l = jnp.sum(lsum, axis=-1, keepdims=True)
            n_loc = nk // LANES
            p_loc = jnp.concatenate([pb.astype(BF16) for pb in ps[:n_loc]], axis=1)
            p_ctx = jnp.concatenate([pb.astype(BF16) for pb in ps[n_loc:]], axis=1)
            pv = (_dot(p_loc, vw[:, c * LANES:(c + 1) * LANES]) + _dot(p_ctx, vc[:, c * LANES:(c + 1) * LANES]))
            mine = lo_half if half == 0 else jnp.logical_not(lo_half)
            o_c = jnp.where(mine, pv * (1.0 / l), o_c)
        o_ref[:, c * LANES:(c + 1) * LANES] = o_c


def _na_attention(qna, kna, vna, bias, *, layer, nb, s_len, l_len):
    rows = s_len // GRID_W
    tq = NA_QROWS * GRID_W
    nrb = rows // NA_QROWS
    ctx0 = nb * s_len // l_len

    def variant(rb):
        return jnp.where(rb == 0, 0, jnp.where(rb == nrb - 1, 2, 1))

    return pl.pallas_call(
        functools.partial(_na_kernel, rows=rows),
        grid=(nb, nrb),
        in_specs=[pl.BlockSpec((tq, NA_HEADS * LANES), lambda b, r: (b * nrb + r, 0)),
                  pl.BlockSpec((s_len, 256), lambda b, r: (b, 0)),
                  pl.BlockSpec((s_len, 256), lambda b, r: (b, 0)),
                  pl.BlockSpec((l_len, 256), lambda b, r: (ctx0 + b, 0)),
                  pl.BlockSpec((l_len, 256), lambda b, r: (ctx0 + b, 0)),
                  pl.BlockSpec((1, NA_HEADS, tq, NA_KROWS * GRID_W), lambda b, r: (variant(r), layer, 0, 0))],
        out_specs=pl.BlockSpec((tq, 256), lambda b, r: (b * nrb + r, 0)),
        out_shape=jax.ShapeDtypeStruct((nb * s_len, 256), F32),
        compiler_params=_cparams(("parallel", "arbitrary"), 56),
        name="na_attn",
    )(qna, kna, vna, kna, vna, bias)


def _na_bias_tables(rpb, rows):
    a = np.arange(NA_QROWS)
    qc = np.arange(GRID_W)
    kr_rel = np.arange(NA_KROWS)
    kc = np.arange(GRID_W)
    col0 = np.clip(qc - NA_KW // 2, 0, GRID_W - NA_KW)
    col_valid = (kc[None, :] >= col0[:, None]) & (kc[None, :] < col0[:, None] + NA_KW)
    col_off = np.clip(kc[None, :] - qc[:, None] + (NA_KW - 1), 0, 2 * NA_KW - 2)
    oh_col = (col_off[:, :, None] == np.arange(2 * NA_KW - 1)).astype(np.float32)
    big = 10 ** 6
    out = []
    for r_blk, k0, rows_eff in ((0, 0, big), (NA_QROWS, NA_KH // 2, big), (rows - NA_QROWS, rows - NA_KROWS, rows)):
        qr = r_blk + a
        r0 = np.clip(qr - NA_KH // 2, 0, rows_eff - NA_KH)
        kr = k0 + kr_rel
        row_valid = (kr[None, :] >= r0[:, None]) & (kr[None, :] < r0[:, None] + NA_KH)
        row_off = np.clip(kr[None, :] - qr[:, None] + (NA_KH - 1), 0, 2 * NA_KH - 2)
        oh_row = (row_off[:, :, None] == np.arange(2 * NA_KH - 1)).astype(np.float32)
        b = jnp.einsum('akr,hrc,qlc->haqkl', oh_row, rpb, oh_col, precision=lax.Precision.HIGHEST)
        valid = row_valid[:, None, :, None] & col_valid[None, :, None, :]
        b = jnp.where(valid[None], b * LOG2E, NEG)
        out.append(b.reshape(rpb.shape[0], NA_QROWS * GRID_W, NA_KROWS * GRID_W))
    return jnp.stack(out)


def _s5_kernel(*refs, nb):
    uf_refs, ub_refs = refs[:nb], refs[nb:2 * nb]
    (bf_ref, bb_ref, a_ref, cf_ref, cb_ref, d_ref, yf_ref, yb_ref,
     uf_sc, ub_sc, buf_sc, bub_sc, hf_sc, hb_sc) = refs[2 * nb:]
    j = pl.program_id(0)
    tc = uf_refs[0].shape[0]
    half = S5_GROUPS * S5_STATE

    @pl.when(j == 0)
    def _():
        hf_sc[...] = jnp.zeros(hf_sc.shape, F32)
        hb_sc[...] = jnp.zeros(hb_sc.shape, F32)

    for b in range(nb):
        for hv in range(2):
            uf_sc[hv, pl.ds(b, tc, stride=nb), :] = uf_refs[b][:, hv * LANES:(hv + 1) * LANES]
            ub_sc[hv, pl.ds(b, tc, stride=nb), :] = ub_refs[b][:, hv * LANES:(hv + 1) * LANES]

    def scan(u_sc, b_ref, c_ref, y_ref, bu_sc, h_sc, d, reverse):
        u = jnp.concatenate([u_sc[0], u_sc[1]], axis=1)
        bu_sc[...] = _dot(u.astype(BF16), b_ref[...])
        ar = jnp.broadcast_to(a_ref[d, 0:1, :], (nb, half))
        ai = jnp.broadcast_to(a_ref[d, 1:2, :], (nb, half))

        def step(t, carry):
            hr, hi = carry
            tt = (tc - 1 - t) if reverse else t
            row = pl.multiple_of(tt * nb, nb)
            nr = ar * hr - ai * hi + bu_sc[pl.ds(row, nb), 0:half]
            ni = ar * hi + ai * hr + bu_sc[pl.ds(row, nb), half:2 * half]
            bu_sc[pl.ds(row, nb), 0:half] = nr
            bu_sc[pl.ds(row, nb), half:2 * half] = ni
            return nr, ni

        hr, hi = lax.fori_loop(0, tc, step, (h_sc[:, 0:half], h_sc[:, half:2 * half]), unroll=True)
        h_sc[:, 0:half] = hr
        h_sc[:, half:2 * half] = hi
        y = _dot(bu_sc[...].astype(BF16), c_ref[...])
        y_ref[...] = y if reverse else y + d_ref[...] * u

    scan(uf_sc, bf_ref, cf_ref, yf_ref, buf_sc, hf_sc, 0, False)
    scan(ub_sc, bb_ref, cb_ref, yb_ref, bub_sc, hb_sc, 1, True)


def _s5_scan(z, bmat, avec, cmat, d_skip, *, nb, s_len, l_len):
    tc = S5_TC
    cr = tc * nb
    nctx, nlat = l_len // tc, s_len // tc
    nchunk = nctx + nlat
    half = S5_GROUPS * S5_STATE
    n_lat = nb * s_len

    def bwd(j):
        return jnp.where(j < nctx, nctx - 1 - j, nchunk - 1 - (j - nctx))

    def u_spec(b, order):
        def row_block(j):
            c = order(j)
            return jnp.where(c < nctx, (n_lat + b * l_len) // tc + c, (b * s_len) // tc + c - nctx)
        return pl.BlockSpec((tc, 256), lambda j: (row_block(j), 0))

    const = lambda shape: pl.BlockSpec(shape, lambda j: (0,) * len(shape))
    return pl.pallas_call(
        functools.partial(_s5_kernel, nb=nb),
        grid=(nchunk,),
        in_specs=[u_spec(b, lambda j: j) for b in range(nb)] + [u_spec(b, bwd) for b in range(nb)]
                 + [const((256, 2 * half)), const((256, 2 * half)), const((2, 2, half)),
                    const((2 * half, 256)), const((2 * half, 256)), const((1, 256))],
        out_specs=[pl.BlockSpec((cr, 256), lambda j: (j, 0)),
                   pl.BlockSpec((cr, 256), lambda j: (bwd(j), 0))],
        out_shape=[jax.ShapeDtypeStruct((nchunk * cr, 256), F32)] * 2,
        scratch_shapes=[pltpu.VMEM((2, cr, LANES), F32), pltpu.VMEM((2, cr, LANES), F32),
                        pltpu.VMEM((cr, 2 * half), F32), pltpu.VMEM((cr, 2 * half), F32),
                        pltpu.VMEM((nb, 2 * half), F32), pltpu.VMEM((nb, 2 * half), F32)],
        compiler_params=_cparams(("arbitrary",)),
        name="s5_scan",
    )(*([z] * (2 * nb)), bmat[0], bmat[1], avec, cmat[0], cmat[1], d_skip)


def _s5_finish_kernel(yf_ref, yb_ref, w_ref, o_ref, r_sc):
    nb, steps = o_ref.shape[0], o_ref.shape[1]
    r = _dot((yf_ref[...] + yb_ref[...]).astype(BF16), w_ref[...])
    o = r[:, :BRANCH_W] * jax.nn.sigmoid(r[:, BRANCH_W:])
    for hv in range(2):
        r_sc[hv] = o[:, hv * LANES:(hv + 1) * LANES]
    for b in range(nb):
        for hv in range(2):
            o_ref[b, :, hv * LANES:(hv + 1) * LANES] = r_sc[hv, pl.ds(b, steps, stride=nb), :]


def _s5_finish(yf, yb, w_glu, *, nb, t0, t_len):
    steps = TM // nb
    tile0 = t0 // steps
    spec = pl.BlockSpec((TM, 256), lambda i: (tile0 + i, 0))
    out = pl.pallas_call(
        _s5_finish_kernel,
        grid=(t_len // steps,),
        in_specs=[spec, spec, pl.BlockSpec((256, 512), lambda i: (0, 0))],
        out_specs=pl.BlockSpec((nb, steps, 256), lambda i: (0, i, 0)),
        out_shape=jax.ShapeDtypeStruct((nb, t_len, 256), F32),
        scratch_shapes=[pltpu.VMEM((2, TM, LANES), F32)],
        compiler_params=_cparams(("parallel",)),
        name="s5_glu",
    )(yf, yb, w_glu)
    return out.reshape(nb * t_len, 256)


def _s5_params(lam_re, lam_im, log_dt, b_re, b_im, c_re, c_im):
    lam = lax.complex(lam_re, lam_im)
    dt = jnp.exp(log_dt)[..., None]
    a_bar = jnp.exp(lam * dt)
    b_bar = ((a_bar - 1.0) / lam)[..., None] * lax.complex(b_re, b_im)
    eye = jnp.eye(S5_GROUPS, dtype=F32)
    g, p, ch = S5_GROUPS, S5_STATE, S5_GROUP_CH

    def bdiag_in(m):
        return jnp.einsum('gh,dgpc->dgchp', eye, m).reshape(2, g * ch, g * p)

    def bdiag_out(m):
        return jnp.einsum('gh,dgcp->dgphc', eye, m).reshape(2, g * p, g * ch)

    bmat = jnp.concatenate([bdiag_in(jnp.real(b_bar)), bdiag_in(jnp.imag(b_bar))], axis=2).astype(BF16)
    cmat = jnp.concatenate([bdiag_out(c_re), -bdiag_out(c_im)], axis=1).astype(BF16)
    avec = jnp.stack([jnp.real(a_bar).reshape(2, g * p), jnp.imag(a_bar).reshape(2, g * p)], axis=1)
    return bmat, avec, cmat


def _merge_kernel(*refs, n_lat_tiles, has_ctx):
    x_ref, mod_ref = refs[0], refs[1]
    n_br = 8 if has_ctx else 4
    br_refs = refs[2:2 + n_br]
    wg_ref, bg_ref, wb_ref, wo_ref, g_ref, b_ref, x1_ref, h2_ref = refs[2 + n_br:]
    d = D_MODEL
    m = mod_ref[0]
    x = x_ref[...]
    hb = (x * (1.0 + m[1:2]) + m[0:1]).astype(BF16)
    is_ctx = pl.program_id(0) >= n_lat_tiles
    acc = None
    for i in range(4):
        if has_ctx:
            o = jnp.where(is_ctx, br_refs[2 * i + 1][...], br_refs[2 * i][...])
        else:
            o = br_refs[i][...]
        gate = jax.nn.sigmoid(_dot(hb, wg_ref[0, :, i * d:(i + 1) * d]) + bg_ref[:, i * d:(i + 1) * d])
        term = gate * _dot(o.astype(BF16), wb_ref[0, i])
        acc = term if acc is None else acc + term
    y = _dot(acc.astype(BF16), wo_ref[0])
    x1 = _layer_norm(ALPHA * x + m[2:3] * y, g_ref[...], b_ref[...])
    x1_ref[...] = x1
    h2_ref[...] = x1 * (1.0 + m[4:5]) + m[3:4]


def _merge(xa, modl, outs_lat, outs_ctx, wg, bg, wb, wo, g, b, seg, n, n_lat_tiles, layer):
    d = D_MODEL
    tok = pl.BlockSpec((TM, d), lambda i: (i, 0))
    lat = pl.BlockSpec((TM, 256), lambda i: (jnp.minimum(i, n_lat_tiles - 1), 0))
    ctx = pl.BlockSpec((TM, 256), lambda i: (jnp.maximum(i - n_lat_tiles, 0), 0))

    def const(a):
        return pl.BlockSpec(a.shape, lambda i: (0,) * a.ndim)

    def stacked(a):
        return pl.BlockSpec((1,) + a.shape[1:], lambda i: (layer,) + (0,) * (a.ndim - 1))
    has_ctx = outs_ctx is not None
    if has_ctx:
        branches = [a for pair in zip(outs_lat, outs_ctx) for a in pair]
        br_specs = [lat, ctx] * 4
    else:
        branches, br_specs = list(outs_lat), [lat] * 4
    return pl.pallas_call(
        functools.partial(_merge_kernel, n_lat_tiles=n_lat_tiles, has_ctx=has_ctx),
        grid=(n // TM,),
        in_specs=[tok, pl.BlockSpec((1, 6, d), lambda i: (seg(i), 0, 0))] + br_specs
                 + [stacked(wg), const(bg), stacked(wb), stacked(wo), const(g), const(b)],
        out_specs=[tok, tok],
        out_shape=[jax.ShapeDtypeStruct((n, d), F32)] * 2,
        compiler_params=_cparams(("parallel",), 56),
        name="merge_ln1",
    )(xa, modl, *branches, wg, bg, wb, wo, g, b)


def _start_row_gather(src_hbm, dst, idx_ref, base, n_rows, sem):
    def body(r, c):
        pltpu.make_async_copy(src_hbm.at[pl.ds(idx_ref[base + r], 1)], dst.at[pl.ds(r, 1)], sem).start()
        return c

    lax.fori_loop(0, n_rows, body, 0, unroll=8)


def _wait_row_gather(src_hbm, dst, sem):
    pltpu.make_async_copy(src_hbm.at[pl.ds(0, dst.shape[0])], dst, sem).wait()


def _swiglu_block(xb, w1_ref, w3_ref, w2_ref, o_ref, between=None):
    lead = (0,) * (len(w1_ref.shape) - 2)
    n_chunks = D_FF // FF_CHUNK
    for c in range(n_chunks):
        cs = slice(c * FF_CHUNK, (c + 1) * FF_CHUNK)
        a = _dot(xb, w1_ref[lead + (slice(None), cs)])
        b = _dot(xb, w3_ref[lead + (slice(None), cs)])
        g = (a * jax.nn.sigmoid(a) * b).astype(BF16)
        y = _dot(g, w2_ref[lead + (cs, slice(None))])
        if c == 0:
            o_ref[...] = y
        else:
            o_ref[...] += y
        if between is not None:
            between(c, n_chunks)


def _ffn_kernel(h_ref, x_ref, mod_ref, w1_ref, w3_ref, w2_ref, g_ref, b_ref, o_ref, acc_sc):
    _swiglu_block(h_ref[...].astype(BF16), w1_ref, w3_ref, w2_ref, acc_sc)
    m = mod_ref[0]
    o_ref[...] = _layer_norm(ALPHA * x_ref[...] + m[5:6] * acc_sc[...], g_ref[...], b_ref[...])


def _moe_ffn_kernel(be_ref, tok_ref, start_ref, nused_ref, h_hbm, w1_ref, w3_ref, w2_ref, o_ref, xbuf, sem):
    i = pl.program_id(0)
    last = pl.num_programs(0) - 1
    slot = i % 2
    nxt_base = start_ref[jnp.minimum(i + 1, last)]
    nxt_buf, nxt_sem = xbuf.at[1 - slot], sem.at[1 - slot]

    @pl.when(i == 0)
    def _():
        _start_row_gather(h_hbm, xbuf.at[0], tok_ref, start_ref[0], MOE_BLK, sem.at[0])

    _wait_row_gather(h_hbm, xbuf.at[slot], sem.at[slot])

    def start_piece(c, n_chunks):
        per = -(-MOE_BLK // n_chunks)
        for r in range(c * per, min((c + 1) * per, MOE_BLK)):
            pltpu.make_async_copy(h_hbm.at[pl.ds(tok_ref[nxt_base + r], 1)], nxt_buf.at[pl.ds(r, 1)],
                                  nxt_sem).start()

    @pl.when(i < nused_ref[0])
    def _():
        _swiglu_block(xbuf[slot].astype(BF16), w1_ref, w3_ref, w2_ref, o_ref, between=start_piece)

    @pl.when(i >= nused_ref[0])
    def _():
        o_ref[...] = jnp.zeros(o_ref.shape, F32)
        _start_row_gather(h_hbm, nxt_buf, tok_ref, nxt_base, MOE_BLK, nxt_sem)

    @pl.when(i == last)
    def _():
        _wait_row_gather(h_hbm, nxt_buf, nxt_sem)


def _moe_ffn(h, block_e, tok_sorted, blk_start, n_used, w1, w3, w2, layer):
    d = h.shape[1]
    n_blocks = block_e.shape[0]
    wspec = lambda shape: pl.BlockSpec(shape, lambda i, be, tok, st, nu: (layer, be[i], 0, 0),
                                       pipeline_mode=pl.Buffered(1))
    grid_spec = pltpu.PrefetchScalarGridSpec(
        num_scalar_prefetch=4,
        grid=(n_blocks,),
        in_specs=[pl.BlockSpec(memory_space=pl.ANY), wspec((1, 1, d, D_FF)), wspec((1, 1, d, D_FF)),
                  wspec((1, 1, D_FF, d))],
        out_specs=pl.BlockSpec((MOE_BLK, d), lambda i, be, tok, st, nu: (i, 0)),
        scratch_shapes=[pltpu.VMEM((2, MOE_BLK, d), F32), pltpu.SemaphoreType.DMA((2,))],
    )
    return pl.pallas_call(
        _moe_ffn_kernel,
        grid_spec=grid_spec,
        out_shape=jax.ShapeDtypeStruct((n_blocks * MOE_BLK, d), F32),
        compiler_params=_cparams(("arbitrary",), 56),
        name="moe_swiglu",
    )(block_e, tok_sorted, blk_start, n_used, h, w1, w3, w2)


def _ffn(h2, x1, modl, w1, w3, w2, g, b, seg, layer):
    n, d = h2.shape
    tok = pl.BlockSpec((TM, d), lambda i: (i, 0))
    vec = pl.BlockSpec((1, d), lambda i: (0, 0))
    wspec = lambda shape: pl.BlockSpec(shape, lambda i: (layer, 0, 0), pipeline_mode=pl.Buffered(1))
    return pl.pallas_call(
        _ffn_kernel,
        grid=(n // TM,),
        in_specs=[tok, tok, pl.BlockSpec((1, 6, d), lambda i: (seg(i), 0, 0)),
                  wspec((1, d, D_FF)), wspec((1, d, D_FF)), wspec((1, D_FF, d)), vec, vec],
        out_specs=tok,
        out_shape=jax.ShapeDtypeStruct((n, d), F32),
        scratch_shapes=[pltpu.VMEM((TM, d), F32)],
        compiler_params=_cparams(("parallel",), 56),
        name="swiglu_ln2",
    )(h2, x1, modl, w1, w3, w2, g, b)


def _router_kernel(h_ref, w_ref, o_ref):
    logits = jnp.dot(h_ref[...], w_ref[...], preferred_element_type=F32, precision=lax.Precision.HIGHEST)
    lane = lax.broadcasted_iota(jnp.int32, logits.shape, 1)
    lg = jnp.where(lane < N_EXPERTS, logits, NEG)
    v1 = jnp.max(lg, axis=-1, keepdims=True)
    i1 = jnp.min(jnp.where(lg == v1, lane, LANES), axis=-1, keepdims=True)
    lg2 = jnp.where(lane == i1, NEG, lg)
    v2 = jnp.max(lg2, axis=-1, keepdims=True)
    i2 = jnp.min(jnp.where(lg2 == v2, lane, LANES), axis=-1, keepdims=True)
    e = jnp.exp(v2 - v1)
    g1 = 1.0 / (1.0 + e)
    g2 = e / (1.0 + e)
    out = jnp.where(lane == 0, i1.astype(F32), jnp.where(lane == 1, i2.astype(F32),
                    jnp.where(lane == 2, g1, jnp.where(lane == 3, g2, 0.0))))
    o_ref[...] = out


def _router(h2, w_router_pad):
    n, d = h2.shape
    return pl.pallas_call(
        _router_kernel,
        grid=(n // TM,),
        in_specs=[pl.BlockSpec((TM, d), lambda i: (i, 0)), pl.BlockSpec((d, LANES), lambda i: (0, 0))],
        out_specs=pl.BlockSpec((TM, LANES), lambda i: (i, 0)),
        out_shape=jax.ShapeDtypeStruct((n, LANES), F32),
        compiler_params=_cparams(("parallel",)),
        name="router",
    )(h2, w_router_pad)


def _moe_combine_kernel(dest_ref, x_ref, mod_ref, gt_ref, y_hbm, g_ref, b_ref, o_ref, ybuf, sem):
    i = pl.program_id(0)
    last = pl.num_programs(0) - 1
    slot = i % 2
    nxt_base = jnp.minimum(i + 1, last) * 2 * TM
    nxt_buf, nxt_sem = ybuf.at[1 - slot], sem.at[1 - slot]

    @pl.when(i == 0)
    def _():
        _start_row_gather(y_hbm, ybuf.at[0], dest_ref, 0, 2 * TM, sem.at[0])

    for r in range(2 * TM):
        pltpu.make_async_copy(y_hbm.at[pl.ds(dest_ref[nxt_base + r], 1)], nxt_buf.at[pl.ds(r, 1)], nxt_sem).start()

    _wait_row_gather(y_hbm, ybuf.at[slot], sem.at[slot])
    gt = gt_ref[...]
    f = gt[:, 2:3] * ybuf[slot, 0:TM, :] + gt[:, 3:4] * ybuf[slot, TM:2 * TM, :]
    m = mod_ref[0]
    o_ref[...] = _layer_norm(ALPHA * x_ref[...] + m[5:6] * f, g_ref[...], b_ref[...])

    @pl.when(i == last)
    def _():
        _wait_row_gather(y_hbm, nxt_buf, nxt_sem)


def _moe_combine(x1, modl, y_rows, dest_tiles, gates, g, b, seg):
    n, d = x1.shape
    tok = pl.BlockSpec((TM, d), lambda i, dst: (i, 0))
    vec = pl.BlockSpec((1, d), lambda i, dst: (0, 0))
    grid_spec = pltpu.PrefetchScalarGridSpec(
        num_scalar_prefetch=1,
        grid=(n // TM,),
        in_specs=[tok, pl.BlockSpec((1, 6, d), lambda i, dst: (seg(i), 0, 0)),
                  pl.BlockSpec((TM, LANES), lambda i, dst: (i, 0)), pl.BlockSpec(memory_space=pl.ANY), vec, vec],
        out_specs=tok,
        scratch_shapes=[pltpu.VMEM((2, 2 * TM, d), F32), pltpu.SemaphoreType.DMA((2,))],
    )
    return pl.pallas_call(
        _moe_combine_kernel,
        grid_spec=grid_spec,
        out_shape=jax.ShapeDtypeStruct((n, d), F32),
        compiler_params=_cparams(("arbitrary",)),
        name="moe_combine_ln2",
    )(dest_tiles, x1, modl, gates, y_rows, g, b)


def _rope_tables(s_len, width, rope_lane, extra_rows):
    t = jnp.arange(s_len)
    lane = np.arange(width)
    inv = ROPE_BASE ** (-(jnp.asarray(lane % 8, F32)) / 8.0)
    pos = jnp.where((lane & 16) == 0, (t // GRID_W)[:, None], (t % GRID_W)[:, None]).astype(F32)
    ang = pos * inv[None, :]
    cos = jnp.where(rope_lane[None, :], jnp.cos(ang), 1.0)
    sin = jnp.where(rope_lane[None, :], jnp.where((lane & 8) == 0, -jnp.sin(ang), jnp.sin(ang)), 0.0)
    cos = jnp.concatenate([cos, jnp.ones((extra_rows, width), F32)])
    sin = jnp.concatenate([sin, jnp.zeros((extra_rows, width), F32)])
    return cos, sin


def _pad_heads(w, n_heads, width):
    k = w.shape[0]
    w = w.reshape(k, n_heads, -1)
    return jnp.pad(w, ((0, 0), (0, 0), (0, width - w.shape[-1]))).reshape(k, n_heads * width)


def _route(top, n_tok):
    e_flat = top[:, 0:2].astype(jnp.int32).reshape(-1)
    n_assign = 2 * n_tok
    onehot = (e_flat[:, None] == jnp.arange(N_EXPERTS)[None, :]).astype(jnp.int32)
    csum = jnp.cumsum(onehot, axis=0)
    counts = csum[-1]
    padded = (counts + MOE_BLK - 1) // MOE_BLK * MOE_BLK
    pad_end = jnp.cumsum(padded)
    pad_start = pad_end - padded
    dest = jnp.sum(onehot * (csum - 1 + pad_start[None, :]), axis=1)
    n_blocks = -(-n_assign // MOE_BLK) + N_EXPERTS
    blk_row0 = jnp.arange(n_blocks) * MOE_BLK
    block_e = jnp.minimum(jnp.searchsorted(pad_end, blk_row0, side='right'), N_EXPERTS - 1).astype(jnp.int32)
    n_used = (pad_end[-1:] // MOE_BLK).astype(jnp.int32)
    tok_sorted = jnp.concatenate([(jnp.argsort(e_flat) // 2).astype(jnp.int32), jnp.zeros((MOE_BLK,), jnp.int32)])
    seg_start = jnp.cumsum(counts) - counts
    blk_start = jnp.clip(seg_start[block_e] + blk_row0 - pad_start[block_e], 0, n_assign).astype(jnp.int32)
    dest_tiles = dest.astype(jnp.int32).reshape(n_tok // TM, TM, 2).transpose(0, 2, 1).reshape(-1)
    return tok_sorted, blk_start, block_e, n_used, dest_tiles


def kernel(x, c, ctx, c_ctx, w_ada, b_ada, w_in, na_rpb, mla_q_norm, mla_kv_norm, mla_w_uq, mla_w_ukv,
           s5_lam_re, s5_lam_im, s5_log_dt, s5_b_re, s5_b_im, s5_c_re, s5_c_im, s5_d, s5_w_glu,
           diff_lam_q1, diff_lam_k1, diff_lam_q2, diff_lam_k2, diff_subln,
           w_branch, w_gate, b_gate, w_out, ln1_g, ln1_b, ln2_g, ln2_b,
           ffn_w1, ffn_w3, ffn_w2, moe_router, moe_w1, moe_w3, moe_w2):
    nb, s_len, d = x.shape
    l_len = ctx.shape[1]
    n_lat, n_ctx = nb * s_len, nb * l_len
    n_tot = n_lat + n_ctx
    rows = s_len // GRID_W
    assert d == D_MODEL and s_len % TM == 0 and n_ctx % TM == 0 and rows % NA_QROWS == 0 and rows >= NA_KROWS
    assert l_len % S5_TC == 0 and s_len % S5_TC == 0

    def seg(i):
        return jnp.minimum((i * TM) // s_len, nb)

    n_lat_tiles = n_lat // TM

    xa = jnp.concatenate([x.reshape(n_lat, d), ctx.reshape(n_ctx, d)], axis=0)
    nrow_mod = -(-(nb + 1) // 16) * 16
    cvec = jnp.zeros((nrow_mod, d), F32).at[:nb].set(c).at[nb].set(c_ctx)
    mod_all = _ada_all(cvec, w_ada, b_ada).reshape(DEPTH, nrow_mod, 6, d)

    lane512 = np.arange(512)
    q_rope = (lane512 % LANES >= MLA_NOPE) & (lane512 % LANES < MLA_NOPE + MLA_ROPE)
    cq, sq = _rope_tables(s_len, 512, q_rope, TM)
    ck, sk = _rope_tables(s_len, LANES, np.arange(LANES) < MLA_ROPE, TM)
    cd, sd = _rope_tables(s_len, 256, np.ones((256,), bool), TM)
    tabs = (cq, sq, ck, sk, cd, sd)
    place = np.zeros((LANES, MLA_HEADS * LANES), np.float32)
    for h in range(MLA_HEADS):
        place[np.arange(MLA_ROPE), h * LANES + MLA_NOPE + np.arange(MLA_ROPE)] = 1.0
    place = jnp.asarray(place, BF16)

    w_gate_b, w_branch_b, w_out_b = w_gate.astype(BF16), w_branch.astype(BF16), w_out.astype(BF16)
    ffn_b = (ffn_w1.astype(BF16), ffn_w3.astype(BF16), ffn_w2.astype(BF16))
    moe_b = (moe_w1.astype(BF16), moe_w3.astype(BF16), moe_w2.astype(BF16))
    bias_all = _na_bias_tables(na_rpb.reshape(DEPTH * NA_HEADS, 2 * NA_KH - 1, 2 * NA_KW - 1), rows)

    for layer in range(DEPTH):
        ctx_out = layer < DEPTH - 1
        lam_init = 0.8 - 0.6 * math.exp(-0.3 * layer)
        modl = mod_all[layer]

        wi = w_in[layer]
        w_in_pad = jnp.concatenate([wi[:, :1184], jnp.zeros((d, 96), F32), wi[:, 1184:]], axis=1).astype(BF16)
        wuq = _pad_heads(mla_w_uq[layer], MLA_HEADS, LANES).astype(BF16)
        wukv4 = mla_w_ukv[layer].reshape(-1, MLA_HEADS, MLA_NOPE + MLA_V)
        wuk = jnp.pad(wukv4[:, :, :MLA_NOPE], ((0, 0), (0, 0), (0, LANES - MLA_NOPE))).reshape(-1, MLA_HEADS * LANES)
        wuv = wukv4[:, :, MLA_NOPE:].reshape(-1, MLA_HEADS * MLA_V)
        wukv = jnp.concatenate([wuk, wuv], axis=1).astype(BF16)
        bmat, avec, cmat = _s5_params(s5_lam_re[layer], s5_lam_im[layer], s5_log_dt[layer], s5_b_re[layer],
                                      s5_b_im[layer], s5_c_re[layer], s5_c_im[layer])
        lam = (jnp.exp(jnp.sum(diff_lam_q1[layer] * diff_lam_k1[layer]))
               - jnp.exp(jnp.sum(diff_lam_q2[layer] * diff_lam_k2[layer])) + lam_init)
        lam_vec = jnp.full((1, LANES), lam, F32)
        subln = jnp.tile(diff_subln[layer], DIFF_HEADS).reshape(1, 256)

        qm, km, vm, qd, kd, vd, qna, kna, vna, u5 = _proj(
            xa, modl, w_in_pad, tabs, mla_q_norm[layer].reshape(1, -1), mla_kv_norm[layer].reshape(1, -1),
            wuq, wukv, place, nb=nb, s_len=s_len)
        common = dict(nb=nb, s_len=s_len, l_len=l_len)
        o_na = _na_attention(qna, kna, vna, bias_all, layer=layer, **common)
        o_mla = _flash(qm, km, vm, heads=MLA_HEADSPEC, kw=512, name="mla_attn", **common)
        o_diff = _flash(qd, kd, vd, heads=DIFF_HEADSPEC, kw=256, diff=True, lam=lam_vec,
                        subln=subln, lam_init=lam_init, name="diff_attn", **common)
        yf, yb = _s5_scan(u5, bmat, avec, cmat, s5_d[layer].reshape(1, 256), **common)
        w_glu = s5_w_glu[layer].astype(BF16)
        o_s5_lat = _s5_finish(yf, yb, w_glu, nb=nb, t0=l_len, t_len=s_len)
        if ctx_out:
            o_na_c = _flash(qna, kna, vna, heads=NA_HEADSPEC, kw=256, with_lat=False, name="na_ctx_attn", **common)
            o_mla_c = _flash(qm, km, vm, heads=MLA_HEADSPEC, kw=512, with_lat=False,
                             name="mla_ctx_attn", **common)
            o_diff_c = _flash(qd, kd, vd, heads=DIFF_HEADSPEC, kw=256, diff=True, lam=lam_vec,
                              subln=subln, lam_init=lam_init, with_lat=False, name="diff_ctx_attn", **common)
            o_s5_c = _s5_finish(yf, yb, w_glu, nb=nb, t0=0, t_len=l_len)
            outs_ctx = [o_na_c, o_mla_c, o_s5_c, o_diff_c]
            n_act = n_tot
        else:
            outs_ctx = None
            n_act = n_lat
        x1, h2 = _merge(xa, modl, [o_na, o_mla, o_s5_lat, o_diff], outs_ctx,
                        w_gate_b, b_gate[layer].reshape(1, -1), w_branch_b, w_out_b,
                        ln1_g[layer].reshape(1, d), ln1_b[layer].reshape(1, d), seg, n_act, n_lat_tiles, layer)

        jj = layer // 2
        g2, b2 = ln2_g[layer].reshape(1, d), ln2_b[layer].reshape(1, d)
        if layer % 2 == 0:
            xa = _ffn(h2, x1, modl, *ffn_b, g2, b2, seg, jj)
        else:
            wr = jnp.pad(moe_router[jj], ((0, 0), (0, LANES - N_EXPERTS)))
            top = _router(h2, wr)
            tok_sorted, blk_start, block_e, n_used, dest_tiles = _route(top, n_act)
            y_rows = _moe_ffn(h2, block_e, tok_sorted, blk_start, n_used, *moe_b, jj)
            xa = _moe_combine(x1, modl, y_rows, dest_tiles, top, g2, b2, seg)
    return xa[:n_lat].reshape(nb, s_len, d)
```

```python
import functools
import math

import jax
import jax.numpy as jnp
import numpy as np
from jax import lax
from jax.experimental import pallas as pl
from jax.experimental.pallas import tpu as pltpu

F32 = jnp.float32
BF16 = jnp.bfloat16

D_MODEL = 1024
DEPTH = 4
GRID_W = 64
BRANCH_W = 256
NA_HEADS = 4
NA_HD = 64
NA_KH = 8
NA_KW = 16
MLA_HEADS = 4
MLA_NOPE = 64
MLA_ROPE = 32
MLA_V = 64
S5_GROUP_CH = 16
S5_GROUPS = 16
S5_STATE = 64
DIFF_HEADS = 4
DIFF_HD = 32
D_FF = 3584
N_EXPERTS = 8
ROPE_BASE = 10000.0
LN_EPS = 1e-5
RMS_EPS = 1e-6
ALPHA = (2 * DEPTH) ** 0.25
NA_SCALE = NA_HD ** -0.5
MLA_SCALE = (MLA_NOPE + MLA_ROPE) ** -0.5
DIFF_SCALE = DIFF_HD ** -0.5
LOG2E = 1.4426950408889634

LANES = 128
ZW = 2304
Z_NAQ, Z_NAK, Z_NAV, Z_MQ, Z_MKV, Z_MKR, Z_S5, Z_DQ, Z_DK, Z_DV = (
    0, 256, 512, 768, 1024, 1152, 1280, 1536, 1792, 2048)
NEG = -1e30
TM = 512
TQ = 1024
TK = 256
NA_QROWS = 8
NA_KROWS = NA_QROWS + NA_KH
S5_TC = 128
FF_CHUNK = 512
MOE_BLK = 512


def _cparams(sem, vmem_mb=48):
    return pltpu.CompilerParams(dimension_semantics=sem, vmem_limit_bytes=vmem_mb << 20)


def _dot(a, b):
    return jnp.dot(a, b, preferred_element_type=F32)


def _dot_nt(a, b):
    return lax.dot_general(a, b, (((1,), (1,)), ((), ())), preferred_element_type=F32)


def _layer_norm(r, g, b):
    rc = r - jnp.mean(r, axis=-1, keepdims=True)
    var = jnp.mean(rc * rc, axis=-1, keepdims=True)
    return rc * lax.rsqrt(var + LN_EPS) * g + b


def _ada_kernel(c_ref, w_ref, b_ref, o_ref):
    c = c_ref[...]
    cond = c * jax.nn.sigmoid(c)
    o_ref[0] = _dot(cond.astype(BF16), w_ref[0].astype(BF16)) + b_ref[0]


def _ada_all(cvec, w_ada, b_ada):
    nrow = cvec.shape[0]
    d = D_MODEL
    return pl.pallas_call(
        _ada_kernel,
        grid=(DEPTH, 6),
        in_specs=[pl.BlockSpec((nrow, d), lambda l, n: (0, 0)),
                  pl.BlockSpec((1, d, d), lambda l, n: (l, 0, n)),
                  pl.BlockSpec((1, 1, d), lambda l, n: (l, 0, n))],
        out_specs=pl.BlockSpec((1, nrow, d), lambda l, n: (l, 0, n)),
        out_shape=jax.ShapeDtypeStruct((DEPTH, nrow, 6 * d), F32),
        compiler_params=_cparams(("parallel", "parallel")),
        name="ada_mod",
    )(cvec, w_ada, b_ada.reshape(DEPTH, 1, 6 * d))


def _swap8(z):
    w = z.shape[-1]
    lane = lax.broadcasted_iota(jnp.int32, z.shape, 1)
    up = pltpu.roll(z, w - 8, 1)
    dn = pltpu.roll(z, 8, 1)
    return jnp.where((lane & 8) == 0, up, dn)


def _rms(z, g):
    return z * lax.rsqrt(jnp.mean(z * z, axis=-1, keepdims=True) + RMS_EPS) * g


def _proj_kernel(x_ref, mod_ref, w_ref, cq_ref, sq_ref, ck_ref, sk_ref, cd_ref, sd_ref,
                 qn_ref, kvn_ref, wuq_ref, wukv_ref, place_ref,
                 qm_ref, km_ref, vm_ref, qd_ref, kd_ref, vd_ref, qna_ref, kna_ref, vna_ref, u_ref):
    m = mod_ref[0]
    h = x_ref[...] * (1.0 + m[1:2]) + m[0:1]
    z = _dot(h.astype(BF16), w_ref[...])
    kna_ref[...] = z[:, Z_NAK:Z_NAK + 256].astype(BF16)
    vna_ref[...] = z[:, Z_NAV:Z_NAV + 256].astype(BF16)
    u_ref[...] = z[:, Z_S5:Z_S5 + 256]
    aq = _rms(z[:, Z_MQ:Z_MQ + 256], qn_ref[...])
    q = _dot(aq.astype(BF16), wuq_ref[...])
    q = q * cq_ref[...] + _swap8(q) * sq_ref[...]
    qm_ref[...] = (q * (MLA_SCALE * LOG2E)).astype(BF16)
    akv = _rms(z[:, Z_MKV:Z_MKV + 128], kvn_ref[...])
    kv = _dot(akv.astype(BF16), wukv_ref[...])
    kr = z[:, Z_MKR:Z_MKR + 128]
    kr = kr * ck_ref[...] + _swap8(kr) * sk_ref[...]
    km_ref[...] = (kv[:, :4 * LANES] + _dot(kr.astype(BF16), place_ref[...])).astype(BF16)
    vm_ref[...] = kv[:, 4 * LANES:].astype(BF16)
    dq = z[:, Z_DQ:Z_DQ + 256]
    dq = (dq * cd_ref[...] + _swap8(dq) * sd_ref[...]) * (DIFF_SCALE * LOG2E)
    dk = z[:, Z_DK:Z_DK + 256]
    dk = dk * cd_ref[...] + _swap8(dk) * sd_ref[...]
    kd_ref[...] = dk.astype(BF16)
    vd_ref[...] = z[:, Z_DV:Z_DV + 256].astype(BF16)
    lane = lax.broadcasted_iota(jnp.int32, (dq.shape[0], LANES), 1)
    for g in range(2 * DIFF_HEADS):
        blk = dq[:, (g // 4) * LANES:(g // 4 + 1) * LANES]
        qd_ref[:, g * LANES:(g + 1) * LANES] = jnp.where((lane // DIFF_HD) == (g % 4), blk, 0.0).astype(BF16)
    naq = z[:, Z_NAQ:Z_NAQ + 256] * (NA_SCALE * LOG2E)
    for hd in range(NA_HEADS):
        blk = naq[:, (hd // 2) * LANES:(hd // 2 + 1) * LANES]
        qna_ref[:, hd * LANES:(hd + 1) * LANES] = jnp.where((lane // NA_HD) == (hd % 2), blk, 0.0).astype(BF16)


def _proj(xa, modl, w_in_pad, tabs, qn, kvn, wuq, wukv, place, *, nb, s_len):
    n, d = xa.shape
    cq, sq, ck, sk, cd, sd = tabs
    tps = s_len // TM
    n_lat_tiles = nb * tps

    def tile(i):
        return jnp.where(i < n_lat_tiles, (i % nb) * tps + i // nb, i)

    def seg(i):
        return jnp.minimum((tile(i) * TM) // s_len, nb)

    def tab_idx(i):
        return jnp.where(i < n_lat_tiles, i // nb, tps)

    def tspec(width):
        return pl.BlockSpec((TM, width), lambda i: (tab_idx(i), 0))

    def wspec(a):
        return pl.BlockSpec(a.shape, lambda i: (0,) * a.ndim)

    def ospec(width):
        return pl.BlockSpec((TM, width), lambda i: (tile(i), 0))

    widths = [512, 512, 256, 1024, 256, 256, 512, 256, 256]
    return pl.pallas_call(
        _proj_kernel,
        grid=(n // TM,),
        in_specs=[pl.BlockSpec((TM, d), lambda i: (tile(i), 0)),
                  pl.BlockSpec((1, 6, d), lambda i: (seg(i), 0, 0)),
                  wspec(w_in_pad),
                  tspec(512), tspec(512), tspec(128), tspec(128), tspec(256), tspec(256),
                  wspec(qn), wspec(kvn), wspec(wuq), wspec(wukv), wspec(place)],
        out_specs=[ospec(w) for w in widths] + [ospec(256)],
        out_shape=[jax.ShapeDtypeStruct((n, w), BF16) for w in widths] + [jax.ShapeDtypeStruct((n, 256), F32)],
        compiler_params=_cparams(("parallel",), 56),
        name="in_proj_prep",
    )(xa, modl, w_in_pad, cq, sq, ck, sk, cd, sd, qn, kvn, wuq, wukv, place)


def _flash_kernel(*refs, heads, n_acc, with_lat, diff, lam_init, tk):
    refs = list(refs)
    q_ref = refs.pop(0)
    if with_lat:
        kl_ref, vl_ref = refs.pop(0), refs.pop(0)
    kc_ref, vc_ref = refs.pop(0), refs.pop(0)
    if diff:
        lam_ref, sub_ref = refs.pop(0), refs.pop(0)
    o_ref, m_sc, l_sc, acc_sc = refs
    tq = q_ref.shape[0]
    lane = lax.broadcasted_iota(jnp.int32, (tq, LANES), 1)
    lo_half = lane < 64

    def tile(k_ref, v_ref, rows):
        nkb = (rows.stop - rows.start if isinstance(rows, slice) else rows.size) // LANES
        for g, (kb, vb, vh, ai) in enumerate(heads):
            q = q_ref[:, g * LANES:(g + 1) * LANES]
            k = k_ref[rows, kb * LANES:(kb + 1) * LANES].astype(BF16)
            s = _dot_nt(q, k)
            blocks = [s[:, c * LANES:(c + 1) * LANES] for c in range(nkb)]
            mx = blocks[0]
            for blk in blocks[1:]:
                mx = jnp.maximum(mx, blk)
            m_prev = m_sc[g]
            m_new = jnp.maximum(m_prev, jnp.max(mx, axis=-1, keepdims=True))
            alpha = jnp.exp2(m_prev - m_new)
            ps = [jnp.exp2(blk - m_new) for blk in blocks]
            lsum = ps[0]
            for pb in ps[1:]:
                lsum = lsum + pb
            l_sc[g] = alpha * l_sc[g] + lsum
            m_sc[g] = m_new
            p = jnp.concatenate([pb.astype(BF16) for pb in ps], axis=1)
            v = v_ref[rows, vb * LANES:(vb + 1) * LANES].astype(BF16)
            pv = _dot(p, v)
            old = acc_sc[ai, :, vb * LANES:(vb + 1) * LANES]
            mine = lo_half if vh == 0 else jnp.logical_not(lo_half)
            acc_sc[ai, :, vb * LANES:(vb + 1) * LANES] = jnp.where(mine, alpha * old + pv, old)

    m_sc[...] = jnp.full(m_sc.shape, NEG, F32)
    l_sc[...] = jnp.zeros(l_sc.shape, F32)
    acc_sc[...] = jnp.zeros(acc_sc.shape, F32)
    tile(kc_ref, vc_ref, slice(0, kc_ref.shape[0]))

    if with_lat:
        def kv_step(j, carry):
            tile(kl_ref, vl_ref, pl.ds(pl.multiple_of(j * tk, tk), tk))
            return carry

        lax.fori_loop(0, kl_ref.shape[0] // tk, kv_step, 0, unroll=8)

    def inv_l(g):
        return 1.0 / jnp.sum(l_sc[g], axis=-1, keepdims=True)

    for c in range(2):
        if not diff:
            o = jnp.where(lo_half, acc_sc[0, :, c * LANES:(c + 1) * LANES] * inv_l(2 * c),
                          acc_sc[0, :, c * LANES:(c + 1) * LANES] * inv_l(2 * c + 1))
        else:
            o1 = jnp.where(lo_half, acc_sc[0, :, c * LANES:(c + 1) * LANES] * inv_l(4 * c),
                           acc_sc[0, :, c * LANES:(c + 1) * LANES] * inv_l(4 * c + 2))
            o2 = jnp.where(lo_half, acc_sc[1, :, c * LANES:(c + 1) * LANES] * inv_l(4 * c + 1),
                           acc_sc[1, :, c * LANES:(c + 1) * LANES] * inv_l(4 * c + 3))
            o = o1 - lam_ref[...] * o2
            sq = o * o
            ms_lo = jnp.sum(jnp.where(lo_half, sq, 0.0), axis=-1, keepdims=True) * (1.0 / 64)
            ms_hi = jnp.sum(jnp.where(lo_half, 0.0, sq), axis=-1, keepdims=True) * (1.0 / 64)
            rs = jnp.where(lo_half, lax.rsqrt(ms_lo + RMS_EPS), lax.rsqrt(ms_hi + RMS_EPS))
            o = o * rs * sub_ref[:, c * LANES:(c + 1) * LANES] * (1.0 - lam_init)
        o_ref[:, c * LANES:(c + 1) * LANES] = o


def _flash(q, k, v, *, nb, s_len, l_len, heads, kcol=0, vcol=0, kw=None, vw=256,
           with_lat=True, diff=False, lam=None, subln=None, lam_init=0.0, name="flash"):
    nh = len(heads)
    n_acc = 2 if diff else 1
    ctx0 = nb * s_len // l_len
    tk = min(TK, s_len)
    if with_lat:
        tq = min(TQ, s_len)
        grid = (nb, s_len // tq)
        rows = nb * s_len
        qmap = lambda b, i: (b * (s_len // tq) + i, 0)
    else:
        tq = l_len
        grid = (nb, 1)
        rows = nb * l_len
        qmap = lambda b, i: (ctx0 + b, 0)
    in_specs = [pl.BlockSpec((tq, nh * LANES), qmap)]
    args = [q]
    if with_lat:
        in_specs += [pl.BlockSpec((s_len, kw), lambda b, i: (b, kcol)),
                     pl.BlockSpec((s_len, vw), lambda b, i: (b, vcol))]
        args += [k, v]
    in_specs += [pl.BlockSpec((l_len, kw), lambda b, i: (ctx0 + b, kcol)),
                 pl.BlockSpec((l_len, vw), lambda b, i: (ctx0 + b, vcol))]
    args += [k, v]
    if diff:
        in_specs += [pl.BlockSpec((1, LANES), lambda b, i: (0, 0)),
                     pl.BlockSpec((1, 256), lambda b, i: (0, 0))]
        args += [lam, subln]
    return pl.pallas_call(
        functools.partial(_flash_kernel, heads=heads, n_acc=n_acc, with_lat=with_lat,
                          diff=diff, lam_init=lam_init, tk=tk),
        grid=grid,
        in_specs=in_specs,
        out_specs=pl.BlockSpec((tq, 256), qmap if with_lat else (lambda b, i: (b, 0))),
        out_shape=jax.ShapeDtypeStruct((rows, 256), F32),
        scratch_shapes=[pltpu.VMEM((nh, tq, LANES), F32), pltpu.VMEM((nh, tq, LANES), F32),
                        pltpu.VMEM((n_acc, tq, 256), F32)],
        compiler_params=_cparams(("parallel", "parallel")),
        name=name,
    )(*args)


MLA_HEADSPEC = tuple((h, h // 2, h % 2, 0) for h in range(MLA_HEADS))
NA_HEADSPEC = tuple((h // 2, h // 2, h % 2, 0) for h in range(NA_HEADS))
DIFF_HEADSPEC = tuple((g // 4, g // 4, (g // 2) % 2, g % 2) for g in range(2 * DIFF_HEADS))


def _na_kernel(q_ref, k_ref, v_ref, kc_ref, vc_ref, bias_ref, o_ref, *, rows):
    rb = pl.program_id(1)
    k0 = jnp.clip(NA_QROWS * rb - NA_KH // 2, 0, rows - NA_KROWS)
    start = pl.multiple_of(k0 * GRID_W, GRID_W)
    nk = NA_KROWS * GRID_W
    kw = k_ref[pl.ds(start, nk), :].astype(BF16)
    vw = v_ref[pl.ds(start, nk), :].astype(BF16)
    kc = kc_ref[...].astype(BF16)
    vc = vc_ref[...].astype(BF16)
    tq = q_ref.shape[0]
    lane = lax.broadcasted_iota(jnp.int32, (tq, LANES), 1)
    lo_half = lane < 64
    for c in range(2):
        o_c = jnp.zeros((tq, LANES), F32)
        for half in range(2):
            h = 2 * c + half
            q = q_ref[:, h * LANES:(h + 1) * LANES]
            s_loc = _dot_nt(q, kw[:, c * LANES:(c + 1) * LANES])
            s_ctx = _dot_nt(q, kc[:, c * LANES:(c + 1) * LANES])
            blocks = [s_loc[:, i * LANES:(i + 1) * LANES] + bias_ref[0, h, :, i * LANES:(i + 1) * LANES]
                      for i in range(nk // LANES)]
            blocks += [s_ctx[:, i * LANES:(i + 1) * LANES] for i in range(kc.shape[0] // LANES)]
            mx = blocks[0]
            for blk in blocks[1:]:
                mx = jnp.maximum(mx, blk)
            m = jnp.max(mx, axis=-1, keepdims=True)
            ps = [jnp.exp2(blk - m) for blk in blocks]
            lsum = ps[0]
            for pb in ps[1:]:
                lsum = lsum + pb
            l = jnp.sum(lsum, axis=-1, keepdims=True)
            n_loc = nk // LANES
            p_loc = jnp.concatenate([pb.astype(BF16) for pb in ps[:n_loc]], axis=1)
            p_ctx = jnp.concatenate([pb.astype(BF16) for pb in ps[n_loc:]], axis=1)
            pv = (_dot(p_loc, vw[:, c * LANES:(c + 1) * LANES]) + _dot(p_ctx, vc[:, c * LANES:(c + 1) * LANES]))
            mine = lo_half if half == 0 else jnp.logical_not(lo_half)
            o_c = jnp.where(mine, pv * (1.0 / l), o_c)
        o_ref[:, c * LANES:(c + 1) * LANES] = o_c


def _na_attention(qna, kna, vna, bias, *, layer, nb, s_len, l_len):
    rows = s_len // GRID_W
    tq = NA_QROWS * GRID_W
    nrb = rows // NA_QROWS
    ctx0 = nb * s_len // l_len

    def variant(rb):
        return jnp.where(rb == 0, 0, jnp.where(rb == nrb - 1, 2, 1))

    return pl.pallas_call(
        functools.partial(_na_kernel, rows=rows),
        grid=(nb, nrb),
        in_specs=[pl.BlockSpec((tq, NA_HEADS * LANES), lambda b, r: (b * nrb + r, 0)),
                  pl.BlockSpec((s_len, 256), lambda b, r: (b, 0)),
                  pl.BlockSpec((s_len, 256), lambda b, r: (b, 0)),
                  pl.BlockSpec((l_len, 256), lambda b, r: (ctx0 + b, 0)),
                  pl.BlockSpec((l_len, 256), lambda b, r: (ctx0 + b, 0)),
                  pl.BlockSpec((1, NA_HEADS, tq, NA_KROWS * GRID_W), lambda b, r: (variant(r), layer, 0, 0))],
        out_specs=pl.BlockSpec((tq, 256), lambda b, r: (b * nrb + r, 0)),
        out_shape=jax.ShapeDtypeStruct((nb * s_len, 256), F32),
        compiler_params=_cparams(("parallel", "arbitrary"), 56),
        name="na_attn",
    )(qna, kna, vna, kna, vna, bias)


def _na_bias_tables(rpb, rows):
    a = np.arange(NA_QROWS)
    qc = np.arange(GRID_W)
    kr_rel = np.arange(NA_KROWS)
    kc = np.arange(GRID_W)
    col0 = np.clip(qc - NA_KW // 2, 0, GRID_W - NA_KW)
    col_valid = (kc[None, :] >= col0[:, None]) & (kc[None, :] < col0[:, None] + NA_KW)
    col_off = np.clip(kc[None, :] - qc[:, None] + (NA_KW - 1), 0, 2 * NA_KW - 2)
    oh_col = (col_off[:, :, None] == np.arange(2 * NA_KW - 1)).astype(np.float32)
    big = 10 ** 6
    out = []
    for r_blk, k0, rows_eff in ((0, 0, big), (NA_QROWS, NA_KH // 2, big), (rows - NA_QROWS, rows - NA_KROWS, rows)):
        qr = r_blk + a
        r0 = np.clip(qr - NA_KH // 2, 0, rows_eff - NA_KH)
        kr = k0 + kr_rel
        row_valid = (kr[None, :] >= r0[:, None]) & (kr[None, :] < r0[:, None] + NA_KH)
        row_off = np.clip(kr[None, :] - qr[:, None] + (NA_KH - 1), 0, 2 * NA_KH - 2)
        oh_row = (row_off[:, :, None] == np.arange(2 * NA_KH - 1)).astype(np.float32)
        b = jnp.einsum('akr,hrc,qlc->haqkl', oh_row, rpb, oh_col, precision=lax.Precision.HIGHEST)
        valid = row_valid[:, None, :, None] & col_valid[None, :, None, :]
        b = jnp.where(valid[None], b * LOG2E, NEG)
        out.append(b.reshape(rpb.shape[0], NA_QROWS * GRID_W, NA_KROWS * GRID_W))
    return jnp.stack(out)


def _s5_kernel(*refs, nb):
    uf_refs, ub_refs = refs[:nb], refs[nb:2 * nb]
    (bf_ref, bb_ref, a_ref, cf_ref, cb_ref, d_ref, yf_ref, yb_ref,
     uf_sc, ub_sc, buf_sc, bub_sc, hf_sc, hb_sc) = refs[2 * nb:]
    j = pl.program_id(0)
    tc = uf_refs[0].shape[0]
    half = S5_GROUPS * S5_STATE

    @pl.when(j == 0)
    def _():
        hf_sc[...] = jnp.zeros(hf_sc.shape, F32)
        hb_sc[...] = jnp.zeros(hb_sc.shape, F32)

    for b in range(nb):
        for hv in range(2):
            uf_sc[hv, pl.ds(b, tc, stride=nb), :] = uf_refs[b][:, hv * LANES:(hv + 1) * LANES]
            ub_sc[hv, pl.ds(b, tc, stride=nb), :] = ub_refs[b][:, hv * LANES:(hv + 1) * LANES]

    def scan(u_sc, b_ref, c_ref, y_ref, bu_sc, h_sc, d, reverse):
        u = jnp.concatenate([u_sc[0], u_sc[1]], axis=1)
        bu_sc[...] = _dot(u.astype(BF16), b_ref[...])
        ar = jnp.broadcast_to(a_ref[d, 0:1, :], (nb, half))
        ai = jnp.broadcast_to(a_ref[d, 1:2, :], (nb, half))

        def step(t, carry):
            hr, hi = carry
            tt = (tc - 1 - t) if reverse else t
            row = pl.multiple_of(tt * nb, nb)
            nr = ar * hr - ai * hi + bu_sc[pl.ds(row, nb), 0:half]
            ni = ar * hi + ai * hr + bu_sc[pl.ds(row, nb), half:2 * half]
            bu_sc[pl.ds(row, nb), 0:half] = nr
            bu_sc[pl.ds(row, nb), half:2 * half] = ni
            return nr, ni

        hr, hi = lax.fori_loop(0, tc, step, (h_sc[:, 0:half], h_sc[:, half:2 * half]), unroll=True)
        h_sc[:, 0:half] = hr
        h_sc[:, half:2 * half] = hi
        y = _dot(bu_sc[...].astype(BF16), c_ref[...])
        y_ref[...] = y if reverse else y + d_ref[...] * u

    scan(uf_sc, bf_ref, cf_ref, yf_ref, buf_sc, hf_sc, 0, False)
    scan(ub_sc, bb_ref, cb_ref, yb_ref, bub_sc, hb_sc, 1, True)


def _s5_scan(z, bmat, avec, cmat, d_skip, *, nb, s_len, l_len):
    tc = S5_TC
    cr = tc * nb
    nctx, nlat = l_len // tc, s_len // tc
    nchunk = nctx + nlat
    half = S5_GROUPS * S5_STATE
    n_lat = nb * s_len

    def bwd(j):
        return jnp.where(j < nctx, nctx - 1 - j, nchunk - 1 - (j - nctx))

    def u_spec(b, order):
        def row_block(j):
            c = order(j)
            return jnp.where(c < nctx, (n_lat + b * l_len) // tc + c, (b * s_len) // tc + c - nctx)
        return pl.BlockSpec((tc, 256), lambda j: (row_block(j), 0))

    const = lambda shape: pl.BlockSpec(shape, lambda j: (0,) * len(shape))
    return pl.pallas_call(
        functools.partial(_s5_kernel, nb=nb),
        grid=(nchunk,),
        in_specs=[u_spec(b, lambda j: j) for b in range(nb)] + [u_spec(b, bwd) for b in range(nb)]
                 + [const((256, 2 * half)), const((256, 2 * half)), const((2, 2, half)),
                    const((2 * half, 256)), const((2 * half, 256)), const((1, 256))],
        out_specs=[pl.BlockSpec((cr, 256), lambda j: (j, 0)),
                   pl.BlockSpec((cr, 256), lambda j: (bwd(j), 0))],
        out_shape=[jax.ShapeDtypeStruct((nchunk * cr, 256), F32)] * 2,
        scratch_shapes=[pltpu.VMEM((2, cr, LANES), F32), pltpu.VMEM((2, cr, LANES), F32),
                        pltpu.VMEM((cr, 2 * half), F32), pltpu.VMEM((cr, 2 * half), F32),
                        pltpu.VMEM((nb, 2 * half), F32), pltpu.VMEM((nb, 2 * half), F32)],
        compiler_params=_cparams(("arbitrary",)),
        name="s5_scan",
    )(*([z] * (2 * nb)), bmat[0], bmat[1], avec, cmat[0], cmat[1], d_skip)


def _s5_finish_kernel(yf_ref, yb_ref, w_ref, o_ref, r_sc):
    nb, steps = o_ref.shape[0], o_ref.shape[1]
    r = _dot((yf_ref[...] + yb_ref[...]).astype(BF16), w_ref[...])
    o = r[:, :BRANCH_W] * jax.nn.sigmoid(r[:, BRANCH_W:])
    for hv in range(2):
        r_sc[hv] = o[:, hv * LANES:(hv + 1) * LANES]
    for b in range(nb):
        for hv in range(2):
            o_ref[b, :, hv * LANES:(hv + 1) * LANES] = r_sc[hv, pl.ds(b, steps, stride=nb), :]


def _s5_finish(yf, yb, w_glu, *, nb, t0, t_len):
    steps = TM // nb
    tile0 = t0 // steps
    spec = pl.BlockSpec((TM, 256), lambda i: (tile0 + i, 0))
    out = pl.pallas_call(
        _s5_finish_kernel,
        grid=(t_len // steps,),
        in_specs=[spec, spec, pl.BlockSpec((256, 512), lambda i: (0, 0))],
        out_specs=pl.BlockSpec((nb, steps, 256), lambda i: (0, i, 0)),
        out_shape=jax.ShapeDtypeStruct((nb, t_len, 256), F32),
        scratch_shapes=[pltpu.VMEM((2, TM, LANES), F32)],
        compiler_params=_cparams(("parallel",)),
        name="s5_glu",
    )(yf, yb, w_glu)
    return out.reshape(nb * t_len, 256)


def _s5_params(lam_re, lam_im, log_dt, b_re, b_im, c_re, c_im):
    lam = lax.complex(lam_re, lam_im)
    dt = jnp.exp(log_dt)[..., None]
    a_bar = jnp.exp(lam * dt)
    b_bar = ((a_bar - 1.0) / lam)[..., None] * lax.complex(b_re, b_im)
    eye = jnp.eye(S5_GROUPS, dtype=F32)
    g, p, ch = S5_GROUPS, S5_STATE, S5_GROUP_CH

    def bdiag_in(m):
        return jnp.einsum('gh,dgpc->dgchp', eye, m).reshape(2, g * ch, g * p)

    def bdiag_out(m):
        return jnp.einsum('gh,dgcp->dgphc', eye, m).reshape(2, g * p, g * ch)

    bmat = jnp.concatenate([bdiag_in(jnp.real(b_bar)), bdiag_in(jnp.imag(b_bar))], axis=2).astype(BF16)
    cmat = jnp.concatenate([bdiag_out(c_re), -bdiag_out(c_im)], axis=1).astype(BF16)
    avec = jnp.stack([jnp.real(a_bar).reshape(2, g * p), jnp.imag(a_bar).reshape(2, g * p)], axis=1)
    return bmat, avec, cmat


def _merge_kernel(*refs, n_lat_tiles, has_ctx):
    x_ref, mod_ref = refs[0], refs[1]
    n_br = 8 if has_ctx else 4
    br_refs = refs[2:2 + n_br]
    wg_ref, bg_ref, wb_ref, wo_ref, g_ref, b_ref, x1_ref, h2_ref = refs[2 + n_br:]
    d = D_MODEL
    m = mod_ref[0]
    x = x_ref[...]
    hb = (x * (1.0 + m[1:2]) + m[0:1]).astype(BF16)
    is_ctx = pl.program_id(0) >= n_lat_tiles
    acc = None
    for i in range(4):
        if has_ctx:
            o = jnp.where(is_ctx, br_refs[2 * i + 1][...], br_refs[2 * i][...])
        else:
            o = br_refs[i][...]
        gate = jax.nn.sigmoid(_dot(hb, wg_ref[0, :, i * d:(i + 1) * d]) + bg_ref[:, i * d:(i + 1) * d])
        term = gate * _dot(o.astype(BF16), wb_ref[0, i])
        acc = term if acc is None else acc + term
    y = _dot(acc.astype(BF16), wo_ref[0])
    x1 = _layer_norm(ALPHA * x + m[2:3] * y, g_ref[...], b_ref[...])
    x1_ref[...] = x1
    h2_ref[...] = x1 * (1.0 + m[4:5]) + m[3:4]


def _merge(xa, modl, outs_lat, outs_ctx, wg, bg, wb, wo, g, b, seg, n, n_lat_tiles, layer):
    d = D_MODEL
    tok = pl.BlockSpec((TM, d), lambda i: (i, 0))
    lat = pl.BlockSpec((TM, 256), lambda i: (jnp.minimum(i, n_lat_tiles - 1), 0))
    ctx = pl.BlockSpec((TM, 256), lambda i: (jnp.maximum(i - n_lat_tiles, 0), 0))

    def const(a):
        return pl.BlockSpec(a.shape, lambda i: (0,) * a.ndim)

    def stacked(a):
        return pl.BlockSpec((1,) + a.shape[1:], lambda i: (layer,) + (0,) * (a.ndim - 1))
    has_ctx = outs_ctx is not None
    if has_ctx:
        branches = [a for pair in zip(outs_lat, outs_ctx) for a in pair]
        br_specs = [lat, ctx] * 4
    else:
        branches, br_specs = list(outs_lat), [lat] * 4
    return pl.pallas_call(
        functools.partial(_merge_kernel, n_lat_tiles=n_lat_tiles, has_ctx=has_ctx),
        grid=(n // TM,),
        in_specs=[tok, pl.BlockSpec((1, 6, d), lambda i: (seg(i), 0, 0))] + br_specs
                 + [stacked(wg), const(bg), stacked(wb), stacked(wo), const(g), const(b)],
        out_specs=[tok, tok],
        out_shape=[jax.ShapeDtypeStruct((n, d), F32)] * 2,
        compiler_params=_cparams(("parallel",), 56),
        name="merge_ln1",
    )(xa, modl, *branches, wg, bg, wb, wo, g, b)


def _start_row_gather(src_hbm, dst, idx_ref, base, n_rows, sem):
    def body(r, c):
        pltpu.make_async_copy(src_hbm.at[pl.ds(idx_ref[base + r], 1)], dst.at[pl.ds(r, 1)], sem).start()
        return c

    lax.fori_loop(0, n_rows, body, 0, unroll=8)


def _wait_row_gather(src_hbm, dst, sem):
    pltpu.make_async_copy(src_hbm.at[pl.ds(0, dst.shape[0])], dst, sem).wait()


def _swiglu_block(xb, w1_ref, w3_ref, w2_ref, o_ref, between=None):
    lead = (0,) * (len(w1_ref.shape) - 2)
    n_chunks = D_FF // FF_CHUNK
    for c in range(n_chunks):
        cs = slice(c * FF_CHUNK, (c + 1) * FF_CHUNK)
        a = _dot(xb, w1_ref[lead + (slice(None), cs)])
        b = _dot(xb, w3_ref[lead + (slice(None), cs)])
        g = (a * jax.nn.sigmoid(a) * b).astype(BF16)
        y = _dot(g, w2_ref[lead + (cs, slice(None))])
        if c == 0:
            o_ref[...] = y
        else:
            o_ref[...] += y
        if between is not None:
            between(c, n_chunks)


def _ffn_kernel(h_ref, x_ref, mod_ref, w1_ref, w3_ref, w2_ref, g_ref, b_ref, o_ref, acc_sc):
    _swiglu_block(h_ref[...].astype(BF16), w1_ref, w3_ref, w2_ref, acc_sc)
    m = mod_ref[0]
    o_ref[...] = _layer_norm(ALPHA * x_ref[...] + m[5:6] * acc_sc[...], g_ref[...], b_ref[...])


def _moe_ffn_kernel(be_ref, tok_ref, start_ref, nused_ref, h_hbm, w1_ref, w3_ref, w2_ref, o_ref, xbuf, sem):
    i = pl.program_id(0)
    last = pl.num_programs(0) - 1
    slot = i % 2
    nxt_base = start_ref[jnp.minimum(i + 1, last)]
    nxt_buf, nxt_sem = xbuf.at[1 - slot], sem.at[1 - slot]

    @pl.when(i == 0)
    def _():
        _start_row_gather(h_hbm, xbuf.at[0], tok_ref, start_ref[0], MOE_BLK, sem.at[0])

    _wait_row_gather(h_hbm, xbuf.at[slot], sem.at[slot])

    def start_piece(c, n_chunks):
        per = -(-MOE_BLK // n_chunks)
        for r in range(c * per, min((c + 1) * per, MOE_BLK)):
            pltpu.make_async_copy(h_hbm.at[pl.ds(tok_ref[nxt_base + r], 1)], nxt_buf.at[pl.ds(r, 1)],
                                  nxt_sem).start()

    @pl.when(i < nused_ref[0])
    def _():
        _swiglu_block(xbuf[slot].astype(BF16), w1_ref, w3_ref, w2_ref, o_ref, between=start_piece)

    @pl.when(i >= nused_ref[0])
    def _():
        o_ref[...] = jnp.zeros(o_ref.shape, F32)
        _start_row_gather(h_hbm, nxt_buf, tok_ref, nxt_base, MOE_BLK, nxt_sem)

    @pl.when(i == last)
    def _():
        _wait_row_gather(h_hbm, nxt_buf, nxt_sem)


def _moe_ffn(h, block_e, tok_sorted, blk_start, n_used, w1, w3, w2, layer):
    d = h.shape[1]
    n_blocks = block_e.shape[0]
    wspec = lambda shape: pl.BlockSpec(shape, lambda i, be, tok, st, nu: (layer, be[i], 0, 0),
                                       pipeline_mode=pl.Buffered(1))
    grid_spec = pltpu.PrefetchScalarGridSpec(
        num_scalar_prefetch=4,
        grid=(n_blocks,),
        in_specs=[pl.BlockSpec(memory_space=pl.ANY), wspec((1, 1, d, D_FF)), wspec((1, 1, d, D_FF)),
                  wspec((1, 1, D_FF, d))],
        out_specs=pl.BlockSpec((MOE_BLK, d), lambda i, be, tok, st, nu: (i, 0)),
        scratch_shapes=[pltpu.VMEM((2, MOE_BLK, d), F32), pltpu.SemaphoreType.DMA((2,))],
    )
    return pl.pallas_call(
        _moe_ffn_kernel,
        grid_spec=grid_spec,
        out_shape=jax.ShapeDtypeStruct((n_blocks * MOE_BLK, d), F32),
        compiler_params=_cparams(("arbitrary",), 56),
        name="moe_swiglu",
    )(block_e, tok_sorted, blk_start, n_used, h, w1, w3, w2)


def _ffn(h2, x1, modl, w1, w3, w2, g, b, seg, layer):
    n, d = h2.shape
    tok = pl.BlockSpec((TM, d), lambda i: (i, 0))
    vec = pl.BlockSpec((1, d), lambda i: (0, 0))
    wspec = lambda shape: pl.BlockSpec(shape, lambda i: (layer, 0, 0), pipeline_mode=pl.Buffered(1))
    return pl.pallas_call(
        _ffn_kernel,
        grid=(n // TM,),
        in_specs=[tok, tok, pl.BlockSpec((1, 6, d), lambda i: (seg(i), 0, 0)),
                  wspec((1, d, D_FF)), wspec((1, d, D_FF)), wspec((1, D_FF, d)), vec, vec],
        out_specs=tok,
        out_shape=jax.ShapeDtypeStruct((n, d), F32),
        scratch_shapes=[pltpu.VMEM((TM, d), F32)],
        compiler_params=_cparams(("parallel",), 56),
        name="swiglu_ln2",
    )(h2, x1, modl, w1, w3, w2, g, b)


def _router_kernel(h_ref, w_ref, o_ref):
    logits = jnp.dot(h_ref[...], w_ref[...], preferred_element_type=F32, precision=lax.Precision.HIGHEST)
    lane = lax.broadcasted_iota(jnp.int32, logits.shape, 1)
    lg = jnp.where(lane < N_EXPERTS, logits, NEG)
    v1 = jnp.max(lg, axis=-1, keepdims=True)
    i1 = jnp.min(jnp.where(lg == v1, lane, LANES), axis=-1, keepdims=True)
    lg2 = jnp.where(lane == i1, NEG, lg)
    v2 = jnp.max(lg2, axis=-1, keepdims=True)
    i2 = jnp.min(jnp.where(lg2 == v2, lane, LANES), axis=-1, keepdims=True)
    e = jnp.exp(v2 - v1)
    g1 = 1.0 / (1.0 + e)
    g2 = e / (1.0 + e)
    out = jnp.where(lane == 0, i1.astype(F32), jnp.where(lane == 1, i2.astype(F32),
                    jnp.where(lane == 2, g1, jnp.where(lane == 3, g2, 0.0))))
    o_ref[...] = out


def _router(h2, w_router_pad):
    n, d = h2.shape
    return pl.pallas_call(
        _router_kernel,
        grid=(n // TM,),
        in_specs=[pl.BlockSpec((TM, d), lambda i: (i, 0)), pl.BlockSpec((d, LANES), lambda i: (0, 0))],
        out_specs=pl.BlockSpec((TM, LANES), lambda i: (i, 0)),
        out_shape=jax.ShapeDtypeStruct((n, LANES), F32),
        compiler_params=_cparams(("parallel",)),
        name="router",
    )(h2, w_router_pad)


def _moe_combine_kernel(dest_ref, x_ref, mod_ref, gt_ref, y_hbm, g_ref, b_ref, o_ref, ybuf, sem):
    i = pl.program_id(0)
    last = pl.num_programs(0) - 1
    slot = i % 2
    nxt_base = jnp.minimum(i + 1, last) * 2 * TM
    nxt_buf, nxt_sem = ybuf.at[1 - slot], sem.at[1 - slot]

    @pl.when(i == 0)
    def _():
        _start_row_gather(y_hbm, ybuf.at[0], dest_ref, 0, 2 * TM, sem.at[0])

    for r in range(2 * TM):
        pltpu.make_async_copy(y_hbm.at[pl.ds(dest_ref[nxt_base + r], 1)], nxt_buf.at[pl.ds(r, 1)], nxt_sem).start()

    _wait_row_gather(y_hbm, ybuf.at[slot], sem.at[slot])
    gt = gt_ref[...]
    f = gt[:, 2:3] * ybuf[slot, 0:TM, :] + gt[:, 3:4] * ybuf[slot, TM:2 * TM, :]
    m = mod_ref[0]
    o_ref[...] = _layer_norm(ALPHA * x_ref[...] + m[5:6] * f, g_ref[...], b_ref[...])

    @pl.when(i == last)
    def _():
        _wait_row_gather(y_hbm, nxt_buf, nxt_sem)


def _moe_combine(x1, modl, y_rows, dest_tiles, gates, g, b, seg):
    n, d = x1.shape
    tok = pl.BlockSpec((TM, d), lambda i, dst: (i, 0))
    vec = pl.BlockSpec((1, d), lambda i, dst: (0, 0))
    grid_spec = pltpu.PrefetchScalarGridSpec(
        num_scalar_prefetch=1,
        grid=(n // TM,),
        in_specs=[tok, pl.BlockSpec((1, 6, d), lambda i, dst: (seg(i), 0, 0)),
                  pl.BlockSpec((TM, LANES), lambda i, dst: (i, 0)), pl.BlockSpec(memory_space=pl.ANY), vec, vec],
        out_specs=tok,
        scratch_shapes=[pltpu.VMEM((2, 2 * TM, d), F32), pltpu.SemaphoreType.DMA((2,))],
    )
    return pl.pallas_call(
        _moe_combine_kernel,
        grid_spec=grid_spec,
        out_shape=jax.ShapeDtypeStruct((n, d), F32),
        compiler_params=_cparams(("arbitrary",)),
        name="moe_combine_ln2",
    )(dest_tiles, x1, modl, gates, y_rows, g, b)


def _rope_tables(s_len, width, rope_lane, extra_rows):
    t = jnp.arange(s_len)
    lane = np.arange(width)
    inv = ROPE_BASE ** (-(jnp.asarray(lane % 8, F32)) / 8.0)
    pos = jnp.where((lane & 16) == 0, (t // GRID_W)[:, None], (t % GRID_W)[:, None]).astype(F32)
    ang = pos * inv[None, :]
    cos = jnp.where(rope_lane[None, :], jnp.cos(ang), 1.0)
    sin = jnp.where(rope_lane[None, :], jnp.where((lane & 8) == 0, -jnp.sin(ang), jnp.sin(ang)), 0.0)
    cos = jnp.concatenate([cos, jnp.ones((extra_rows, width), F32)])
    sin = jnp.concatenate([sin, jnp.zeros((extra_rows, width), F32)])
    return cos, sin


def _pad_heads(w, n_heads, width):
    k = w.shape[0]
    w = w.reshape(k, n_heads, -1)
    return jnp.pad(w, ((0, 0), (0, 0), (0, width - w.shape[-1]))).reshape(k, n_heads * width)


def _route(top, n_tok):
    e_flat = jnp.concatenate([top[:, 0], top[:, 1]]).astype(jnp.int32)
    n_assign = 2 * n_tok
    onehot = (e_flat[:, None] == jnp.arange(N_EXPERTS)[None, :]).astype(jnp.int32)
    csum = jnp.cumsum(onehot, axis=0)
    counts = csum[-1]
    padded = (counts + MOE_BLK - 1) // MOE_BLK * MOE_BLK
    pad_end = jnp.cumsum(padded)
    pad_start = pad_end - padded
    dest = jnp.sum(onehot * (csum - 1 + pad_start[None, :]), axis=1)
    n_blocks = -(-n_assign // MOE_BLK) + N_EXPERTS
    blk_row0 = jnp.arange(n_blocks) * MOE_BLK
    block_e = jnp.minimum(jnp.searchsorted(pad_end, blk_row0, side='right'), N_EXPERTS - 1).astype(jnp.int32)
    n_used = (pad_end[-1:] // MOE_BLK).astype(jnp.int32)
    tok_sorted = jnp.concatenate([(jnp.argsort(e_flat) % n_tok).astype(jnp.int32), jnp.zeros((MOE_BLK,), jnp.int32)])
    seg_start = jnp.cumsum(counts) - counts
    blk_start = jnp.clip(seg_start[block_e] + blk_row0 - pad_start[block_e], 0, n_assign).astype(jnp.int32)
    dest_tiles = dest.astype(jnp.int32).reshape(2, n_tok // TM, TM).transpose(1, 0, 2).reshape(-1)
    return tok_sorted, blk_start, block_e, n_used, dest_tiles


def kernel(x, c, ctx, c_ctx, w_ada, b_ada, w_in, na_rpb, mla_q_norm, mla_kv_norm, mla_w_uq, mla_w_ukv,
           s5_lam_re, s5_lam_im, s5_log_dt, s5_b_re, s5_b_im, s5_c_re, s5_c_im, s5_d, s5_w_glu,
           diff_lam_q1, diff_lam_k1, diff_lam_q2, diff_lam_k2, diff_subln,
           w_branch, w_gate, b_gate, w_out, ln1_g, ln1_b, ln2_g, ln2_b,
           ffn_w1, ffn_w3, ffn_w2, moe_router, moe_w1, moe_w3, moe_w2):
    nb, s_len, d = x.shape
    l_len = ctx.shape[1]
    n_lat, n_ctx = nb * s_len, nb * l_len
    n_tot = n_lat + n_ctx
    rows = s_len // GRID_W
    assert d == D_MODEL and s_len % TM == 0 and n_ctx % TM == 0 and rows % NA_QROWS == 0 and rows >= NA_KROWS
    assert l_len % S5_TC == 0 and s_len % S5_TC == 0

    def seg(i):
        return jnp.minimum((i * TM) // s_len, nb)

    n_lat_tiles = n_lat // TM

    xa = jnp.concatenate([x.reshape(n_lat, d), ctx.reshape(n_ctx, d)], axis=0)
    nrow_mod = -(-(nb + 1) // 16) * 16
    cvec = jnp.concatenate([c, c_ctx[None, :], jnp.zeros((nrow_mod - nb - 1, d), F32)], axis=0)
    mod_all = _ada_all(cvec, w_ada, b_ada).reshape(DEPTH, nrow_mod, 6, d)

    lane512 = np.arange(512)
    q_rope = (lane512 % LANES >= MLA_NOPE) & (lane512 % LANES < MLA_NOPE + MLA_ROPE)
    cq, sq = _rope_tables(s_len, 512, q_rope, TM)
    ck, sk = _rope_tables(s_len, LANES, np.arange(LANES) < MLA_ROPE, TM)
    cd, sd = _rope_tables(s_len, 256, np.ones((256,), bool), TM)
    tabs = (cq, sq, ck, sk, cd, sd)
    place = np.zeros((LANES, MLA_HEADS * LANES), np.float32)
    for h in range(MLA_HEADS):
        place[np.arange(MLA_ROPE), h * LANES + MLA_NOPE + np.arange(MLA_ROPE)] = 1.0
    place = jnp.asarray(place, BF16)

    w_gate_b, w_branch_b, w_out_b = w_gate.astype(BF16), w_branch.astype(BF16), w_out.astype(BF16)
    ffn_b = (ffn_w1.astype(BF16), ffn_w3.astype(BF16), ffn_w2.astype(BF16))
    moe_b = (moe_w1.astype(BF16), moe_w3.astype(BF16), moe_w2.astype(BF16))
    bias_all = _na_bias_tables(na_rpb.reshape(DEPTH * NA_HEADS, 2 * NA_KH - 1, 2 * NA_KW - 1), rows)

    for layer in range(DEPTH):
        ctx_out = layer < DEPTH - 1
        lam_init = 0.8 - 0.6 * math.exp(-0.3 * layer)
        modl = mod_all[layer]

        wi = w_in[layer]
        w_in_pad = jnp.concatenate([wi[:, :1184], jnp.zeros((d, 96), F32), wi[:, 1184:]], axis=1).astype(BF16)
        wuq = _pad_heads(mla_w_uq[layer], MLA_HEADS, LANES).astype(BF16)
        wukv4 = mla_w_ukv[layer].reshape(-1, MLA_HEADS, MLA_NOPE + MLA_V)
        wuk = jnp.pad(wukv4[:, :, :MLA_NOPE], ((0, 0), (0, 0), (0, LANES - MLA_NOPE))).reshape(-1, MLA_HEADS * LANES)
        wuv = wukv4[:, :, MLA_NOPE:].reshape(-1, MLA_HEADS * MLA_V)
        wukv = jnp.concatenate([wuk, wuv], axis=1).astype(BF16)
        bmat, avec, cmat = _s5_params(s5_lam_re[layer], s5_lam_im[layer], s5_log_dt[layer], s5_b_re[layer],
                                      s5_b_im[layer], s5_c_re[layer], s5_c_im[layer])
        lam = (jnp.exp(jnp.sum(diff_lam_q1[layer] * diff_lam_k1[layer]))
               - jnp.exp(jnp.sum(diff_lam_q2[layer] * diff_lam_k2[layer])) + lam_init)
        lam_vec = jnp.full((1, LANES), lam, F32)
        subln = jnp.tile(diff_subln[layer], DIFF_HEADS).reshape(1, 256)

        qm, km, vm, qd, kd, vd, qna, kna, vna, u5 = _proj(
            xa, modl, w_in_pad, tabs, mla_q_norm[layer].reshape(1, -1), mla_kv_norm[layer].reshape(1, -1),
            wuq, wukv, place, nb=nb, s_len=s_len)
        common = dict(nb=nb, s_len=s_len, l_len=l_len)
        o_na = _na_attention(qna, kna, vna, bias_all, layer=layer, **common)
        o_mla = _flash(qm, km, vm, heads=MLA_HEADSPEC, kw=512, name="mla_attn", **common)
        o_diff = _flash(qd, kd, vd, heads=DIFF_HEADSPEC, kw=256, diff=True, lam=lam_vec,
                        subln=subln, lam_init=lam_init, name="diff_attn", **common)
        yf, yb = _s5_scan(u5, bmat, avec, cmat, s5_d[layer].reshape(1, 256), **common)
        w_glu = s5_w_glu[layer].astype(BF16)
        o_s5_lat = _s5_finish(yf, yb, w_glu, nb=nb, t0=l_len, t_len=s_len)
        if ctx_out:
            o_na_c = _flash(qna, kna, vna, heads=NA_HEADSPEC, kw=256, with_lat=False, name="na_ctx_attn", **common)
            o_mla_c = _flash(qm, km, vm, heads=MLA_HEADSPEC, kw=512, with_lat=False,
                             name="mla_ctx_attn", **common)
            o_diff_c = _flash(qd, kd, vd, heads=DIFF_HEADSPEC, kw=256, diff=True, lam=lam_vec,
                              subln=subln, lam_init=lam_init, with_lat=False, name="diff_ctx_attn", **common)
            o_s5_c = _s5_finish(yf, yb, w_glu, nb=nb, t0=0, t_len=l_len)
            outs_ctx = [o_na_c, o_mla_c, o_s5_c, o_diff_c]
            n_act = n_tot
        else:
            outs_ctx = None
            n_act = n_lat
        x1, h2 = _merge(xa, modl, [o_na, o_mla, o_s5_lat, o_diff], outs_ctx,
                        w_gate_b, b_gate[layer].reshape(1, -1), w_branch_b, w_out_b,
                        ln1_g[layer].reshape(1, d), ln1_b[layer].reshape(1, d), seg, n_act, n_lat_tiles, layer)

        jj = layer // 2
        g2, b2 = ln2_g[layer].reshape(1, d), ln2_b[layer].reshape(1, d)
        if layer % 2 == 0:
            xa = _ffn(h2, x1, modl, *ffn_b, g2, b2, seg, jj)
        else:
            wr = jnp.pad(moe_router[jj], ((0, 0), (0, LANES - N_EXPERTS)))
            top = _router(h2, wr)
            tok_sorted, blk_start, block_e, n_used, dest_tiles = _route(top, n_act)
            y_rows = _moe_ffn(h2, block_e, tok_sorted, blk_start, n_used, *moe_b, jj)
            xa = _moe_combine(x1, modl, y_rows, dest_tiles, top, g2, b2, seg)
    return xa[:n_lat].reshape(nb, s_len, d)
```

```python
import functools
import math

import jax
import jax.numpy as jnp
import numpy as np
from jax import lax
from jax.experimental import pallas as pl
from jax.experimental.pallas import tpu as pltpu

F32 = jnp.float32
BF16 = jnp.bfloat16

D_MODEL = 1024
DEPTH = 4
GRID_W = 64
BRANCH_W = 256
NA_HEADS = 4
NA_HD = 64
NA_KH = 8
NA_KW = 16
MLA_HEADS = 4
MLA_NOPE = 64
MLA_ROPE = 32
MLA_V = 64
S5_GROUP_CH = 16
S5_GROUPS = 16
S5_STATE = 64
DIFF_HEADS = 4
DIFF_HD = 32
D_FF = 3584
N_EXPERTS = 8
ROPE_BASE = 10000.0
LN_EPS = 1e-5
RMS_EPS = 1e-6
ALPHA = (2 * DEPTH) ** 0.25
NA_SCALE = NA_HD ** -0.5
MLA_SCALE = (MLA_NOPE + MLA_ROPE) ** -0.5
DIFF_SCALE = DIFF_HD ** -0.5
LOG2E = 1.4426950408889634

LANES = 128
ZW = 2304
Z_NAQ, Z_NAK, Z_NAV, Z_MQ, Z_MKV, Z_MKR, Z_S5, Z_DQ, Z_DK, Z_DV = (
    0, 256, 512, 768, 1024, 1152, 1280, 1536, 1792, 2048)
NEG = -1e30
TM = 512
TQ = 1024
TK = 256
NA_QROWS = 4
NA_KROWS = NA_QROWS + NA_KH
S5_TC = 128
FF_CHUNK = 512
MOE_BLK = 512


def _cparams(sem, vmem_mb=48):
    return pltpu.CompilerParams(dimension_semantics=sem, vmem_limit_bytes=vmem_mb << 20)


def _dot(a, b):
    return jnp.dot(a, b, preferred_element_type=F32)


def _dot_nt(a, b):
    return lax.dot_general(a, b, (((1,), (1,)), ((), ())), preferred_element_type=F32)


def _layer_norm(r, g, b):
    rc = r - jnp.mean(r, axis=-1, keepdims=True)
    var = jnp.mean(rc * rc, axis=-1, keepdims=True)
    return rc * lax.rsqrt(var + LN_EPS) * g + b


def _ada_kernel(c_ref, w_ref, b_ref, o_ref):
    c = c_ref[...]
    cond = c * jax.nn.sigmoid(c)
    o_ref[0] = _dot(cond.astype(BF16), w_ref[0].astype(BF16)) + b_ref[0]


def _ada_all(cvec, w_ada, b_ada):
    nrow = cvec.shape[0]
    d = D_MODEL
    return pl.pallas_call(
        _ada_kernel,
        grid=(DEPTH, 6),
        in_specs=[pl.BlockSpec((nrow, d), lambda l, n: (0, 0)),
                  pl.BlockSpec((1, d, d), lambda l, n: (l, 0, n)),
                  pl.BlockSpec((1, 1, d), lambda l, n: (l, 0, n))],
        out_specs=pl.BlockSpec((1, nrow, d), lambda l, n: (l, 0, n)),
        out_shape=jax.ShapeDtypeStruct((DEPTH, nrow, 6 * d), F32),
        compiler_params=_cparams(("parallel", "parallel")),
        name="ada_mod",
    )(cvec, w_ada, b_ada.reshape(DEPTH, 1, 6 * d))


def _swap8(z):
    w = z.shape[-1]
    lane = lax.broadcasted_iota(jnp.int32, z.shape, 1)
    up = pltpu.roll(z, w - 8, 1)
    dn = pltpu.roll(z, 8, 1)
    return jnp.where((lane & 8) == 0, up, dn)


def _rms(z, g):
    return z * lax.rsqrt(jnp.mean(z * z, axis=-1, keepdims=True) + RMS_EPS) * g


def _proj_kernel(x_ref, mod_ref, w_ref, cq_ref, sq_ref, ck_ref, sk_ref, cd_ref, sd_ref,
                 qn_ref, kvn_ref, wuq_ref, wukv_ref, place_ref,
                 qm_ref, km_ref, vm_ref, qd_ref, kd_ref, vd_ref, qna_ref, kna_ref, vna_ref, u_ref):
    m = mod_ref[0]
    h = x_ref[...] * (1.0 + m[1:2]) + m[0:1]
    z = _dot(h.astype(BF16), w_ref[...])
    kna_ref[...] = z[:, Z_NAK:Z_NAK + 256].astype(BF16)
    vna_ref[...] = z[:, Z_NAV:Z_NAV + 256].astype(BF16)
    u_ref[...] = z[:, Z_S5:Z_S5 + 256]
    aq = _rms(z[:, Z_MQ:Z_MQ + 256], qn_ref[...])
    q = _dot(aq.astype(BF16), wuq_ref[...])
    q = q * cq_ref[...] + _swap8(q) * sq_ref[...]
    qm_ref[...] = (q * (MLA_SCALE * LOG2E)).astype(BF16)
    akv = _rms(z[:, Z_MKV:Z_MKV + 128], kvn_ref[...])
    kv = _dot(akv.astype(BF16), wukv_ref[...])
    kr = z[:, Z_MKR:Z_MKR + 128]
    kr = kr * ck_ref[...] + _swap8(kr) * sk_ref[...]
    km_ref[...] = (kv[:, :4 * LANES] + _dot(kr.astype(BF16), place_ref[...])).astype(BF16)
    vm_ref[...] = kv[:, 4 * LANES:].astype(BF16)
    dq = z[:, Z_DQ:Z_DQ + 256]
    dq = (dq * cd_ref[...] + _swap8(dq) * sd_ref[...]) * (DIFF_SCALE * LOG2E)
    dk = z[:, Z_DK:Z_DK + 256]
    dk = dk * cd_ref[...] + _swap8(dk) * sd_ref[...]
    kd_ref[...] = dk.astype(BF16)
    vd_ref[...] = z[:, Z_DV:Z_DV + 256].astype(BF16)
    lane = lax.broadcasted_iota(jnp.int32, (dq.shape[0], LANES), 1)
    for g in range(2 * DIFF_HEADS):
        blk = dq[:, (g // 4) * LANES:(g // 4 + 1) * LANES]
        qd_ref[:, g * LANES:(g + 1) * LANES] = jnp.where((lane // DIFF_HD) == (g % 4), blk, 0.0).astype(BF16)
    naq = z[:, Z_NAQ:Z_NAQ + 256] * (NA_SCALE * LOG2E)
    for hd in range(NA_HEADS):
        blk = naq[:, (hd // 2) * LANES:(hd // 2 + 1) * LANES]
        qna_ref[:, hd * LANES:(hd + 1) * LANES] = jnp.where((lane // NA_HD) == (hd % 2), blk, 0.0).astype(BF16)


def _proj(xa, modl, w_in_pad, tabs, qn, kvn, wuq, wukv, place, *, nb, s_len):
    n, d = xa.shape
    cq, sq, ck, sk, cd, sd = tabs
    tps = s_len // TM
    n_lat_tiles = nb * tps

    def tile(i):
        return jnp.where(i < n_lat_tiles, (i % nb) * tps + i // nb, i)

    def seg(i):
        return jnp.minimum((tile(i) * TM) // s_len, nb)

    def tab_idx(i):
        return jnp.where(i < n_lat_tiles, i // nb, tps)

    def tspec(width):
        return pl.BlockSpec((TM, width), lambda i: (tab_idx(i), 0))

    def wspec(a):
        return pl.BlockSpec(a.shape, lambda i: (0,) * a.ndim)

    def ospec(width):
        return pl.BlockSpec((TM, width), lambda i: (tile(i), 0))

    widths = [512, 512, 256, 1024, 256, 256, 512, 256, 256]
    return pl.pallas_call(
        _proj_kernel,
        grid=(n // TM,),
        in_specs=[pl.BlockSpec((TM, d), lambda i: (tile(i), 0)),
                  pl.BlockSpec((1, 6, d), lambda i: (seg(i), 0, 0)),
                  wspec(w_in_pad),
                  tspec(512), tspec(512), tspec(128), tspec(128), tspec(256), tspec(256),
                  wspec(qn), wspec(kvn), wspec(wuq), wspec(wukv), wspec(place)],
        out_specs=[ospec(w) for w in widths] + [ospec(256)],
        out_shape=[jax.ShapeDtypeStruct((n, w), BF16) for w in widths] + [jax.ShapeDtypeStruct((n, 256), F32)],
        compiler_params=_cparams(("parallel",), 56),
        name="in_proj_prep",
    )(xa, modl, w_in_pad, cq, sq, ck, sk, cd, sd, qn, kvn, wuq, wukv, place)


def _flash_kernel(*refs, heads, n_acc, with_lat, diff, lam_init, tk):
    refs = list(refs)
    q_ref = refs.pop(0)
    if with_lat:
        kl_ref, vl_ref = refs.pop(0), refs.pop(0)
    kc_ref, vc_ref = refs.pop(0), refs.pop(0)
    if diff:
        lam_ref, sub_ref = refs.pop(0), refs.pop(0)
    o_ref, m_sc, l_sc, acc_sc = refs
    tq = q_ref.shape[0]
    lane = lax.broadcasted_iota(jnp.int32, (tq, LANES), 1)
    lo_half = lane < 64

    def tile(k_ref, v_ref, rows):
        nkb = (rows.stop - rows.start if isinstance(rows, slice) else rows.size) // LANES
        for g, (kb, vb, vh, ai) in enumerate(heads):
            q = q_ref[:, g * LANES:(g + 1) * LANES]
            k = k_ref[rows, kb * LANES:(kb + 1) * LANES].astype(BF16)
            s = _dot_nt(q, k)
            blocks = [s[:, c * LANES:(c + 1) * LANES] for c in range(nkb)]
            mx = blocks[0]
            for blk in blocks[1:]:
                mx = jnp.maximum(mx, blk)
            m_prev = m_sc[g]
            m_new = jnp.maximum(m_prev, jnp.max(mx, axis=-1, keepdims=True))
            alpha = jnp.exp2(m_prev - m_new)
            ps = [jnp.exp2(blk - m_new) for blk in blocks]
            lsum = ps[0]
            for pb in ps[1:]:
                lsum = lsum + pb
            l_sc[g] = alpha * l_sc[g] + lsum
            m_sc[g] = m_new
            p = jnp.concatenate([pb.astype(BF16) for pb in ps], axis=1)
            v = v_ref[rows, vb * LANES:(vb + 1) * LANES].astype(BF16)
            pv = _dot(p, v)
            old = acc_sc[ai, :, vb * LANES:(vb + 1) * LANES]
            mine = lo_half if vh == 0 else jnp.logical_not(lo_half)
            acc_sc[ai, :, vb * LANES:(vb + 1) * LANES] = jnp.where(mine, alpha * old + pv, old)

    m_sc[...] = jnp.full(m_sc.shape, NEG, F32)
    l_sc[...] = jnp.zeros(l_sc.shape, F32)
    acc_sc[...] = jnp.zeros(acc_sc.shape, F32)
    tile(kc_ref, vc_ref, slice(0, kc_ref.shape[0]))

    if with_lat:
        def kv_step(j, carry):
            tile(kl_ref, vl_ref, pl.ds(pl.multiple_of(j * tk, tk), tk))
            return carry

        lax.fori_loop(0, kl_ref.shape[0] // tk, kv_step, 0, unroll=8)

    def inv_l(g):
        return 1.0 / jnp.sum(l_sc[g], axis=-1, keepdims=True)

    for c in range(2):
        if not diff:
            o = jnp.where(lo_half, acc_sc[0, :, c * LANES:(c + 1) * LANES] * inv_l(2 * c),
                          acc_sc[0, :, c * LANES:(c + 1) * LANES] * inv_l(2 * c + 1))
        else:
            o1 = jnp.where(lo_half, acc_sc[0, :, c * LANES:(c + 1) * LANES] * inv_l(4 * c),
                           acc_sc[0, :, c * LANES:(c + 1) * LANES] * inv_l(4 * c + 2))
            o2 = jnp.where(lo_half, acc_sc[1, :, c * LANES:(c + 1) * LANES] * inv_l(4 * c + 1),
                           acc_sc[1, :, c * LANES:(c + 1) * LANES] * inv_l(4 * c + 3))
            o = o1 - lam_ref[...] * o2
            sq = o * o
            ms_lo = jnp.sum(jnp.where(lo_half, sq, 0.0), axis=-1, keepdims=True) * (1.0 / 64)
            ms_hi = jnp.sum(jnp.where(lo_half, 0.0, sq), axis=-1, keepdims=True) * (1.0 / 64)
            rs = jnp.where(lo_half, lax.rsqrt(ms_lo + RMS_EPS), lax.rsqrt(ms_hi + RMS_EPS))
            o = o * rs * sub_ref[:, c * LANES:(c + 1) * LANES] * (1.0 - lam_init)
        o_ref[:, c * LANES:(c + 1) * LANES] = o


def _flash(q, k, v, *, nb, s_len, l_len, heads, kcol=0, vcol=0, kw=None, vw=256,
           with_lat=True, diff=False, lam=None, subln=None, lam_init=0.0, name="flash"):
    nh = len(heads)
    n_acc = 2 if diff else 1
    ctx0 = nb * s_len // l_len
    tk = min(TK, s_len)
    if with_lat:
        tq = min(TQ, s_len)
        grid = (nb, s_len // tq)
        rows = nb * s_len
        qmap = lambda b, i: (b * (s_len // tq) + i, 0)
    else:
        tq = l_len
        grid = (nb, 1)
        rows = nb * l_len
        qmap = lambda b, i: (ctx0 + b, 0)
    in_specs = [pl.BlockSpec((tq, nh * LANES), qmap)]
    args = [q]
    if with_lat:
        in_specs += [pl.BlockSpec((s_len, kw), lambda b, i: (b, kcol)),
                     pl.BlockSpec((s_len, vw), lambda b, i: (b, vcol))]
        args += [k, v]
    in_specs += [pl.BlockSpec((l_len, kw), lambda b, i: (ctx0 + b, kcol)),
                 pl.BlockSpec((l_len, vw), lambda b, i: (ctx0 + b, vcol))]
    args += [k, v]
    if diff:
        in_specs += [pl.BlockSpec((1, LANES), lambda b, i: (0, 0)),
                     pl.BlockSpec((1, 256), lambda b, i: (0, 0))]
        args += [lam, subln]
    return pl.pallas_call(
        functools.partial(_flash_kernel, heads=heads, n_acc=n_acc, with_lat=with_lat,
                          diff=diff, lam_init=lam_init, tk=tk),
        grid=grid,
        in_specs=in_specs,
        out_specs=pl.BlockSpec((tq, 256), qmap if with_lat else (lambda b, i: (b, 0))),
        out_shape=jax.ShapeDtypeStruct((rows, 256), F32),
        scratch_shapes=[pltpu.VMEM((nh, tq, LANES), F32), pltpu.VMEM((nh, tq, LANES), F32),
                        pltpu.VMEM((n_acc, tq, 256), F32)],
        compiler_params=_cparams(("parallel", "parallel")),
        name=name,
    )(*args)


MLA_HEADSPEC = tuple((h, h // 2, h % 2, 0) for h in range(MLA_HEADS))
NA_HEADSPEC = tuple((h // 2, h // 2, h % 2, 0) for h in range(NA_HEADS))
DIFF_HEADSPEC = tuple((g // 4, g // 4, (g // 2) % 2, g % 2) for g in range(2 * DIFF_HEADS))


def _na_kernel(q_ref, k_ref, v_ref, kc_ref, vc_ref, bias_ref, o_ref, *, rows):
    rb = pl.program_id(1)
    k0 = jnp.clip(NA_QROWS * rb - NA_KH // 2, 0, rows - NA_KROWS)
    start = pl.multiple_of(k0 * GRID_W, GRID_W)
    nk = NA_KROWS * GRID_W
    kw = k_ref[pl.ds(start, nk), :].astype(BF16)
    vw = v_ref[pl.ds(start, nk), :].astype(BF16)
    kc = kc_ref[...].astype(BF16)
    vc = vc_ref[...].astype(BF16)
    tq = q_ref.shape[0]
    lane = lax.broadcasted_iota(jnp.int32, (tq, LANES), 1)
    lo_half = lane < 64
    for c in range(2):
        o_c = jnp.zeros((tq, LANES), F32)
        for half in range(2):
            h = 2 * c + half
            q = q_ref[:, h * LANES:(h + 1) * LANES]
            s_loc = _dot_nt(q, kw[:, c * LANES:(c + 1) * LANES])
            s_ctx = _dot_nt(q, kc[:, c * LANES:(c + 1) * LANES])
            blocks = [s_loc[:, i * LANES:(i + 1) * LANES] + bias_ref[0, h, :, i * LANES:(i + 1) * LANES]
                      for i in range(nk // LANES)]
            blocks += [s_ctx[:, i * LANES:(i + 1) * LANES] for i in range(kc.shape[0] // LANES)]
            mx = blocks[0]
            for blk in blocks[1:]:
                mx = jnp.maximum(mx, blk)
            m = jnp.max(mx, axis=-1, keepdims=True)
            ps = [jnp.exp2(blk - m) for blk in blocks]
            lsum = ps[0]
            for pb in ps[1:]:
                lsum = lsum + pb
            l = jnp.sum(lsum, axis=-1, keepdims=True)
            n_loc = nk // LANES
            p_loc = jnp.concatenate([pb.astype(BF16) for pb in ps[:n_loc]], axis=1)
            p_ctx = jnp.concatenate([pb.astype(BF16) for pb in ps[n_loc:]], axis=1)
            pv = (_dot(p_loc, vw[:, c * LANES:(c + 1) * LANES]) + _dot(p_ctx, vc[:, c * LANES:(c + 1) * LANES]))
            mine = lo_half if half == 0 else jnp.logical_not(lo_half)
            o_c = jnp.where(mine, pv * (1.0 / l), o_c)
        o_ref[:, c * LANES:(c + 1) * LANES] = o_c


def _na_attention(qna, kna, vna, bias, *, layer, nb, s_len, l_len):
    rows = s_len // GRID_W
    tq = NA_QROWS * GRID_W
    nrb = rows // NA_QROWS
    ctx0 = nb * s_len // l_len

    def variant(rb):
        return jnp.where(rb == 0, 0, jnp.where(rb == nrb - 1, 2, 1))

    return pl.pallas_call(
        functools.partial(_na_kernel, rows=rows),
        grid=(nb, nrb),
        in_specs=[pl.BlockSpec((tq, NA_HEADS * LANES), lambda b, r: (b * nrb + r, 0)),
                  pl.BlockSpec((s_len, 256), lambda b, r: (b, 0)),
                  pl.BlockSpec((s_len, 256), lambda b, r: (b, 0)),
                  pl.BlockSpec((l_len, 256), lambda b, r: (ctx0 + b, 0)),
                  pl.BlockSpec((l_len, 256), lambda b, r: (ctx0 + b, 0)),
                  pl.BlockSpec((1, NA_HEADS, tq, NA_KROWS * GRID_W), lambda b, r: (variant(r), layer, 0, 0))],
        out_specs=pl.BlockSpec((tq, 256), lambda b, r: (b * nrb + r, 0)),
        out_shape=jax.ShapeDtypeStruct((nb * s_len, 256), F32),
        compiler_params=_cparams(("parallel", "arbitrary"), 56),
        name="na_attn",
    )(qna, kna, vna, kna, vna, bias)


def _na_bias_tables(rpb, rows):
    a = np.arange(NA_QROWS)
    qc = np.arange(GRID_W)
    kr_rel = np.arange(NA_KROWS)
    kc = np.arange(GRID_W)
    col0 = np.clip(qc - NA_KW // 2, 0, GRID_W - NA_KW)
    col_valid = (kc[None, :] >= col0[:, None]) & (kc[None, :] < col0[:, None] + NA_KW)
    col_off = np.clip(kc[None, :] - qc[:, None] + (NA_KW - 1), 0, 2 * NA_KW - 2)
    oh_col = (col_off[:, :, None] == np.arange(2 * NA_KW - 1)).astype(np.float32)
    big = 10 ** 6
    out = []
    for r_blk, k0, rows_eff in ((0, 0, big), (NA_KH, NA_KH // 2, big), (rows - NA_QROWS, rows - NA_KROWS, rows)):
        qr = r_blk + a
        r0 = np.clip(qr - NA_KH // 2, 0, rows_eff - NA_KH)
        kr = k0 + kr_rel
        row_valid = (kr[None, :] >= r0[:, None]) & (kr[None, :] < r0[:, None] + NA_KH)
        row_off = np.clip(kr[None, :] - qr[:, None] + (NA_KH - 1), 0, 2 * NA_KH - 2)
        oh_row = (row_off[:, :, None] == np.arange(2 * NA_KH - 1)).astype(np.float32)
        b = jnp.einsum('akr,hrc,qlc->haqkl', oh_row, rpb, oh_col, precision=lax.Precision.HIGHEST)
        valid = row_valid[:, None, :, None] & col_valid[None, :, None, :]
        b = jnp.where(valid[None], b * LOG2E, NEG)
        out.append(b.reshape(rpb.shape[0], NA_QROWS * GRID_W, NA_KROWS * GRID_W))
    return jnp.stack(out)


def _s5_kernel(*refs, nb):
    uf_refs, ub_refs = refs[:nb], refs[nb:2 * nb]
    (bf_ref, bb_ref, a_ref, cf_ref, cb_ref, d_ref, yf_ref, yb_ref,
     uf_sc, ub_sc, buf_sc, bub_sc, hf_sc, hb_sc) = refs[2 * nb:]
    j = pl.program_id(0)
    tc = uf_refs[0].shape[0]
    half = S5_GROUPS * S5_STATE

    @pl.when(j == 0)
    def _():
        hf_sc[...] = jnp.zeros(hf_sc.shape, F32)
        hb_sc[...] = jnp.zeros(hb_sc.shape, F32)

    for b in range(nb):
        for hv in range(2):
            uf_sc[hv, pl.ds(b, tc, stride=nb), :] = uf_refs[b][:, hv * LANES:(hv + 1) * LANES]
            ub_sc[hv, pl.ds(b, tc, stride=nb), :] = ub_refs[b][:, hv * LANES:(hv + 1) * LANES]

    def scan(u_sc, b_ref, c_ref, y_ref, bu_sc, h_sc, d, reverse):
        u = jnp.concatenate([u_sc[0], u_sc[1]], axis=1)
        bu_sc[...] = _dot(u.astype(BF16), b_ref[...])
        ar = jnp.broadcast_to(a_ref[d, 0:1, :], (nb, half))
        ai = jnp.broadcast_to(a_ref[d, 1:2, :], (nb, half))

        def step(t, carry):
            hr, hi = carry
            tt = (tc - 1 - t) if reverse else t
            row = pl.multiple_of(tt * nb, nb)
            nr = ar * hr - ai * hi + bu_sc[pl.ds(row, nb), 0:half]
            ni = ar * hi + ai * hr + bu_sc[pl.ds(row, nb), half:2 * half]
            bu_sc[pl.ds(row, nb), 0:half] = nr
            bu_sc[pl.ds(row, nb), half:2 * half] = ni
            return nr, ni

        hr, hi = lax.fori_loop(0, tc, step, (h_sc[:, 0:half], h_sc[:, half:2 * half]), unroll=True)
        h_sc[:, 0:half] = hr
        h_sc[:, half:2 * half] = hi
        y = _dot(bu_sc[...].astype(BF16), c_ref[...])
        y_ref[...] = y if reverse else y + d_ref[...] * u

    scan(uf_sc, bf_ref, cf_ref, yf_ref, buf_sc, hf_sc, 0, False)
    scan(ub_sc, bb_ref, cb_ref, yb_ref, bub_sc, hb_sc, 1, True)


def _s5_scan(z, bmat, avec, cmat, d_skip, *, nb, s_len, l_len):
    tc = S5_TC
    cr = tc * nb
    nctx, nlat = l_len // tc, s_len // tc
    nchunk = nctx + nlat
    half = S5_GROUPS * S5_STATE
    n_lat = nb * s_len

    def bwd(j):
        return jnp.where(j < nctx, nctx - 1 - j, nchunk - 1 - (j - nctx))

    def u_spec(b, order):
        def row_block(j):
            c = order(j)
            return jnp.where(c < nctx, (n_lat + b * l_len) // tc + c, (b * s_len) // tc + c - nctx)
        return pl.BlockSpec((tc, 256), lambda j: (row_block(j), 0))

    const = lambda shape: pl.BlockSpec(shape, lambda j: (0,) * len(shape))
    return pl.pallas_call(
        functools.partial(_s5_kernel, nb=nb),
        grid=(nchunk,),
        in_specs=[u_spec(b, lambda j: j) for b in range(nb)] + [u_spec(b, bwd) for b in range(nb)]
                 + [const((256, 2 * half)), const((256, 2 * half)), const((2, 2, half)),
                    const((2 * half, 256)), const((2 * half, 256)), const((1, 256))],
        out_specs=[pl.BlockSpec((cr, 256), lambda j: (j, 0)),
                   pl.BlockSpec((cr, 256), lambda j: (bwd(j), 0))],
        out_shape=[jax.ShapeDtypeStruct((nchunk * cr, 256), F32)] * 2,
        scratch_shapes=[pltpu.VMEM((2, cr, LANES), F32), pltpu.VMEM((2, cr, LANES), F32),
                        pltpu.VMEM((cr, 2 * half), F32), pltpu.VMEM((cr, 2 * half), F32),
                        pltpu.VMEM((nb, 2 * half), F32), pltpu.VMEM((nb, 2 * half), F32)],
        compiler_params=_cparams(("arbitrary",)),
        name="s5_scan",
    )(*([z] * (2 * nb)), bmat[0], bmat[1], avec, cmat[0], cmat[1], d_skip)


def _s5_finish_kernel(yf_ref, yb_ref, w_ref, o_ref, r_sc):
    nb, steps = o_ref.shape[0], o_ref.shape[1]
    r = _dot((yf_ref[...] + yb_ref[...]).astype(BF16), w_ref[...])
    o = r[:, :BRANCH_W] * jax.nn.sigmoid(r[:, BRANCH_W:])
    for hv in range(2):
        r_sc[hv] = o[:, hv * LANES:(hv + 1) * LANES]
    for b in range(nb):
        for hv in range(2):
            o_ref[b, :, hv * LANES:(hv + 1) * LANES] = r_sc[hv, pl.ds(b, steps, stride=nb), :]


def _s5_finish(yf, yb, w_glu, *, nb, t0, t_len):
    steps = TM // nb
    tile0 = t0 // steps
    spec = pl.BlockSpec((TM, 256), lambda i: (tile0 + i, 0))
    out = pl.pallas_call(
        _s5_finish_kernel,
        grid=(t_len // steps,),
        in_specs=[spec, spec, pl.BlockSpec((256, 512), lambda i: (0, 0))],
        out_specs=pl.BlockSpec((nb, steps, 256), lambda i: (0, i, 0)),
        out_shape=jax.ShapeDtypeStruct((nb, t_len, 256), F32),
        scratch_shapes=[pltpu.VMEM((2, TM, LANES), F32)],
        compiler_params=_cparams(("parallel",)),
        name="s5_glu",
    )(yf, yb, w_glu)
    return out.reshape(nb * t_len, 256)


def _s5_params(lam_re, lam_im, log_dt, b_re, b_im, c_re, c_im):
    lam = lax.complex(lam_re, lam_im)
    dt = jnp.exp(log_dt)[..., None]
    a_bar = jnp.exp(lam * dt)
    b_bar = ((a_bar - 1.0) / lam)[..., None] * lax.complex(b_re, b_im)
    eye = jnp.eye(S5_GROUPS, dtype=F32)
    g, p, ch = S5_GROUPS, S5_STATE, S5_GROUP_CH

    def bdiag_in(m):
        return jnp.einsum('gh,dgpc->dgchp', eye, m).reshape(2, g * ch, g * p)

    def bdiag_out(m):
        return jnp.einsum('gh,dgcp->dgphc', eye, m).reshape(2, g * p, g * ch)

    bmat = jnp.concatenate([bdiag_in(jnp.real(b_bar)), bdiag_in(jnp.imag(b_bar))], axis=2).astype(BF16)
    cmat = jnp.concatenate([bdiag_out(c_re), -bdiag_out(c_im)], axis=1).astype(BF16)
    avec = jnp.stack([jnp.real(a_bar).reshape(2, g * p), jnp.imag(a_bar).reshape(2, g * p)], axis=1)
    return bmat, avec, cmat


def _merge_kernel(*refs, n_lat_tiles, has_ctx):
    x_ref, mod_ref = refs[0], refs[1]
    n_br = 8 if has_ctx else 4
    br_refs = refs[2:2 + n_br]
    wg_ref, bg_ref, wb_ref, wo_ref, g_ref, b_ref, x1_ref, h2_ref = refs[2 + n_br:]
    d = D_MODEL
    m = mod_ref[0]
    x = x_ref[...]
    hb = (x * (1.0 + m[1:2]) + m[0:1]).astype(BF16)
    is_ctx = pl.program_id(0) >= n_lat_tiles
    acc = None
    for i in range(4):
        if has_ctx:
            o = jnp.where(is_ctx, br_refs[2 * i + 1][...], br_refs[2 * i][...])
        else:
            o = br_refs[i][...]
        gate = jax.nn.sigmoid(_dot(hb, wg_ref[0, :, i * d:(i + 1) * d]) + bg_ref[:, i * d:(i + 1) * d])
        term = gate * _dot(o.astype(BF16), wb_ref[0, i])
        acc = term if acc is None else acc + term
    y = _dot(acc.astype(BF16), wo_ref[0])
    x1 = _layer_norm(ALPHA * x + m[2:3] * y, g_ref[...], b_ref[...])
    x1_ref[...] = x1
    h2_ref[...] = x1 * (1.0 + m[4:5]) + m[3:4]


def _merge(xa, modl, outs_lat, outs_ctx, wg, bg, wb, wo, g, b, seg, n, n_lat_tiles, layer):
    d = D_MODEL
    tok = pl.BlockSpec((TM, d), lambda i: (i, 0))
    lat = pl.BlockSpec((TM, 256), lambda i: (jnp.minimum(i, n_lat_tiles - 1), 0))
    ctx = pl.BlockSpec((TM, 256), lambda i: (jnp.maximum(i - n_lat_tiles, 0), 0))

    def const(a):
        return pl.BlockSpec(a.shape, lambda i: (0,) * a.ndim)

    def stacked(a):
        return pl.BlockSpec((1,) + a.shape[1:], lambda i: (layer,) + (0,) * (a.ndim - 1))
    has_ctx = outs_ctx is not None
    if has_ctx:
        branches = [a for pair in zip(outs_lat, outs_ctx) for a in pair]
        br_specs = [lat, ctx] * 4
    else:
        branches, br_specs = list(outs_lat), [lat] * 4
    return pl.pallas_call(
        functools.partial(_merge_kernel, n_lat_tiles=n_lat_tiles, has_ctx=has_ctx),
        grid=(n // TM,),
        in_specs=[tok, pl.BlockSpec((1, 6, d), lambda i: (seg(i), 0, 0))] + br_specs
                 + [stacked(wg), const(bg), stacked(wb), stacked(wo), const(g), const(b)],
        out_specs=[tok, tok],
        out_shape=[jax.ShapeDtypeStruct((n, d), F32)] * 2,
        compiler_params=_cparams(("parallel",), 56),
        name="merge_ln1",
    )(xa, modl, *branches, wg, bg, wb, wo, g, b)


def _start_row_gather(src_hbm, dst, idx_ref, base, n_rows, sem):
    def body(r, c):
        pltpu.make_async_copy(src_hbm.at[pl.ds(idx_ref[base + r], 1)], dst.at[pl.ds(r, 1)], sem).start()
        return c

    lax.fori_loop(0, n_rows, body, 0, unroll=8)


def _wait_row_gather(src_hbm, dst, sem):
    pltpu.make_async_copy(src_hbm.at[pl.ds(0, dst.shape[0])], dst, sem).wait()


def _swiglu_block(xb, w1_ref, w3_ref, w2_ref, o_ref, between=None):
    lead = (0,) * (len(w1_ref.shape) - 2)
    n_chunks = D_FF // FF_CHUNK
    for c in range(n_chunks):
        cs = slice(c * FF_CHUNK, (c + 1) * FF_CHUNK)
        a = _dot(xb, w1_ref[lead + (slice(None), cs)])
        b = _dot(xb, w3_ref[lead + (slice(None), cs)])
        g = (a * jax.nn.sigmoid(a) * b).astype(BF16)
        y = _dot(g, w2_ref[lead + (cs, slice(None))])
        if c == 0:
            o_ref[...] = y
        else:
            o_ref[...] += y
        if between is not None:
            between(c, n_chunks)


def _ffn_kernel(h_ref, x_ref, mod_ref, w1_ref, w3_ref, w2_ref, g_ref, b_ref, o_ref, acc_sc):
    _swiglu_block(h_ref[...].astype(BF16), w1_ref, w3_ref, w2_ref, acc_sc)
    m = mod_ref[0]
    o_ref[...] = _layer_norm(ALPHA * x_ref[...] + m[5:6] * acc_sc[...], g_ref[...], b_ref[...])


def _moe_ffn_kernel(be_ref, tok_ref, start_ref, nused_ref, h_hbm, w1_ref, w3_ref, w2_ref, o_ref, xbuf, sem):
    i = pl.program_id(0)
    last = pl.num_programs(0) - 1
    slot = i % 2
    nxt_base = start_ref[jnp.minimum(i + 1, last)]
    nxt_buf, nxt_sem = xbuf.at[1 - slot], sem.at[1 - slot]

    @pl.when(i == 0)
    def _():
        _start_row_gather(h_hbm, xbuf.at[0], tok_ref, start_ref[0], MOE_BLK, sem.at[0])

    _wait_row_gather(h_hbm, xbuf.at[slot], sem.at[slot])

    def start_piece(c, n_chunks):
        per = -(-MOE_BLK // n_chunks)
        for r in range(c * per, min((c + 1) * per, MOE_BLK)):
            pltpu.make_async_copy(h_hbm.at[pl.ds(tok_ref[nxt_base + r], 1)], nxt_buf.at[pl.ds(r, 1)],
                                  nxt_sem).start()

    @pl.when(i < nused_ref[0])
    def _():
        _swiglu_block(xbuf[slot].astype(BF16), w1_ref, w3_ref, w2_ref, o_ref, between=start_piece)

    @pl.when(i >= nused_ref[0])
    def _():
        o_ref[...] = jnp.zeros(o_ref.shape, F32)
        _start_row_gather(h_hbm, nxt_buf, tok_ref, nxt_base, MOE_BLK, nxt_sem)

    @pl.when(i == last)
    def _():
        _wait_row_gather(h_hbm, nxt_buf, nxt_sem)


def _moe_ffn(h, block_e, tok_sorted, blk_start, n_used, w1, w3, w2, layer):
    d = h.shape[1]
    n_blocks = block_e.shape[0]
    wspec = lambda shape: pl.BlockSpec(shape, lambda i, be, tok, st, nu: (layer, be[i], 0, 0),
                                       pipeline_mode=pl.Buffered(1))
    grid_spec = pltpu.PrefetchScalarGridSpec(
        num_scalar_prefetch=4,
        grid=(n_blocks,),
        in_specs=[pl.BlockSpec(memory_space=pl.ANY), wspec((1, 1, d, D_FF)), wspec((1, 1, d, D_FF)),
                  wspec((1, 1, D_FF, d))],
        out_specs=pl.BlockSpec((MOE_BLK, d), lambda i, be, tok, st, nu: (i, 0)),
        scratch_shapes=[pltpu.VMEM((2, MOE_BLK, d), F32), pltpu.SemaphoreType.DMA((2,))],
    )
    return pl.pallas_call(
        _moe_ffn_kernel,
        grid_spec=grid_spec,
        out_shape=jax.ShapeDtypeStruct((n_blocks * MOE_BLK, d), F32),
        compiler_params=_cparams(("arbitrary",), 56),
        name="moe_swiglu",
    )(block_e, tok_sorted, blk_start, n_used, h, w1, w3, w2)


def _ffn(h2, x1, modl, w1, w3, w2, g, b, seg, layer):
    n, d = h2.shape
    tok = pl.BlockSpec((TM, d), lambda i: (i, 0))
    vec = pl.BlockSpec((1, d), lambda i: (0, 0))
    wspec = lambda shape: pl.BlockSpec(shape, lambda i: (layer, 0, 0), pipeline_mode=pl.Buffered(1))
    return pl.pallas_call(
        _ffn_kernel,
        grid=(n // TM,),
        in_specs=[tok, tok, pl.BlockSpec((1, 6, d), lambda i: (seg(i), 0, 0)),
                  wspec((1, d, D_FF)), wspec((1, d, D_FF)), wspec((1, D_FF, d)), vec, vec],
        out_specs=tok,
        out_shape=jax.ShapeDtypeStruct((n, d), F32),
        scratch_shapes=[pltpu.VMEM((TM, d), F32)],
        compiler_params=_cparams(("parallel",), 56),
        name="swiglu_ln2",
    )(h2, x1, modl, w1, w3, w2, g, b)


def _router_kernel(h_ref, w_ref, o_ref):
    logits = jnp.dot(h_ref[...], w_ref[...], preferred_element_type=F32, precision=lax.Precision.HIGHEST)
    lane = lax.broadcasted_iota(jnp.int32, logits.shape, 1)
    lg = jnp.where(lane < N_EXPERTS, logits, NEG)
    v1 = jnp.max(lg, axis=-1, keepdims=True)
    i1 = jnp.min(jnp.where(lg == v1, lane, LANES), axis=-1, keepdims=True)
    lg2 = jnp.where(lane == i1, NEG, lg)
    v2 = jnp.max(lg2, axis=-1, keepdims=True)
    i2 = jnp.min(jnp.where(lg2 == v2, lane, LANES), axis=-1, keepdims=True)
    e = jnp.exp(v2 - v1)
    g1 = 1.0 / (1.0 + e)
    g2 = e / (1.0 + e)
    out = jnp.where(lane == 0, i1.astype(F32), jnp.where(lane == 1, i2.astype(F32),
                    jnp.where(lane == 2, g1, jnp.where(lane == 3, g2, 0.0))))
    o_ref[...] = out


def _router(h2, w_router_pad):
    n, d = h2.shape
    return pl.pallas_call(
        _router_kernel,
        grid=(n // TM,),
        in_specs=[pl.BlockSpec((TM, d), lambda i: (i, 0)), pl.BlockSpec((d, LANES), lambda i: (0, 0))],
        out_specs=pl.BlockSpec((TM, LANES), lambda i: (i, 0)),
        out_shape=jax.ShapeDtypeStruct((n, LANES), F32),
        compiler_params=_cparams(("parallel",)),
        name="router",
    )(h2, w_router_pad)


def _moe_combine_kernel(dest_ref, x_ref, mod_ref, gt_ref, y_hbm, g_ref, b_ref, o_ref, ybuf, sem):
    i = pl.program_id(0)
    last = pl.num_programs(0) - 1
    slot = i % 2
    nxt_base = jnp.minimum(i + 1, last) * 2 * TM
    nxt_buf, nxt_sem = ybuf.at[1 - slot], sem.at[1 - slot]

    @pl.when(i == 0)
    def _():
        _start_row_gather(y_hbm, ybuf.at[0], dest_ref, 0, 2 * TM, sem.at[0])

    for r in range(2 * TM):
        pltpu.make_async_copy(y_hbm.at[pl.ds(dest_ref[nxt_base + r], 1)], nxt_buf.at[pl.ds(r, 1)], nxt_sem).start()

    _wait_row_gather(y_hbm, ybuf.at[slot], sem.at[slot])
    gt = gt_ref[...]
    f = gt[:, 2:3] * ybuf[slot, 0:TM, :] + gt[:, 3:4] * ybuf[slot, TM:2 * TM, :]
    m = mod_ref[0]
    o_ref[...] = _layer_norm(ALPHA * x_ref[...] + m[5:6] * f, g_ref[...], b_ref[...])

    @pl.when(i == last)
    def _():
        _wait_row_gather(y_hbm, nxt_buf, nxt_sem)


def _moe_combine(x1, modl, y_rows, dest_tiles, gates, g, b, seg):
    n, d = x1.shape
    tok = pl.BlockSpec((TM, d), lambda i, dst: (i, 0))
    vec = pl.BlockSpec((1, d), lambda i, dst: (0, 0))
    grid_spec = pltpu.PrefetchScalarGridSpec(
        num_scalar_prefetch=1,
        grid=(n // TM,),
        in_specs=[tok, pl.BlockSpec((1, 6, d), lambda i, dst: (seg(i), 0, 0)),
                  pl.BlockSpec((TM, LANES), lambda i, dst: (i, 0)), pl.BlockSpec(memory_space=pl.ANY), vec, vec],
        out_specs=tok,
        scratch_shapes=[pltpu.VMEM((2, 2 * TM, d), F32), pltpu.SemaphoreType.DMA((2,))],
    )
    return pl.pallas_call(
        _moe_combine_kernel,
        grid_spec=grid_spec,
        out_shape=jax.ShapeDtypeStruct((n, d), F32),
        compiler_params=_cparams(("arbitrary",)),
        name="moe_combine_ln2",
    )(dest_tiles, x1, modl, gates, y_rows, g, b)


def _rope_tables(s_len, width, rope_lane, extra_rows):
    t = jnp.arange(s_len)
    lane = np.arange(width)
    inv = ROPE_BASE ** (-(jnp.asarray(lane % 8, F32)) / 8.0)
    pos = jnp.where((lane & 16) == 0, (t // GRID_W)[:, None], (t % GRID_W)[:, None]).astype(F32)
    ang = pos * inv[None, :]
    cos = jnp.where(rope_lane[None, :], jnp.cos(ang), 1.0)
    sin = jnp.where(rope_lane[None, :], jnp.where((lane & 8) == 0, -jnp.sin(ang), jnp.sin(ang)), 0.0)
    cos = jnp.concatenate([cos, jnp.ones((extra_rows, width), F32)])
    sin = jnp.concatenate([sin, jnp.zeros((extra_rows, width), F32)])
    return cos, sin


def _pad_heads(w, n_heads, width):
    k = w.shape[0]
    w = w.reshape(k, n_heads, -1)
    return jnp.pad(w, ((0, 0), (0, 0), (0, width - w.shape[-1]))).reshape(k, n_heads * width)


def _route(top, n_tok):
    e_flat = jnp.concatenate([top[:, 0], top[:, 1]]).astype(jnp.int32)
    n_assign = 2 * n_tok
    onehot = (e_flat[:, None] == jnp.arange(N_EXPERTS)[None, :]).astype(jnp.int32)
    csum = jnp.cumsum(onehot, axis=0)
    counts = csum[-1]
    padded = (counts + MOE_BLK - 1) // MOE_BLK * MOE_BLK
    pad_end = jnp.cumsum(padded)
    pad_start = pad_end - padded
    dest = jnp.sum(onehot * (csum - 1 + pad_start[None, :]), axis=1)
    n_blocks = -(-n_assign // MOE_BLK) + N_EXPERTS
    blk_row0 = jnp.arange(n_blocks) * MOE_BLK
    block_e = jnp.minimum(jnp.searchsorted(pad_end, blk_row0, side='right'), N_EXPERTS - 1).astype(jnp.int32)
    n_used = (pad_end[-1:] // MOE_BLK).astype(jnp.int32)
    tok_sorted = jnp.concatenate([(jnp.argsort(e_flat) % n_tok).astype(jnp.int32), jnp.zeros((MOE_BLK,), jnp.int32)])
    seg_start = jnp.cumsum(counts) - counts
    blk_start = jnp.clip(seg_start[block_e] + blk_row0 - pad_start[block_e], 0, n_assign).astype(jnp.int32)
    dest_tiles = dest.astype(jnp.int32).reshape(2, n_tok // TM, TM).transpose(1, 0, 2).reshape(-1)
    return tok_sorted, blk_start, block_e, n_used, dest_tiles


def kernel(x, c, ctx, c_ctx, w_ada, b_ada, w_in, na_rpb, mla_q_norm, mla_kv_norm, mla_w_uq, mla_w_ukv,
           s5_lam_re, s5_lam_im, s5_log_dt, s5_b_re, s5_b_im, s5_c_re, s5_c_im, s5_d, s5_w_glu,
           diff_lam_q1, diff_lam_k1, diff_lam_q2, diff_lam_k2, diff_subln,
           w_branch, w_gate, b_gate, w_out, ln1_g, ln1_b, ln2_g, ln2_b,
           ffn_w1, ffn_w3, ffn_w2, moe_router, moe_w1, moe_w3, moe_w2):
    nb, s_len, d = x.shape
    l_len = ctx.shape[1]
    n_lat, n_ctx = nb * s_len, nb * l_len
    n_tot = n_lat + n_ctx
    rows = s_len // GRID_W
    assert d == D_MODEL and s_len % TM == 0 and n_ctx % TM == 0 and rows % NA_QROWS == 0 and rows >= NA_KROWS
    assert l_len % S5_TC == 0 and s_len % S5_TC == 0

    def seg(i):
        return jnp.minimum((i * TM) // s_len, nb)

    n_lat_tiles = n_lat // TM

    xa = jnp.concatenate([x.reshape(n_lat, d), ctx.reshape(n_ctx, d)], axis=0)
    nrow_mod = -(-(nb + 1) // 16) * 16
    cvec = jnp.concatenate([c, c_ctx[None, :], jnp.zeros((nrow_mod - nb - 1, d), F32)], axis=0)
    mod_all = _ada_all(cvec, w_ada, b_ada).reshape(DEPTH, nrow_mod, 6, d)

    lane512 = np.arange(512)
    q_rope = (lane512 % LANES >= MLA_NOPE) & (lane512 % LANES < MLA_NOPE + MLA_ROPE)
    cq, sq = _rope_tables(s_len, 512, q_rope, TM)
    ck, sk = _rope_tables(s_len, LANES, np.arange(LANES) < MLA_ROPE, TM)
    cd, sd = _rope_tables(s_len, 256, np.ones((256,), bool), TM)
    tabs = (cq, sq, ck, sk, cd, sd)
    place = np.zeros((LANES, MLA_HEADS * LANES), np.float32)
    for h in range(MLA_HEADS):
        place[np.arange(MLA_ROPE), h * LANES + MLA_NOPE + np.arange(MLA_ROPE)] = 1.0
    place = jnp.asarray(place, BF16)

    w_gate_b, w_branch_b, w_out_b = w_gate.astype(BF16), w_branch.astype(BF16), w_out.astype(BF16)
    ffn_b = (ffn_w1.astype(BF16), ffn_w3.astype(BF16), ffn_w2.astype(BF16))
    moe_b = (moe_w1.astype(BF16), moe_w3.astype(BF16), moe_w2.astype(BF16))
    bias_all = _na_bias_tables(na_rpb.reshape(DEPTH * NA_HEADS, 2 * NA_KH - 1, 2 * NA_KW - 1), rows)

    for layer in range(DEPTH):
        ctx_out = layer < DEPTH - 1
        lam_init = 0.8 - 0.6 * math.exp(-0.3 * layer)
        modl = mod_all[layer]

        wi = w_in[layer]
        w_in_pad = jnp.concatenate([wi[:, :1184], jnp.zeros((d, 96), F32), wi[:, 1184:]], axis=1).astype(BF16)
        wuq = _pad_heads(mla_w_uq[layer], MLA_HEADS, LANES).astype(BF16)
        wukv4 = mla_w_ukv[layer].reshape(-1, MLA_HEADS, MLA_NOPE + MLA_V)
        wuk = jnp.pad(wukv4[:, :, :MLA_NOPE], ((0, 0), (0, 0), (0, LANES - MLA_NOPE))).reshape(-1, MLA_HEADS * LANES)
        wuv = wukv4[:, :, MLA_NOPE:].reshape(-1, MLA_HEADS * MLA_V)
        wukv = jnp.concatenate([wuk, wuv], axis=1).astype(BF16)
        bmat, avec, cmat = _s5_params(s5_lam_re[layer], s5_lam_im[layer], s5_log_dt[layer], s5_b_re[layer],
                                      s5_b_im[layer], s5_c_re[layer], s5_c_im[layer])
        lam = (jnp.exp(jnp.sum(diff_lam_q1[layer] * diff_lam_k1[layer]))
               - jnp.exp(jnp.sum(diff_lam_q2[layer] * diff_lam_k2[layer])) + lam_init)
        lam_vec = jnp.full((1, LANES), lam, F32)
        subln = jnp.tile(diff_subln[layer], DIFF_HEADS).reshape(1, 256)

        qm, km, vm, qd, kd, vd, qna, kna, vna, u5 = _proj(
            xa, modl, w_in_pad, tabs, mla_q_norm[layer].reshape(1, -1), mla_kv_norm[layer].reshape(1, -1),
            wuq, wukv, place, nb=nb, s_len=s_len)
        common = dict(nb=nb, s_len=s_len, l_len=l_len)
        o_na = _na_attention(qna, kna, vna, bias_all, layer=layer, **common)
        o_mla = _flash(qm, km, vm, heads=MLA_HEADSPEC, kw=512, name="mla_attn", **common)
        o_diff = _flash(qd, kd, vd, heads=DIFF_HEADSPEC, kw=256, diff=True, lam=lam_vec,
                        subln=subln, lam_init=lam_init, name="diff_attn", **common)
        yf, yb = _s5_scan(u5, bmat, avec, cmat, s5_d[layer].reshape(1, 256), **common)
        w_glu = s5_w_glu[layer].astype(BF16)
        o_s5_lat = _s5_finish(yf, yb, w_glu, nb=nb, t0=l_len, t_len=s_len)
        if ctx_out:
            o_na_c = _flash(qna, kna, vna, heads=NA_HEADSPEC, kw=256, with_lat=False, name="na_ctx_attn", **common)
            o_mla_c = _flash(qm, km, vm, heads=MLA_HEADSPEC, kw=512, with_lat=False,
                             name="mla_ctx_attn", **common)
            o_diff_c = _flash(qd, kd, vd, heads=DIFF_HEADSPEC, kw=256, diff=True, lam=lam_vec,
                              subln=subln, lam_init=lam_init, with_lat=False, name="diff_ctx_attn", **common)
            o_s5_c = _s5_finish(yf, yb, w_glu, nb=nb, t0=0, t_len=l_len)
            outs_ctx = [o_na_c, o_mla_c, o_s5_c, o_diff_c]
            n_act = n_tot
        else:
            outs_ctx = None
            n_act = n_lat
        x1, h2 = _merge(xa, modl, [o_na, o_mla, o_s5_lat, o_diff], outs_ctx,
                        w_gate_b, b_gate[layer].reshape(1, -1), w_branch_b, w_out_b,
                        ln1_g[layer].reshape(1, d), ln1_b[layer].reshape(1, d), seg, n_act, n_lat_tiles, layer)

        jj = layer // 2
        g2, b2 = ln2_g[layer].reshape(1, d), ln2_b[layer].reshape(1, d)
        if layer % 2 == 0:
            xa = _ffn(h2, x1, modl, *ffn_b, g2, b2, seg, jj)
        else:
            wr = jnp.pad(moe_router[jj], ((0, 0), (0, LANES - N_EXPERTS)))
            top = _router(h2, wr)
            tok_sorted, blk_start, block_e, n_used, dest_tiles = _route(top, n_act)
            y_rows = _moe_ffn(h2, block_e, tok_sorted, blk_start, n_used, *moe_b, jj)
            xa = _moe_combine(x1, modl, y_rows, dest_tiles, top, g2, b2, seg)
    return xa[:n_lat].reshape(nb, s_len, d)
```

```python
import functools
import math

import jax
import jax.numpy as jnp
import numpy as np
from jax import lax
from jax.experimental import pallas as pl
from jax.experimental.pallas import tpu as pltpu

F32 = jnp.float32
BF16 = jnp.bfloat16

D_MODEL = 1024
DEPTH = 4
GRID_W = 64
BRANCH_W = 256
NA_HEADS = 4
NA_HD = 64
NA_KH = 8
NA_KW = 16
MLA_HEADS = 4
MLA_NOPE = 64
MLA_ROPE = 32
MLA_V = 64
S5_GROUP_CH = 16
S5_GROUPS = 16
S5_STATE = 64
DIFF_HEADS = 4
DIFF_HD = 32
D_FF = 3584
N_EXPERTS = 8
ROPE_BASE = 10000.0
LN_EPS = 1e-5
RMS_EPS = 1e-6
ALPHA = (2 * DEPTH) ** 0.25
NA_SCALE = NA_HD ** -0.5
MLA_SCALE = (MLA_NOPE + MLA_ROPE) ** -0.5
DIFF_SCALE = DIFF_HD ** -0.5
LOG2E = 1.4426950408889634

LANES = 128
ZW = 2304
Z_NAQ, Z_NAK, Z_NAV, Z_MQ, Z_MKV, Z_MKR, Z_S5, Z_DQ, Z_DK, Z_DV = (
    0, 256, 512, 768, 1024, 1152, 1280, 1536, 1792, 2048)
NEG = -1e30
TM = 512
TQ = 1024
TK = 256
NA_QROWS = 4
NA_KROWS = NA_QROWS + NA_KH
S5_TC = 128
FF_CHUNK = 512
MOE_BLK = 512


def _cparams(sem, vmem_mb=48):
    return pltpu.CompilerParams(dimension_semantics=sem, vmem_limit_bytes=vmem_mb << 20)


def _dot(a, b):
    return jnp.dot(a, b, preferred_element_type=F32)


def _dot_nt(a, b):
    return lax.dot_general(a, b, (((1,), (1,)), ((), ())), preferred_element_type=F32)


def _layer_norm(r, g, b):
    rc = r - jnp.mean(r, axis=-1, keepdims=True)
    var = jnp.mean(rc * rc, axis=-1, keepdims=True)
    return rc * lax.rsqrt(var + LN_EPS) * g + b


def _ada_kernel(c_ref, w_ref, b_ref, o_ref):
    c = c_ref[...]
    cond = c * jax.nn.sigmoid(c)
    o_ref[0] = _dot(cond.astype(BF16), w_ref[0].astype(BF16)) + b_ref[0]


def _ada_all(cvec, w_ada, b_ada):
    nrow = cvec.shape[0]
    d = D_MODEL
    return pl.pallas_call(
        _ada_kernel,
        grid=(DEPTH, 6),
        in_specs=[pl.BlockSpec((nrow, d), lambda l, n: (0, 0)),
                  pl.BlockSpec((1, d, d), lambda l, n: (l, 0, n)),
                  pl.BlockSpec((1, 1, d), lambda l, n: (l, 0, n))],
        out_specs=pl.BlockSpec((1, nrow, d), lambda l, n: (l, 0, n)),
        out_shape=jax.ShapeDtypeStruct((DEPTH, nrow, 6 * d), F32),
        compiler_params=_cparams(("parallel", "parallel")),
        name="ada_mod",
    )(cvec, w_ada, b_ada.reshape(DEPTH, 1, 6 * d))


def _swap8(z):
    w = z.shape[-1]
    lane = lax.broadcasted_iota(jnp.int32, z.shape, 1)
    up = pltpu.roll(z, w - 8, 1)
    dn = pltpu.roll(z, 8, 1)
    return jnp.where((lane & 8) == 0, up, dn)


def _rms(z, g):
    return z * lax.rsqrt(jnp.mean(z * z, axis=-1, keepdims=True) + RMS_EPS) * g


def _proj_kernel(x_ref, mod_ref, w_ref, cq_ref, sq_ref, ck_ref, sk_ref, cd_ref, sd_ref,
                 qn_ref, kvn_ref, wuq_ref, wukv_ref, place_ref,
                 qm_ref, km_ref, vm_ref, qd_ref, kd_ref, vd_ref, qna_ref, kna_ref, vna_ref, u_ref):
    m = mod_ref[0]
    h = x_ref[...] * (1.0 + m[1:2]) + m[0:1]
    z = _dot(h.astype(BF16), w_ref[...])
    kna_ref[...] = z[:, Z_NAK:Z_NAK + 256].astype(BF16)
    vna_ref[...] = z[:, Z_NAV:Z_NAV + 256].astype(BF16)
    u_ref[...] = z[:, Z_S5:Z_S5 + 256]
    aq = _rms(z[:, Z_MQ:Z_MQ + 256], qn_ref[...])
    q = _dot(aq.astype(BF16), wuq_ref[...])
    q = q * cq_ref[...] + _swap8(q) * sq_ref[...]
    qm_ref[...] = (q * (MLA_SCALE * LOG2E)).astype(BF16)
    akv = _rms(z[:, Z_MKV:Z_MKV + 128], kvn_ref[...])
    kv = _dot(akv.astype(BF16), wukv_ref[...])
    kr = z[:, Z_MKR:Z_MKR + 128]
    kr = kr * ck_ref[...] + _swap8(kr) * sk_ref[...]
    km_ref[...] = (kv[:, :4 * LANES] + _dot(kr.astype(BF16), place_ref[...])).astype(BF16)
    vm_ref[...] = kv[:, 4 * LANES:].astype(BF16)
    dq = z[:, Z_DQ:Z_DQ + 256]
    dq = (dq * cd_ref[...] + _swap8(dq) * sd_ref[...]) * (DIFF_SCALE * LOG2E)
    dk = z[:, Z_DK:Z_DK + 256]
    dk = dk * cd_ref[...] + _swap8(dk) * sd_ref[...]
    kd_ref[...] = dk.astype(BF16)
    vd_ref[...] = z[:, Z_DV:Z_DV + 256].astype(BF16)
    lane = lax.broadcasted_iota(jnp.int32, (dq.shape[0], LANES), 1)
    for g in range(2 * DIFF_HEADS):
        blk = dq[:, (g // 4) * LANES:(g // 4 + 1) * LANES]
        qd_ref[:, g * LANES:(g + 1) * LANES] = jnp.where((lane // DIFF_HD) == (g % 4), blk, 0.0).astype(BF16)
    naq = z[:, Z_NAQ:Z_NAQ + 256] * (NA_SCALE * LOG2E)
    for hd in range(NA_HEADS):
        blk = naq[:, (hd // 2) * LANES:(hd // 2 + 1) * LANES]
        qna_ref[:, hd * LANES:(hd + 1) * LANES] = jnp.where((lane // NA_HD) == (hd % 2), blk, 0.0).astype(BF16)


def _proj(xa, modl, w_in_pad, tabs, qn, kvn, wuq, wukv, place, *, nb, s_len):
    n, d = xa.shape
    cq, sq, ck, sk, cd, sd = tabs
    tps = s_len // TM
    n_lat_tiles = nb * tps

    def tile(i):
        return jnp.where(i < n_lat_tiles, (i % nb) * tps + i // nb, i)

    def seg(i):
        return jnp.minimum((tile(i) * TM) // s_len, nb)

    def tab_idx(i):
        return jnp.where(i < n_lat_tiles, i // nb, tps)

    def tspec(width):
        return pl.BlockSpec((TM, width), lambda i: (tab_idx(i), 0))

    def wspec(a):
        return pl.BlockSpec(a.shape, lambda i: (0,) * a.ndim)

    def ospec(width):
        return pl.BlockSpec((TM, width), lambda i: (tile(i), 0))

    widths = [512, 512, 256, 1024, 256, 256, 512, 256, 256]
    return pl.pallas_call(
        _proj_kernel,
        grid=(n // TM,),
        in_specs=[pl.BlockSpec((TM, d), lambda i: (tile(i), 0)),
                  pl.BlockSpec((1, 6, d), lambda i: (seg(i), 0, 0)),
                  wspec(w_in_pad),
                  tspec(512), tspec(512), tspec(128), tspec(128), tspec(256), tspec(256),
                  wspec(qn), wspec(kvn), wspec(wuq), wspec(wukv), wspec(place)],
        out_specs=[ospec(w) for w in widths] + [ospec(256)],
        out_shape=[jax.ShapeDtypeStruct((n, w), BF16) for w in widths] + [jax.ShapeDtypeStruct((n, 256), F32)],
        compiler_params=_cparams(("parallel",), 56),
        name="in_proj_prep",
    )(xa, modl, w_in_pad, cq, sq, ck, sk, cd, sd, qn, kvn, wuq, wukv, place)


def _flash_kernel(*refs, heads, n_acc, with_lat, diff, lam_init, tk):
    refs = list(refs)
    q_ref = refs.pop(0)
    if with_lat:
        kl_ref, vl_ref = refs.pop(0), refs.pop(0)
    kc_ref, vc_ref = refs.pop(0), refs.pop(0)
    if diff:
        lam_ref, sub_ref = refs.pop(0), refs.pop(0)
    o_ref, m_sc, l_sc, acc_sc = refs
    tq = q_ref.shape[0]
    lane = lax.broadcasted_iota(jnp.int32, (tq, LANES), 1)
    lo_half = lane < 64

    def tile(k_ref, v_ref, rows):
        nkb = (rows.stop - rows.start if isinstance(rows, slice) else rows.size) // LANES
        for g, (kb, vb, vh, ai) in enumerate(heads):
            q = q_ref[:, g * LANES:(g + 1) * LANES]
            k = k_ref[rows, kb * LANES:(kb + 1) * LANES].astype(BF16)
            s = _dot_nt(q, k)
            blocks = [s[:, c * LANES:(c + 1) * LANES] for c in range(nkb)]
            mx = blocks[0]
            for blk in blocks[1:]:
                mx = jnp.maximum(mx, blk)
            m_prev = m_sc[g]
            m_new = jnp.maximum(m_prev, jnp.max(mx, axis=-1, keepdims=True))
            alpha = jnp.exp2(m_prev - m_new)
            ps = [jnp.exp2(blk - m_new) for blk in blocks]
            lsum = ps[0]
            for pb in ps[1:]:
                lsum = lsum + pb
            l_sc[g] = alpha * l_sc[g] + lsum
            m_sc[g] = m_new
            p = jnp.concatenate([pb.astype(BF16) for pb in ps], axis=1)
            v = v_ref[rows, vb * LANES:(vb + 1) * LANES].astype(BF16)
            pv = _dot(p, v)
            old = acc_sc[ai, :, vb * LANES:(vb + 1) * LANES]
            mine = lo_half if vh == 0 else jnp.logical_not(lo_half)
            acc_sc[ai, :, vb * LANES:(vb + 1) * LANES] = jnp.where(mine, alpha * old + pv, old)

    m_sc[...] = jnp.full(m_sc.shape, NEG, F32)
    l_sc[...] = jnp.zeros(l_sc.shape, F32)
    acc_sc[...] = jnp.zeros(acc_sc.shape, F32)
    tile(kc_ref, vc_ref, slice(0, kc_ref.shape[0]))

    if with_lat:
        def kv_step(j, carry):
            tile(kl_ref, vl_ref, pl.ds(pl.multiple_of(j * tk, tk), tk))
            return carry

        lax.fori_loop(0, kl_ref.shape[0] // tk, kv_step, 0, unroll=8)

    def inv_l(g):
        return 1.0 / jnp.sum(l_sc[g], axis=-1, keepdims=True)

    for c in range(2):
        if not diff:
            o = jnp.where(lo_half, acc_sc[0, :, c * LANES:(c + 1) * LANES] * inv_l(2 * c),
                          acc_sc[0, :, c * LANES:(c + 1) * LANES] * inv_l(2 * c + 1))
        else:
            o1 = jnp.where(lo_half, acc_sc[0, :, c * LANES:(c + 1) * LANES] * inv_l(4 * c),
                           acc_sc[0, :, c * LANES:(c + 1) * LANES] * inv_l(4 * c + 2))
            o2 = jnp.where(lo_half, acc_sc[1, :, c * LANES:(c + 1) * LANES] * inv_l(4 * c + 1),
                           acc_sc[1, :, c * LANES:(c + 1) * LANES] * inv_l(4 * c + 3))
            o = o1 - lam_ref[...] * o2
            sq = o * o
            ms_lo = jnp.sum(jnp.where(lo_half, sq, 0.0), axis=-1, keepdims=True) * (1.0 / 64)
            ms_hi = jnp.sum(jnp.where(lo_half, 0.0, sq), axis=-1, keepdims=True) * (1.0 / 64)
            rs = jnp.where(lo_half, lax.rsqrt(ms_lo + RMS_EPS), lax.rsqrt(ms_hi + RMS_EPS))
            o = o * rs * sub_ref[:, c * LANES:(c + 1) * LANES] * (1.0 - lam_init)
        o_ref[:, c * LANES:(c + 1) * LANES] = o


def _flash(q, k, v, *, nb, s_len, l_len, heads, kcol=0, vcol=0, kw=None, vw=256,
           with_lat=True, diff=False, lam=None, subln=None, lam_init=0.0, name="flash"):
    nh = len(heads)
    n_acc = 2 if diff else 1
    ctx0 = nb * s_len // l_len
    tk = min(TK, s_len)
    if with_lat:
        tq = min(TQ, s_len)
        grid = (nb, s_len // tq)
        rows = nb * s_len
        qmap = lambda b, i: (b * (s_len // tq) + i, 0)
    else:
        tq = l_len
        grid = (nb, 1)
        rows = nb * l_len
        qmap = lambda b, i: (ctx0 + b, 0)
    in_specs = [pl.BlockSpec((tq, nh * LANES), qmap)]
    args = [q]
    if with_lat:
        in_specs += [pl.BlockSpec((s_len, kw), lambda b, i: (b, kcol)),
                     pl.BlockSpec((s_len, vw), lambda b, i: (b, vcol))]
        args += [k, v]
    in_specs += [pl.BlockSpec((l_len, kw), lambda b, i: (ctx0 + b, kcol)),
                 pl.BlockSpec((l_len, vw), lambda b, i: (ctx0 + b, vcol))]
    args += [k, v]
    if diff:
        in_specs += [pl.BlockSpec((1, LANES), lambda b, i: (0, 0)),
                     pl.BlockSpec((1, 256), lambda b, i: (0, 0))]
        args += [lam, subln]
    return pl.pallas_call(
        functools.partial(_flash_kernel, heads=heads, n_acc=n_acc, with_lat=with_lat,
                          diff=diff, lam_init=lam_init, tk=tk),
        grid=grid,
        in_specs=in_specs,
        out_specs=pl.BlockSpec((tq, 256), qmap if with_lat else (lambda b, i: (b, 0))),
        out_shape=jax.ShapeDtypeStruct((rows, 256), F32),
        scratch_shapes=[pltpu.VMEM((nh, tq, LANES), F32), pltpu.VMEM((nh, tq, LANES), F32),
                        pltpu.VMEM((n_acc, tq, 256), F32)],
        compiler_params=_cparams(("parallel", "parallel")),
        name=name,
    )(*args)


MLA_HEADSPEC = tuple((h, h // 2, h % 2, 0) for h in range(MLA_HEADS))
NA_HEADSPEC = tuple((h // 2, h // 2, h % 2, 0) for h in range(NA_HEADS))
DIFF_HEADSPEC = tuple((g // 4, g // 4, (g // 2) % 2, g % 2) for g in range(2 * DIFF_HEADS))


def _na_kernel(q_ref, k_ref, v_ref, kc_ref, vc_ref, bias_ref, o_ref, *, rows):
    rb = pl.program_id(1)
    k0 = jnp.clip(NA_QROWS * rb - NA_KH // 2, 0, rows - NA_KROWS)
    start = pl.multiple_of(k0 * GRID_W, GRID_W)
    nk = NA_KROWS * GRID_W
    kw = k_ref[pl.ds(start, nk), :].astype(BF16)
    vw = v_ref[pl.ds(start, nk), :].astype(BF16)
    kc = kc_ref[...].astype(BF16)
    vc = vc_ref[...].astype(BF16)
    tq = q_ref.shape[0]
    lane = lax.broadcasted_iota(jnp.int32, (tq, LANES), 1)
    lo_half = lane < 64
    for c in range(2):
        o_c = jnp.zeros((tq, LANES), F32)
        for half in range(2):
            h = 2 * c + half
            q = q_ref[:, h * LANES:(h + 1) * LANES]
            s_loc = _dot_nt(q, kw[:, c * LANES:(c + 1) * LANES])
            s_ctx = _dot_nt(q, kc[:, c * LANES:(c + 1) * LANES])
            blocks = [s_loc[:, i * LANES:(i + 1) * LANES] + bias_ref[0, h, :, i * LANES:(i + 1) * LANES]
                      for i in range(nk // LANES)]
            blocks += [s_ctx[:, i * LANES:(i + 1) * LANES] for i in range(kc.shape[0] // LANES)]
            mx = blocks[0]
            for blk in blocks[1:]:
                mx = jnp.maximum(mx, blk)
            m = jnp.max(mx, axis=-1, keepdims=True)
            ps = [jnp.exp2(blk - m) for blk in blocks]
            lsum = ps[0]
            for pb in ps[1:]:
                lsum = lsum + pb
            l = jnp.sum(lsum, axis=-1, keepdims=True)
            n_loc = nk // LANES
            p_loc = jnp.concatenate([pb.astype(BF16) for pb in ps[:n_loc]], axis=1)
            p_ctx = jnp.concatenate([pb.astype(BF16) for pb in ps[n_loc:]], axis=1)
            pv = (_dot(p_loc, vw[:, c * LANES:(c + 1) * LANES]) + _dot(p_ctx, vc[:, c * LANES:(c + 1) * LANES]))
            mine = lo_half if half == 0 else jnp.logical_not(lo_half)
            o_c = jnp.where(mine, pv * (1.0 / l), o_c)
        o_ref[:, c * LANES:(c + 1) * LANES] = o_c


def _na_attention(qna, kna, vna, bias, *, layer, nb, s_len, l_len):
    rows = s_len // GRID_W
    tq = NA_QROWS * GRID_W
    nrb = rows // NA_QROWS
    ctx0 = nb * s_len // l_len

    def variant(rb):
        return jnp.where(rb == 0, 0, jnp.where(rb == nrb - 1, 2, 1))

    return pl.pallas_call(
        functools.partial(_na_kernel, rows=rows),
        grid=(nb, nrb),
        in_specs=[pl.BlockSpec((tq, NA_HEADS * LANES), lambda b, r: (b * nrb + r, 0)),
                  pl.BlockSpec((s_len, 256), lambda b, r: (b, 0)),
                  pl.BlockSpec((s_len, 256), lambda b, r: (b, 0)),
                  pl.BlockSpec((l_len, 256), lambda b, r: (ctx0 + b, 0)),
                  pl.BlockSpec((l_len, 256), lambda b, r: (ctx0 + b, 0)),
                  pl.BlockSpec((1, NA_HEADS, tq, NA_KROWS * GRID_W), lambda b, r: (variant(r), layer, 0, 0))],
        out_specs=pl.BlockSpec((tq, 256), lambda b, r: (b * nrb + r, 0)),
        out_shape=jax.ShapeDtypeStruct((nb * s_len, 256), F32),
        compiler_params=_cparams(("parallel", "arbitrary"), 56),
        name="na_attn",
    )(qna, kna, vna, kna, vna, bias)


def _na_bias_tables(rpb, rows):
    a = np.arange(NA_QROWS)
    qc = np.arange(GRID_W)
    kr_rel = np.arange(NA_KROWS)
    kc = np.arange(GRID_W)
    col0 = np.clip(qc - NA_KW // 2, 0, GRID_W - NA_KW)
    col_valid = (kc[None, :] >= col0[:, None]) & (kc[None, :] < col0[:, None] + NA_KW)
    col_off = np.clip(kc[None, :] - qc[:, None] + (NA_KW - 1), 0, 2 * NA_KW - 2)
    oh_col = (col_off[:, :, None] == np.arange(2 * NA_KW - 1)).astype(np.float32)
    big = 10 ** 6
    out = []
    for r_blk, k0, rows_eff in ((0, 0, big), (NA_KH, NA_KH // 2, big), (rows - NA_QROWS, rows - NA_KROWS, rows)):
        qr = r_blk + a
        r0 = np.clip(qr - NA_KH // 2, 0, rows_eff - NA_KH)
        kr = k0 + kr_rel
        row_valid = (kr[None, :] >= r0[:, None]) & (kr[None, :] < r0[:, None] + NA_KH)
        row_off = np.clip(kr[None, :] - qr[:, None] + (NA_KH - 1), 0, 2 * NA_KH - 2)
        oh_row = (row_off[:, :, None] == np.arange(2 * NA_KH - 1)).astype(np.float32)
        b = jnp.einsum('akr,hrc,qlc->haqkl', oh_row, rpb, oh_col, precision=lax.Precision.HIGHEST)
        valid = row_valid[:, None, :, None] & col_valid[None, :, None, :]
        b = jnp.where(valid[None], b * LOG2E, NEG)
        out.append(b.reshape(rpb.shape[0], NA_QROWS * GRID_W, NA_KROWS * GRID_W))
    return jnp.stack(out)


def _s5_kernel(*refs, nb):
    uf_refs, ub_refs = refs[:nb], refs[nb:2 * nb]
    (bf_ref, bb_ref, a_ref, cf_ref, cb_ref, d_ref, yf_ref, yb_ref,
     uf_sc, ub_sc, buf_sc, bub_sc, hf_sc, hb_sc) = refs[2 * nb:]
    j = pl.program_id(0)
    tc = uf_refs[0].shape[0]
    half = S5_GROUPS * S5_STATE

    @pl.when(j == 0)
    def _():
        hf_sc[...] = jnp.zeros(hf_sc.shape, F32)
        hb_sc[...] = jnp.zeros(hb_sc.shape, F32)

    for b in range(nb):
        for hv in range(2):
            uf_sc[hv, pl.ds(b, tc, stride=nb), :] = uf_refs[b][:, hv * LANES:(hv + 1) * LANES]
            ub_sc[hv, pl.ds(b, tc, stride=nb), :] = ub_refs[b][:, hv * LANES:(hv + 1) * LANES]

    def scan(u_sc, b_ref, c_ref, y_ref, bu_sc, h_sc, d, reverse):
        u = jnp.concatenate([u_sc[0], u_sc[1]], axis=1)
        bu_sc[...] = _dot(u.astype(BF16), b_ref[...])
        ar = jnp.broadcast_to(a_ref[d, 0:1, :], (nb, half))
        ai = jnp.broadcast_to(a_ref[d, 1:2, :], (nb, half))

        def step(t, carry):
            hr, hi = carry
            tt = (tc - 1 - t) if reverse else t
            row = pl.multiple_of(tt * nb, nb)
            nr = ar * hr - ai * hi + bu_sc[pl.ds(row, nb), 0:half]
            ni = ar * hi + ai * hr + bu_sc[pl.ds(row, nb), half:2 * half]
            bu_sc[pl.ds(row, nb), 0:half] = nr
            bu_sc[pl.ds(row, nb), half:2 * half] = ni
            return nr, ni

        hr, hi = lax.fori_loop(0, tc, step, (h_sc[:, 0:half], h_sc[:, half:2 * half]), unroll=True)
        h_sc[:, 0:half] = hr
        h_sc[:, half:2 * half] = hi
        y = _dot(bu_sc[...].astype(BF16), c_ref[...])
        y_ref[...] = y if reverse else y + d_ref[...] * u

    scan(uf_sc, bf_ref, cf_ref, yf_ref, buf_sc, hf_sc, 0, False)
    scan(ub_sc, bb_ref, cb_ref, yb_ref, bub_sc, hb_sc, 1, True)


def _s5_scan(z, bmat, avec, cmat, d_skip, *, nb, s_len, l_len):
    tc = S5_TC
    cr = tc * nb
    nctx, nlat = l_len // tc, s_len // tc
    nchunk = nctx + nlat
    half = S5_GROUPS * S5_STATE
    n_lat = nb * s_len

    def bwd(j):
        return jnp.where(j < nctx, nctx - 1 - j, nchunk - 1 - (j - nctx))

    def u_spec(b, order):
        def row_block(j):
            c = order(j)
            return jnp.where(c < nctx, (n_lat + b * l_len) // tc + c, (b * s_len) // tc + c - nctx)
        return pl.BlockSpec((tc, 256), lambda j: (row_block(j), 0))

    const = lambda shape: pl.BlockSpec(shape, lambda j: (0,) * len(shape))
    return pl.pallas_call(
        functools.partial(_s5_kernel, nb=nb),
        grid=(nchunk,),
        in_specs=[u_spec(b, lambda j: j) for b in range(nb)] + [u_spec(b, bwd) for b in range(nb)]
                 + [const((256, 2 * half)), const((256, 2 * half)), const((2, 2, half)),
                    const((2 * half, 256)), const((2 * half, 256)), const((1, 256))],
        out_specs=[pl.BlockSpec((cr, 256), lambda j: (j, 0)),
                   pl.BlockSpec((cr, 256), lambda j: (bwd(j), 0))],
        out_shape=[jax.ShapeDtypeStruct((nchunk * cr, 256), F32)] * 2,
        scratch_shapes=[pltpu.VMEM((2, cr, LANES), F32), pltpu.VMEM((2, cr, LANES), F32),
                        pltpu.VMEM((cr, 2 * half), F32), pltpu.VMEM((cr, 2 * half), F32),
                        pltpu.VMEM((nb, 2 * half), F32), pltpu.VMEM((nb, 2 * half), F32)],
        compiler_params=_cparams(("arbitrary",)),
        name="s5_scan",
    )(*([z] * (2 * nb)), bmat[0], bmat[1], avec, cmat[0], cmat[1], d_skip)


def _s5_finish_kernel(yf_ref, yb_ref, w_ref, o_ref, r_sc):
    nb, steps = o_ref.shape[0], o_ref.shape[1]
    r = _dot((yf_ref[...] + yb_ref[...]).astype(BF16), w_ref[...])
    o = r[:, :BRANCH_W] * jax.nn.sigmoid(r[:, BRANCH_W:])
    for hv in range(2):
        r_sc[hv] = o[:, hv * LANES:(hv + 1) * LANES]
    for b in range(nb):
        for hv in range(2):
            o_ref[b, :, hv * LANES:(hv + 1) * LANES] = r_sc[hv, pl.ds(b, steps, stride=nb), :]


def _s5_finish(yf, yb, w_glu, *, nb, t0, t_len):
    steps = TM // nb
    tile0 = t0 // steps
    spec = pl.BlockSpec((TM, 256), lambda i: (tile0 + i, 0))
    out = pl.pallas_call(
        _s5_finish_kernel,
        grid=(t_len // steps,),
        in_specs=[spec, spec, pl.BlockSpec((256, 512), lambda i: (0, 0))],
        out_specs=pl.BlockSpec((nb, steps, 256), lambda i: (0, i, 0)),
        out_shape=jax.ShapeDtypeStruct((nb, t_len, 256), F32),
        scratch_shapes=[pltpu.VMEM((2, TM, LANES), F32)],
        compiler_params=_cparams(("parallel",)),
        name="s5_glu",
    )(yf, yb, w_glu)
    return out.reshape(nb * t_len, 256)


def _s5_params(lam_re, lam_im, log_dt, b_re, b_im, c_re, c_im):
    lam = lax.complex(lam_re, lam_im)
    dt = jnp.exp(log_dt)[..., None]
    a_bar = jnp.exp(lam * dt)
    b_bar = ((a_bar - 1.0) / lam)[..., None] * lax.complex(b_re, b_im)
    eye = jnp.eye(S5_GROUPS, dtype=F32)
    g, p, ch = S5_GROUPS, S5_STATE, S5_GROUP_CH

    def bdiag_in(m):
        return jnp.einsum('gh,dgpc->dgchp', eye, m).reshape(2, g * ch, g * p)

    def bdiag_out(m):
        return jnp.einsum('gh,dgcp->dgphc', eye, m).reshape(2, g * p, g * ch)

    bmat = jnp.concatenate([bdiag_in(jnp.real(b_bar)), bdiag_in(jnp.imag(b_bar))], axis=2).astype(BF16)
    cmat = jnp.concatenate([bdiag_out(c_re), -bdiag_out(c_im)], axis=1).astype(BF16)
    avec = jnp.stack([jnp.real(a_bar).reshape(2, g * p), jnp.imag(a_bar).reshape(2, g * p)], axis=1)
    return bmat, avec, cmat


def _merge_kernel(*refs, n_lat_tiles, has_ctx):
    x_ref, mod_ref = refs[0], refs[1]
    n_br = 8 if has_ctx else 4
    br_refs = refs[2:2 + n_br]
    wg_ref, bg_ref, wb_ref, wo_ref, g_ref, b_ref, x1_ref, h2_ref = refs[2 + n_br:]
    d = D_MODEL
    m = mod_ref[0]
    x = x_ref[...]
    hb = (x * (1.0 + m[1:2]) + m[0:1]).astype(BF16)
    is_ctx = pl.program_id(0) >= n_lat_tiles
    acc = None
    for i in range(4):
        if has_ctx:
            o = jnp.where(is_ctx, br_refs[2 * i + 1][...], br_refs[2 * i][...])
        else:
            o = br_refs[i][...]
        gate = jax.nn.sigmoid(_dot(hb, wg_ref[0, :, i * d:(i + 1) * d]) + bg_ref[:, i * d:(i + 1) * d])
        term = gate * _dot(o.astype(BF16), wb_ref[0, i])
        acc = term if acc is None else acc + term
    y = _dot(acc.astype(BF16), wo_ref[0])
    x1 = _layer_norm(ALPHA * x + m[2:3] * y, g_ref[...], b_ref[...])
    x1_ref[...] = x1
    h2_ref[...] = x1 * (1.0 + m[4:5]) + m[3:4]


def _merge(xa, modl, outs_lat, outs_ctx, wg, bg, wb, wo, g, b, seg, n, n_lat_tiles, layer):
    d = D_MODEL
    tok = pl.BlockSpec((TM, d), lambda i: (i, 0))
    lat = pl.BlockSpec((TM, 256), lambda i: (jnp.minimum(i, n_lat_tiles - 1), 0))
    ctx = pl.BlockSpec((TM, 256), lambda i: (jnp.maximum(i - n_lat_tiles, 0), 0))

    def const(a):
        return pl.BlockSpec(a.shape, lambda i: (0,) * a.ndim)

    def stacked(a):
        return pl.BlockSpec((1,) + a.shape[1:], lambda i: (layer,) + (0,) * (a.ndim - 1))
    has_ctx = outs_ctx is not None
    if has_ctx:
        branches = [a for pair in zip(outs_lat, outs_ctx) for a in pair]
        br_specs = [lat, ctx] * 4
    else:
        branches, br_specs = list(outs_lat), [lat] * 4
    return pl.pallas_call(
        functools.partial(_merge_kernel, n_lat_tiles=n_lat_tiles, has_ctx=has_ctx),
        grid=(n // TM,),
        in_specs=[tok, pl.BlockSpec((1, 6, d), lambda i: (seg(i), 0, 0))] + br_specs
                 + [stacked(wg), const(bg), stacked(wb), stacked(wo), const(g), const(b)],
        out_specs=[tok, tok],
        out_shape=[jax.ShapeDtypeStruct((n, d), F32)] * 2,
        compiler_params=_cparams(("parallel",), 56),
        name="merge_ln1",
    )(xa, modl, *branches, wg, bg, wb, wo, g, b)


def _start_row_gather(src_hbm, dst, idx_ref, base, n_rows, sem):
    def body(r, c):
        pltpu.make_async_copy(src_hbm.at[pl.ds(idx_ref[base + r], 1)], dst.at[pl.ds(r, 1)], sem).start()
        return c

    lax.fori_loop(0, n_rows, body, 0, unroll=8)


def _wait_row_gather(src_hbm, dst, sem):
    pltpu.make_async_copy(src_hbm.at[pl.ds(0, dst.shape[0])], dst, sem).wait()


def _swiglu_block(xb, w1_ref, w3_ref, w2_ref, o_ref, between=None):
    lead = (0,) * (len(w1_ref.shape) - 2)
    n_chunks = D_FF // FF_CHUNK
    for c in range(n_chunks):
        cs = slice(c * FF_CHUNK, (c + 1) * FF_CHUNK)
        a = _dot(xb, w1_ref[lead + (slice(None), cs)])
        b = _dot(xb, w3_ref[lead + (slice(None), cs)])
        g = (a * jax.nn.sigmoid(a) * b).astype(BF16)
        y = _dot(g, w2_ref[lead + (cs, slice(None))])
        if c == 0:
            o_ref[...] = y
        else:
            o_ref[...] += y
        if between is not None:
            between(c, n_chunks)


def _ffn_kernel(h_ref, x_ref, mod_ref, w1_ref, w3_ref, w2_ref, g_ref, b_ref, o_ref, acc_sc):
    _swiglu_block(h_ref[...].astype(BF16), w1_ref, w3_ref, w2_ref, acc_sc)
    m = mod_ref[0]
    o_ref[...] = _layer_norm(ALPHA * x_ref[...] + m[5:6] * acc_sc[...], g_ref[...], b_ref[...])


def _moe_ffn_kernel(be_ref, tok_ref, start_ref, nused_ref, h_hbm, w1_ref, w3_ref, w2_ref, o_ref, xbuf, sem):
    i = pl.program_id(0)
    last = pl.num_programs(0) - 1
    slot = i % 2
    nxt_base = start_ref[jnp.minimum(i + 1, last)]
    nxt_buf, nxt_sem = xbuf.at[1 - slot], sem.at[1 - slot]

    @pl.when(i == 0)
    def _():
        _start_row_gather(h_hbm, xbuf.at[0], tok_ref, start_ref[0], MOE_BLK, sem.at[0])

    _wait_row_gather(h_hbm, xbuf.at[slot], sem.at[slot])

    def start_piece(c, n_chunks):
        per = -(-MOE_BLK // n_chunks)
        for r in range(c * per, min((c + 1) * per, MOE_BLK)):
            pltpu.make_async_copy(h_hbm.at[pl.ds(tok_ref[nxt_base + r], 1)], nxt_buf.at[pl.ds(r, 1)],
                                  nxt_sem).start(priority=r % 2)

    @pl.when(i < nused_ref[0])
    def _():
        _swiglu_block(xbuf[slot].astype(BF16), w1_ref, w3_ref, w2_ref, o_ref, between=start_piece)

    @pl.when(i >= nused_ref[0])
    def _():
        o_ref[...] = jnp.zeros(o_ref.shape, F32)
        _start_row_gather(h_hbm, nxt_buf, tok_ref, nxt_base, MOE_BLK, nxt_sem)

    @pl.when(i == last)
    def _():
        _wait_row_gather(h_hbm, nxt_buf, nxt_sem)


def _moe_ffn(h, block_e, tok_sorted, blk_start, n_used, w1, w3, w2, layer):
    d = h.shape[1]
    n_blocks = block_e.shape[0]
    wspec = lambda shape: pl.BlockSpec(shape, lambda i, be, tok, st, nu: (layer, be[i], 0, 0),
                                       pipeline_mode=pl.Buffered(1))
    grid_spec = pltpu.PrefetchScalarGridSpec(
        num_scalar_prefetch=4,
        grid=(n_blocks,),
        in_specs=[pl.BlockSpec(memory_space=pl.ANY), wspec((1, 1, d, D_FF)), wspec((1, 1, d, D_FF)),
                  wspec((1, 1, D_FF, d))],
        out_specs=pl.BlockSpec((MOE_BLK, d), lambda i, be, tok, st, nu: (i, 0)),
        scratch_shapes=[pltpu.VMEM((2, MOE_BLK, d), F32), pltpu.SemaphoreType.DMA((2,))],
    )
    return pl.pallas_call(
        _moe_ffn_kernel,
        grid_spec=grid_spec,
        out_shape=jax.ShapeDtypeStruct((n_blocks * MOE_BLK, d), F32),
        compiler_params=_cparams(("arbitrary",), 56),
        name="moe_swiglu",
    )(block_e, tok_sorted, blk_start, n_used, h, w1, w3, w2)


def _ffn(h2, x1, modl, w1, w3, w2, g, b, seg, layer):
    n, d = h2.shape
    tok = pl.BlockSpec((TM, d), lambda i: (i, 0))
    vec = pl.BlockSpec((1, d), lambda i: (0, 0))
    wspec = lambda shape: pl.BlockSpec(shape, lambda i: (layer, 0, 0), pipeline_mode=pl.Buffered(1))
    return pl.pallas_call(
        _ffn_kernel,
        grid=(n // TM,),
        in_specs=[tok, tok, pl.BlockSpec((1, 6, d), lambda i: (seg(i), 0, 0)),
                  wspec((1, d, D_FF)), wspec((1, d, D_FF)), wspec((1, D_FF, d)), vec, vec],
        out_specs=tok,
        out_shape=jax.ShapeDtypeStruct((n, d), F32),
        scratch_shapes=[pltpu.VMEM((TM, d), F32)],
        compiler_params=_cparams(("parallel",), 56),
        name="swiglu_ln2",
    )(h2, x1, modl, w1, w3, w2, g, b)


def _router_kernel(h_ref, w_ref, o_ref):
    logits = jnp.dot(h_ref[...], w_ref[...], preferred_element_type=F32, precision=lax.Precision.HIGHEST)
    lane = lax.broadcasted_iota(jnp.int32, logits.shape, 1)
    lg = jnp.where(lane < N_EXPERTS, logits, NEG)
    v1 = jnp.max(lg, axis=-1, keepdims=True)
    i1 = jnp.min(jnp.where(lg == v1, lane, LANES), axis=-1, keepdims=True)
    lg2 = jnp.where(lane == i1, NEG, lg)
    v2 = jnp.max(lg2, axis=-1, keepdims=True)
    i2 = jnp.min(jnp.where(lg2 == v2, lane, LANES), axis=-1, keepdims=True)
    e = jnp.exp(v2 - v1)
    g1 = 1.0 / (1.0 + e)
    g2 = e / (1.0 + e)
    out = jnp.where(lane == 0, i1.astype(F32), jnp.where(lane == 1, i2.astype(F32),
                    jnp.where(lane == 2, g1, jnp.where(lane == 3, g2, 0.0))))
    o_ref[...] = out


def _router(h2, w_router_pad):
    n, d = h2.shape
    return pl.pallas_call(
        _router_kernel,
        grid=(n // TM,),
        in_specs=[pl.BlockSpec((TM, d), lambda i: (i, 0)), pl.BlockSpec((d, LANES), lambda i: (0, 0))],
        out_specs=pl.BlockSpec((TM, LANES), lambda i: (i, 0)),
        out_shape=jax.ShapeDtypeStruct((n, LANES), F32),
        compiler_params=_cparams(("parallel",)),
        name="router",
    )(h2, w_router_pad)


def _moe_combine_kernel(dest_ref, x_ref, mod_ref, gt_ref, y_hbm, g_ref, b_ref, o_ref, ybuf, sem):
    i = pl.program_id(0)
    last = pl.num_programs(0) - 1
    slot = i % 2
    nxt_base = jnp.minimum(i + 1, last) * 2 * TM
    nxt_buf, nxt_sem = ybuf.at[1 - slot], sem.at[1 - slot]

    @pl.when(i == 0)
    def _():
        _start_row_gather(y_hbm, ybuf.at[0], dest_ref, 0, 2 * TM, sem.at[0])

    for r in range(2 * TM):
        pltpu.make_async_copy(y_hbm.at[pl.ds(dest_ref[nxt_base + r], 1)], nxt_buf.at[pl.ds(r, 1)],
                              nxt_sem).start(priority=r % 2)

    _wait_row_gather(y_hbm, ybuf.at[slot], sem.at[slot])
    gt = gt_ref[...]
    f = gt[:, 2:3] * ybuf[slot, 0:TM, :] + gt[:, 3:4] * ybuf[slot, TM:2 * TM, :]
    m = mod_ref[0]
    o_ref[...] = _layer_norm(ALPHA * x_ref[...] + m[5:6] * f, g_ref[...], b_ref[...])

    @pl.when(i == last)
    def _():
        _wait_row_gather(y_hbm, nxt_buf, nxt_sem)


def _moe_combine(x1, modl, y_rows, dest_tiles, gates, g, b, seg):
    n, d = x1.shape
    tok = pl.BlockSpec((TM, d), lambda i, dst: (i, 0))
    vec = pl.BlockSpec((1, d), lambda i, dst: (0, 0))
    grid_spec = pltpu.PrefetchScalarGridSpec(
        num_scalar_prefetch=1,
        grid=(n // TM,),
        in_specs=[tok, pl.BlockSpec((1, 6, d), lambda i, dst: (seg(i), 0, 0)),
                  pl.BlockSpec((TM, LANES), lambda i, dst: (i, 0)), pl.BlockSpec(memory_space=pl.ANY), vec, vec],
        out_specs=tok,
        scratch_shapes=[pltpu.VMEM((2, 2 * TM, d), F32), pltpu.SemaphoreType.DMA((2,))],
    )
    return pl.pallas_call(
        _moe_combine_kernel,
        grid_spec=grid_spec,
        out_shape=jax.ShapeDtypeStruct((n, d), F32),
        compiler_params=_cparams(("arbitrary",)),
        name="moe_combine_ln2",
    )(dest_tiles, x1, modl, gates, y_rows, g, b)


def _rope_tables(s_len, width, rope_lane, extra_rows):
    t = jnp.arange(s_len)
    lane = np.arange(width)
    inv = ROPE_BASE ** (-(jnp.asarray(lane % 8, F32)) / 8.0)
    pos = jnp.where((lane & 16) == 0, (t // GRID_W)[:, None], (t % GRID_W)[:, None]).astype(F32)
    ang = pos * inv[None, :]
    cos = jnp.where(rope_lane[None, :], jnp.cos(ang), 1.0)
    sin = jnp.where(rope_lane[None, :], jnp.where((lane & 8) == 0, -jnp.sin(ang), jnp.sin(ang)), 0.0)
    cos = jnp.concatenate([cos, jnp.ones((extra_rows, width), F32)])
    sin = jnp.concatenate([sin, jnp.zeros((extra_rows, width), F32)])
    return cos, sin


def _pad_heads(w, n_heads, width):
    k = w.shape[0]
    w = w.reshape(k, n_heads, -1)
    return jnp.pad(w, ((0, 0), (0, 0), (0, width - w.shape[-1]))).reshape(k, n_heads * width)


def _route(top, n_tok):
    e_flat = jnp.concatenate([top[:, 0], top[:, 1]]).astype(jnp.int32)
    n_assign = 2 * n_tok
    onehot = (e_flat[:, None] == jnp.arange(N_EXPERTS)[None, :]).astype(jnp.int32)
    csum = jnp.cumsum(onehot, axis=0)
    counts = csum[-1]
    padded = (counts + MOE_BLK - 1) // MOE_BLK * MOE_BLK
    pad_end = jnp.cumsum(padded)
    pad_start = pad_end - padded
    dest = jnp.sum(onehot * (csum - 1 + pad_start[None, :]), axis=1)
    n_blocks = -(-n_assign // MOE_BLK) + N_EXPERTS
    blk_row0 = jnp.arange(n_blocks) * MOE_BLK
    block_e = jnp.minimum(jnp.searchsorted(pad_end, blk_row0, side='right'), N_EXPERTS - 1).astype(jnp.int32)
    n_used = (pad_end[-1:] // MOE_BLK).astype(jnp.int32)
    tok_sorted = jnp.concatenate([(jnp.argsort(e_flat) % n_tok).astype(jnp.int32), jnp.zeros((MOE_BLK,), jnp.int32)])
    seg_start = jnp.cumsum(counts) - counts
    blk_start = jnp.clip(seg_start[block_e] + blk_row0 - pad_start[block_e], 0, n_assign).astype(jnp.int32)
    dest_tiles = dest.astype(jnp.int32).reshape(2, n_tok // TM, TM).transpose(1, 0, 2).reshape(-1)
    return tok_sorted, blk_start, block_e, n_used, dest_tiles


def kernel(x, c, ctx, c_ctx, w_ada, b_ada, w_in, na_rpb, mla_q_norm, mla_kv_norm, mla_w_uq, mla_w_ukv,
           s5_lam_re, s5_lam_im, s5_log_dt, s5_b_re, s5_b_im, s5_c_re, s5_c_im, s5_d, s5_w_glu,
           diff_lam_q1, diff_lam_k1, diff_lam_q2, diff_lam_k2, diff_subln,
           w_branch, w_gate, b_gate, w_out, ln1_g, ln1_b, ln2_g, ln2_b,
           ffn_w1, ffn_w3, ffn_w2, moe_router, moe_w1, moe_w3, moe_w2):
    nb, s_len, d = x.shape
    l_len = ctx.shape[1]
    n_lat, n_ctx = nb * s_len, nb * l_len
    n_tot = n_lat + n_ctx
    rows = s_len // GRID_W
    assert d == D_MODEL and s_len % TM == 0 and n_ctx % TM == 0 and rows % NA_QROWS == 0 and rows >= NA_KROWS
    assert l_len % S5_TC == 0 and s_len % S5_TC == 0

    def seg(i):
        return jnp.minimum((i * TM) // s_len, nb)

    n_lat_tiles = n_lat // TM

    xa = jnp.concatenate([x.reshape(n_lat, d), ctx.reshape(n_ctx, d)], axis=0)
    nrow_mod = -(-(nb + 1) // 16) * 16
    cvec = jnp.concatenate([c, c_ctx[None, :], jnp.zeros((nrow_mod - nb - 1, d), F32)], axis=0)
    mod_all = _ada_all(cvec, w_ada, b_ada).reshape(DEPTH, nrow_mod, 6, d)

    lane512 = np.arange(512)
    q_rope = (lane512 % LANES >= MLA_NOPE) & (lane512 % LANES < MLA_NOPE + MLA_ROPE)
    cq, sq = _rope_tables(s_len, 512, q_rope, TM)
    ck, sk = _rope_tables(s_len, LANES, np.arange(LANES) < MLA_ROPE, TM)
    cd, sd = _rope_tables(s_len, 256, np.ones((256,), bool), TM)
    tabs = (cq, sq, ck, sk, cd, sd)
    place = np.zeros((LANES, MLA_HEADS * LANES), np.float32)
    for h in range(MLA_HEADS):
        place[np.arange(MLA_ROPE), h * LANES + MLA_NOPE + np.arange(MLA_ROPE)] = 1.0
    place = jnp.asarray(place, BF16)

    w_gate_b, w_branch_b, w_out_b = w_gate.astype(BF16), w_branch.astype(BF16), w_out.astype(BF16)
    ffn_b = (ffn_w1.astype(BF16), ffn_w3.astype(BF16), ffn_w2.astype(BF16))
    moe_b = (moe_w1.astype(BF16), moe_w3.astype(BF16), moe_w2.astype(BF16))
    bias_all = _na_bias_tables(na_rpb.reshape(DEPTH * NA_HEADS, 2 * NA_KH - 1, 2 * NA_KW - 1), rows)

    for layer in range(DEPTH):
        ctx_out = layer < DEPTH - 1
        lam_init = 0.8 - 0.6 * math.exp(-0.3 * layer)
        modl = mod_all[layer]

        wi = w_in[layer]
        w_in_pad = jnp.concatenate([wi[:, :1184], jnp.zeros((d, 96), F32), wi[:, 1184:]], axis=1).astype(BF16)
        wuq = _pad_heads(mla_w_uq[layer], MLA_HEADS, LANES).astype(BF16)
        wukv4 = mla_w_ukv[layer].reshape(-1, MLA_HEADS, MLA_NOPE + MLA_V)
        wuk = jnp.pad(wukv4[:, :, :MLA_NOPE], ((0, 0), (0, 0), (0, LANES - MLA_NOPE))).reshape(-1, MLA_HEADS * LANES)
        wuv = wukv4[:, :, MLA_NOPE:].reshape(-1, MLA_HEADS * MLA_V)
        wukv = jnp.concatenate([wuk, wuv], axis=1).astype(BF16)
        bmat, avec, cmat = _s5_params(s5_lam_re[layer], s5_lam_im[layer], s5_log_dt[layer], s5_b_re[layer],
                                      s5_b_im[layer], s5_c_re[layer], s5_c_im[layer])
        lam = (jnp.exp(jnp.sum(diff_lam_q1[layer] * diff_lam_k1[layer]))
               - jnp.exp(jnp.sum(diff_lam_q2[layer] * diff_lam_k2[layer])) + lam_init)
        lam_vec = jnp.full((1, LANES), lam, F32)
        subln = jnp.tile(diff_subln[layer], DIFF_HEADS).reshape(1, 256)

        qm, km, vm, qd, kd, vd, qna, kna, vna, u5 = _proj(
            xa, modl, w_in_pad, tabs, mla_q_norm[layer].reshape(1, -1), mla_kv_norm[layer].reshape(1, -1),
            wuq, wukv, place, nb=nb, s_len=s_len)
        common = dict(nb=nb, s_len=s_len, l_len=l_len)
        o_na = _na_attention(qna, kna, vna, bias_all, layer=layer, **common)
        o_mla = _flash(qm, km, vm, heads=MLA_HEADSPEC, kw=512, name="mla_attn", **common)
        o_diff = _flash(qd, kd, vd, heads=DIFF_HEADSPEC, kw=256, diff=True, lam=lam_vec,
                        subln=subln, lam_init=lam_init, name="diff_attn", **common)
        yf, yb = _s5_scan(u5, bmat, avec, cmat, s5_d[layer].reshape(1, 256), **common)
        w_glu = s5_w_glu[layer].astype(BF16)
        o_s5_lat = _s5_finish(yf, yb, w_glu, nb=nb, t0=l_len, t_len=s_len)
        if ctx_out:
            o_na_c = _flash(qna, kna, vna, heads=NA_HEADSPEC, kw=256, with_lat=False, name="na_ctx_attn", **common)
            o_mla_c = _flash(qm, km, vm, heads=MLA_HEADSPEC, kw=512, with_lat=False,
                             name="mla_ctx_attn", **common)
            o_diff_c = _flash(qd, kd, vd, heads=DIFF_HEADSPEC, kw=256, diff=True, lam=lam_vec,
                              subln=subln, lam_init=lam_init, with_lat=False, name="diff_ctx_attn", **common)
            o_s5_c = _s5_finish(yf, yb, w_glu, nb=nb, t0=0, t_len=l_len)
            outs_ctx = [o_na_c, o_mla_c, o_s5_c, o_diff_c]
            n_act = n_tot
        else:
            outs_ctx = None
            n_act = n_lat
        x1, h2 = _merge(xa, modl, [o_na, o_mla, o_s5_lat, o_diff], outs_ctx,
                        w_gate_b, b_gate[layer].reshape(1, -1), w_branch_b, w_out_b,
                        ln1_g[layer].reshape(1, d), ln1_b[layer].reshape(1, d), seg, n_act, n_lat_tiles, layer)

        jj = layer // 2
        g2, b2 = ln2_g[layer].reshape(1, d), ln2_b[layer].reshape(1, d)
        if layer % 2 == 0:
            xa = _ffn(h2, x1, modl, *ffn_b, g2, b2, seg, jj)
        else:
            wr = jnp.pad(moe_router[jj], ((0, 0), (0, LANES - N_EXPERTS)))
            top = _router(h2, wr)
            tok_sorted, blk_start, block_e, n_used, dest_tiles = _route(top, n_act)
            y_rows = _moe_ffn(h2, block_e, tok_sorted, blk_start, n_used, *moe_b, jj)
            xa = _moe_combine(x1, modl, y_rows, dest_tiles, top, g2, b2, seg)
    return xa[:n_lat].reshape(nb, s_len, d)
```
